```python
import math
import jax, jax.numpy as jnp
from jax import lax
import numpy as np

D_MODEL = 1024
BATCH = 8
SEQ = 2048
DEPTH = 2

HEAD_DIM = 64
A_GROUPS = 4
A_WIDTH = A_GROUPS * HEAD_DIM
A_CHUNK = 128
B_HEADS = 4
B_WIDTH = B_HEADS * HEAD_DIM
RET_CHUNK = 128
ROPE_BASE = 10000.0
C_HEADS = 8
C_KV_HEADS = 2
C_WIDTH = C_HEADS * HEAD_DIM
C_KV_WIDTH = C_KV_HEADS * HEAD_DIM
CMP_BLOCK = 32
CMP_STRIDE = 16
SEL_BLOCK = 64
N_SEL = 8
WINDOW = 512
Q_BLOCK = 128
D_MIX = A_WIDTH + B_WIDTH + C_WIDTH
IN_COLS = 2 * A_WIDTH + 4 * B_WIDTH + C_WIDTH + 6 * C_KV_WIDTH + 3 * C_HEADS
D_FF = 2816
CONV_WIDTH = 3
DN_ALPHA = (2 * DEPTH) ** 0.25
DN_BETA = (8 * DEPTH) ** -0.25
LN_EPS = 1e-5
NEG = -1e30
BIG = 1e30

kernel_name = 'hymba_style_gmlp_retnet_nsa_deepnorm_block'


def layer_norm(x, g, b):
    xf = x.astype(jnp.float32)
    mu = jnp.mean(xf, axis=-1, keepdims=True)
    var = jnp.mean(jnp.square(xf - mu), axis=-1, keepdims=True)
    y = (xf - mu) * lax.rsqrt(var + LN_EPS)
    return (y * g.astype(jnp.float32) + b.astype(jnp.float32)).astype(x.dtype)


def masked_softmax(scores, mask):
    s = jnp.where(mask, scores.astype(jnp.float32), NEG)
    return jax.nn.softmax(s, axis=-1) * mask


def rotary(x, pos):
    half = HEAD_DIM // 2
    inv = jnp.power(ROPE_BASE, -jnp.arange(half, dtype=jnp.float32) / half)
    ang = pos.astype(jnp.float32)[:, None] * inv[None, :]
    cos = jnp.cos(ang)[None, :, None, :].astype(x.dtype)
    sin = jnp.sin(ang)[None, :, None, :].astype(x.dtype)
    x1, x2 = x[..., :half], x[..., half:]
    return jnp.concatenate([x1 * cos - x2 * sin, x1 * sin + x2 * cos], axis=-1)


def spatial_gating_mixer(z, ln_g, ln_b, w_s, b_s):
    Bn, S, _ = z.shape
    z = jax.nn.gelu(z)
    u, v = jnp.split(z, 2, axis=-1)
    v = layer_norm(v.reshape(Bn, S, A_GROUPS, HEAD_DIM), ln_g, ln_b)
    nch = S // A_CHUNK
    v = v.reshape(Bn, nch, A_CHUNK, A_GROUPS, HEAD_DIM)
    causal = jnp.tril(jnp.ones((A_CHUNK, A_CHUNK), dtype=bool))
    w = jnp.where(causal, w_s, 0).astype(v.dtype)
    vs = jnp.einsum('gts,bcsgd->bctgd', w, v) + b_s.T[None, None, :, :, None]
    return u * vs.reshape(Bn, S, A_WIDTH)


def retention_mixer(q, k, v, g, gn_g, gn_b):
    Bn, S, _ = q.shape
    H, d, L = B_HEADS, HEAD_DIM, RET_CHUNK
    nch = S // L
    dt = q.dtype
    pos = jnp.arange(S)
    q = rotary(q.reshape(Bn, S, H, d), pos)
    k = rotary(k.reshape(Bn, S, H, d), pos) * (d ** -0.5)
    v = v.reshape(Bn, S, H, d)
    log_gamma = jnp.log1p(-jnp.exp2(-5.0 - jnp.arange(H, dtype=jnp.float32)))
    idx = jnp.arange(L, dtype=jnp.float32)
    diff = idx[:, None] - idx[None, :]
    decay_in = jnp.where(diff >= 0, jnp.exp(log_gamma[:, None, None] * jnp.maximum(diff, 0.0)), 0.0)
    xi = jnp.exp(log_gamma[:, None] * (idx + 1.0))
    zeta = jnp.exp(log_gamma[:, None] * (L - 1.0 - idx))
    chunk_decay = jnp.exp(log_gamma * L).astype(dt)[None, :, None, None]
    qc = q.reshape(Bn, nch, L, H, d)
    kc = k.reshape(Bn, nch, L, H, d)
    vc = v.reshape(Bn, nch, L, H, d)
    scores = jnp.einsum('bclhd,bcmhd->bchlm', qc, kc) * decay_in.astype(dt)
    o_inner = jnp.einsum('bchlm,bcmhe->bclhe', scores, vc)
    kv = jnp.einsum('bcmhd,bcmhe->bchde', kc * zeta.T[:, :, None].astype(dt), vc)

    def step(state, kv_c):
        return state * chunk_decay + kv_c, state

    state0 = jnp.zeros((Bn, H, d, d), dt)
    _, states = lax.scan(step, state0, jnp.moveaxis(kv, 1, 0))
    states = jnp.moveaxis(states, 0, 1)
    o_cross = jnp.einsum('bclhd,bchde->bclhe', qc, states) * xi.T[:, :, None].astype(dt)
    o = (o_inner + o_cross).reshape(Bn, S, H, d)
    o = layer_norm(o, gn_g, gn_b)
    return jax.nn.silu(g) * o.reshape(Bn, S, B_WIDTH)


def compress(kv, pos_emb, w1, w2):
    Bn, S, G, d = kv.shape
    nc = (S - CMP_BLOCK) // CMP_STRIDE + 1
    idx = (jnp.arange(nc) * CMP_STRIDE)[:, None] + jnp.arange(CMP_BLOCK)[None, :]
    blocks = kv[:, idx] + pos_emb[None, None, :, None, :]
    flat = jnp.moveaxis(blocks, 3, 2).reshape(Bn, nc, G, CMP_BLOCK * d)
    return jax.nn.gelu(flat @ w1) @ w2


def cmp_to_sel_overlap(nc, ns):
    c0 = np.arange(nc)[:, None] * CMP_STRIDE
    s0 = np.arange(ns)[None, :] * SEL_BLOCK
    ov = np.clip(np.minimum(c0 + CMP_BLOCK, s0 + SEL_BLOCK) - np.maximum(c0, s0), 0, None)
    return (ov / CMP_BLOCK).astype(np.float32)


def nsa_mixer(q, k_c, v_c, k_s, v_s, k_w, v_w, gates, pos_k, w1_k, w2_k, pos_v, w1_v, w2_v):
    Bn, S, _ = q.shape
    G, d = C_KV_HEADS, HEAD_DIM
    R = C_HEADS // C_KV_HEADS
    dt = q.dtype
    q = q.reshape(Bn, S, G, R, d) * (d ** -0.5)
    k_cmp = compress(k_c.reshape(Bn, S, G, d), pos_k, w1_k, w2_k)
    v_cmp = compress(v_c.reshape(Bn, S, G, d), pos_v, w1_v, w2_v)
    nc = k_cmp.shape[1]
    ns = S // SEL_BLOCK
    n_sel = min(N_SEL, ns)
    overlap = jnp.asarray(cmp_to_sel_overlap(nc, ns))
    cmp_end = jnp.arange(nc) * CMP_STRIDE + CMP_BLOCK - 1
    ks_blocks = jnp.moveaxis(k_s.reshape(Bn, ns, SEL_BLOCK, G, d), 3, 1)
    vs_blocks = jnp.moveaxis(v_s.reshape(Bn, ns, SEL_BLOCK, G, d), 3, 1)
    pad = jnp.zeros((Bn, WINDOW, G, d), k_w.dtype)
    kw_pad = jnp.concatenate([pad, k_w.reshape(Bn, S, G, d)], axis=1)
    vw_pad = jnp.concatenate([pad, v_w.reshape(Bn, S, G, d)], axis=1)
    gates = jax.nn.sigmoid(gates.reshape(Bn, S, G, R, 3))
    b_ix = jnp.arange(Bn)[:, None, None, None]
    g_ix = jnp.arange(G)[None, :, None, None]
    blk = jnp.arange(ns)
    span = n_sel * SEL_BLOCK

    def query_block(i):
        s0 = i * Q_BLOCK
        t = s0 + jnp.arange(Q_BLOCK)
        qb = lax.dynamic_slice_in_dim(q, s0, Q_BLOCK, axis=1)
        sc = jnp.einsum('bqgrd,bkgd->bgrqk', qb, k_cmp)
        p_cmp = masked_softmax(sc, cmp_end[None, :] <= t[:, None])
        o_cmp = jnp.einsum('bgrqk,bkgd->bqgrd', p_cmp.astype(dt), v_cmp)
        imp = jnp.einsum('bgrqk,kj->bgqj', p_cmp, overlap)
        cur = t // SEL_BLOCK
        future = blk[None, :] > cur[:, None]
        forced = (blk[None, :] == 0) | (blk[None, :] == cur[:, None]) | (blk[None, :] == cur[:, None] - 1)
        imp = jnp.where(forced, BIG, jnp.where(future, NEG, imp))
        _, sel = lax.top_k(imp, n_sel)
        k_sel = ks_blocks[b_ix, g_ix, sel]
        v_sel = vs_blocks[b_ix, g_ix, sel]
        key_pos = sel[..., None] * SEL_BLOCK + jnp.arange(SEL_BLOCK)
        m_sel = (key_pos <= t[None, None, :, None, None]).reshape(Bn, G, 1, Q_BLOCK, span)
        ss = jnp.einsum('bqgrd,bgqnld->bgrqnl', qb, k_sel).reshape(Bn, G, R, Q_BLOCK, span)
        p_sel = masked_softmax(ss, m_sel).reshape(Bn, G, R, Q_BLOCK, n_sel, SEL_BLOCK)
        o_sel = jnp.einsum('bgrqnl,bgqnld->bqgrd', p_sel.astype(dt), v_sel)
        kwb = lax.dynamic_slice_in_dim(kw_pad, s0, WINDOW + Q_BLOCK, axis=1)
        vwb = lax.dynamic_slice_in_dim(vw_pad, s0, WINDOW + Q_BLOCK, axis=1)
        kpos = s0 - WINDOW + jnp.arange(WINDOW + Q_BLOCK)
        rel = t[:, None] - kpos[None, :]
        m_win = (kpos[None, :] >= 0) & (rel >= 0) & (rel < WINDOW)
        sw = jnp.einsum('bqgrd,bkgd->bgrqk', qb, kwb)
        p_win = masked_softmax(sw, m_win)
        o_win = jnp.einsum('bgrqk,bkgd->bqgrd', p_win.astype(dt), vwb)
        gb = lax.dynamic_slice_in_dim(gates, s0, Q_BLOCK, axis=1)
        o = gb[..., 0:1] * o_cmp + gb[..., 1:2] * o_sel + gb[..., 2:3] * o_win
        return o.reshape(Bn, Q_BLOCK, C_WIDTH)

    out = lax.map(query_block, jnp.arange(S // Q_BLOCK))
    return jnp.moveaxis(out, 0, 1).reshape(Bn, S, C_WIDTH)


def hybrid_mixer(h, w_in, a_ln_g, a_ln_b, a_ws, a_bs, b_gn_g, b_gn_b,
                 c_pos_k, c_w1_k, c_w2_k, c_pos_v, c_w1_v, c_w2_v, w_out):
    p = h @ w_in
    widths = [2 * A_WIDTH, B_WIDTH, B_WIDTH, B_WIDTH, B_WIDTH, C_WIDTH] + [C_KV_WIDTH] * 6
    splits = [int(s) for s in np.cumsum(widths)]
    za, qb, kb, vb, gb, qc, kcm, vcm, ksl, vsl, kwn, vwn, gc = jnp.split(p, splits, axis=-1)
    y_a = spatial_gating_mixer(za, a_ln_g, a_ln_b, a_ws, a_bs)
    y_b = retention_mixer(qb, kb, vb, gb, b_gn_g, b_gn_b)
    y_c = nsa_mixer(qc, kcm, vcm, ksl, vsl, kwn, vwn, gc,
                    c_pos_k, c_w1_k, c_w2_k, c_pos_v, c_w1_v, c_w2_v)
    return jnp.concatenate([y_a, y_b, y_c], axis=-1) @ w_out


def conv_ffn(h, w_up, conv_w, conv_b, w_down):
    a = h @ w_up
    ch = a.shape[-1]
    a = lax.conv_general_dilated(a, conv_w[:, None, :].astype(a.dtype), window_strides=(1,),
                                 padding=[(CONV_WIDTH - 1, 0)],
                                 dimension_numbers=('NWC', 'WIO', 'NWC'),
                                 feature_group_count=ch) + conv_b
    gate, up = jnp.split(a, 2, axis=-1)
    return (jax.nn.silu(gate) * up) @ w_down


def setup_inputs(seed: int = 0) -> dict:
    key = jax.random.key(seed)
    ks = jax.random.split(key, 26)
    n = lambda k, shape: jax.random.normal(k, shape, jnp.float32)
    L = DEPTH
    return {
        'x': n(ks[0], (BATCH, SEQ, D_MODEL)),
        'c': n(ks[1], (BATCH, D_MODEL)),
        'w_ada': n(ks[2], (L, D_MODEL, 6 * D_MODEL)) * D_MODEL ** -0.5,
        'b_ada': n(ks[3], (L, 6 * D_MODEL)) * 0.01,
        'w_in': n(ks[4], (L, D_MODEL, IN_COLS)) * D_MODEL ** -0.5,
        'a_ln_g': 1.0 + 0.02 * n(ks[5], (L, A_GROUPS, HEAD_DIM)),
        'a_ln_b': 0.02 * n(ks[6], (L, A_GROUPS, HEAD_DIM)),
        'a_ws': n(ks[7], (L, A_GROUPS, A_CHUNK, A_CHUNK)) * A_CHUNK ** -0.5,
        'a_bs': 1.0 + 0.02 * n(ks[8], (L, A_GROUPS, A_CHUNK)),
        'b_gn_g': 1.0 + 0.02 * n(ks[9], (L, B_HEADS, HEAD_DIM)),
        'b_gn_b': 0.02 * n(ks[10], (L, B_HEADS, HEAD_DIM)),
        'c_pos_k': 0.02 * n(ks[11], (L, CMP_BLOCK, HEAD_DIM)),
        'c_w1_k': n(ks[12], (L, CMP_BLOCK * HEAD_DIM, HEAD_DIM)) * (CMP_BLOCK * HEAD_DIM) ** -0.5,
        'c_w2_k': n(ks[13], (L, HEAD_DIM, HEAD_DIM)) * HEAD_DIM ** -0.5,
        'c_pos_v': 0.02 * n(ks[14], (L, CMP_BLOCK, HEAD_DIM)),
        'c_w1_v': n(ks[15], (L, CMP_BLOCK * HEAD_DIM, HEAD_DIM)) * (CMP_BLOCK * HEAD_DIM) ** -0.5,
        'c_w2_v': n(ks[16], (L, HEAD_DIM, HEAD_DIM)) * HEAD_DIM ** -0.5,
        'w_out': n(ks[17], (L, D_MIX, D_MODEL)) * (D_MIX ** -0.5) * DN_BETA,
        'ln1_g': 1.0 + 0.02 * n(ks[18], (L, D_MODEL)),
        'ln1_b': 0.02 * n(ks[19], (L, D_MODEL)),
        'w_up': n(ks[20], (L, D_MODEL, 2 * D_FF)) * D_MODEL ** -0.5,
        'conv_w': n(ks[21], (L, CONV_WIDTH, 2 * D_FF)) * CONV_WIDTH ** -0.5,
        'conv_b': 0.02 * n(ks[22], (L, 2 * D_FF)),
        'w_down': n(ks[23], (L, D_FF, D_MODEL)) * (D_FF ** -0.5) * DN_BETA,
        'ln2_g': 1.0 + 0.02 * n(ks[24], (L, D_MODEL)),
        'ln2_b': 0.02 * n(ks[25], (L, D_MODEL)),
    }


def reference(x, c, w_ada, b_ada, w_in, a_ln_g, a_ln_b, a_ws, a_bs, b_gn_g, b_gn_b,
              c_pos_k, c_w1_k, c_w2_k, c_pos_v, c_w1_v, c_w2_v, w_out, ln1_g, ln1_b,
              w_up, conv_w, conv_b, w_down, ln2_g, ln2_b):
    cond = jax.nn.silu(c)
    for l in range(DEPTH):
        mod = cond @ w_ada[l] + b_ada[l]
        sh1, sc1, g1, sh2, sc2, g2 = [m[:, None, :] for m in jnp.split(mod, 6, axis=-1)]
        h = x * (1 + sc1) + sh1
        y = hybrid_mixer(h, w_in[l], a_ln_g[l], a_ln_b[l], a_ws[l], a_bs[l], b_gn_g[l], b_gn_b[l],
                         c_pos_k[l], c_w1_k[l], c_w2_k[l], c_pos_v[l], c_w1_v[l], c_w2_v[l], w_out[l])
        x = layer_norm(DN_ALPHA * x + g1 * y, ln1_g[l], ln1_b[l])
        h = x * (1 + sc2) + sh2
        y = conv_ffn(h, w_up[l], conv_w[l], conv_b[l], w_down[l])
        x = layer_norm(DN_ALPHA * x + g2 * y, ln2_g[l], ln2_b[l])
    return x
```

```python
import functools

import numpy as np
import jax
import jax.numpy as jnp
from jax import lax
from jax.experimental import pallas as pl
from jax.experimental.pallas import tpu as pltpu

F32 = jnp.float32
BF16 = jnp.bfloat16

HEAD_DIM = 64
A_GROUPS = 4
A_WIDTH = A_GROUPS * HEAD_DIM
CHUNK = 128
B_HEADS = 4
B_WIDTH = B_HEADS * HEAD_DIM
ROPE_BASE = 10000.0
C_HEADS = 8
C_KV_HEADS = 2
C_REP = C_HEADS // C_KV_HEADS
C_WIDTH = C_HEADS * HEAD_DIM
C_KV_WIDTH = C_KV_HEADS * HEAD_DIM
CMP_BLOCK = 32
CMP_STRIDE = 16
SEL_BLOCK = 64
N_SEL = 8
WINDOW = 512
CONV_WIDTH = 3
LN_EPS = 1e-5
NEG = -1e30
BIG = 1e30

LANES = 128
SUBLANES = 8
VMEM_BYTES = 64 * 1024 * 1024

PN_RET, PN_ZA, PN_KS, PN_KW, PN_KVC = 0, 1024, 1536, 1664, 1792
PN_COLS = 1792
PT_Q, PT_VS, PT_VW, PT_G = 0, 512, 640, 768
PT_ROWS = 800


def _dot(a, b):
    return jnp.dot(a, b, preferred_element_type=F32)


def _dot_nt(a, b):
    return lax.dot_general(a, b, (((1,), (1,)), ((), ())), preferred_element_type=F32)


def _dot_tn(a, b):
    return lax.dot_general(a, b, (((0,), (0,)), ((), ())), preferred_element_type=F32)


def _split_dot(x, m):
    hi = x.astype(BF16)
    lo = (x - hi.astype(F32)).astype(BF16)
    return _dot(hi, m) + _dot(lo, m)


def _split_dot_left(m, x):
    hi = x.astype(BF16)
    lo = (x - hi.astype(F32)).astype(BF16)
    return _dot(m, hi) + _dot(m, lo)


def _group_ln(x, mavg, g, b):
    mu = _split_dot(x, mavg)
    d = x - mu
    var = _split_dot(d * d, mavg)
    return d * lax.rsqrt(var + LN_EPS) * g + b


def _row_ln(x, g, b):
    mu = jnp.mean(x, axis=-1, keepdims=True)
    d = x - mu
    var = jnp.mean(d * d, axis=-1, keepdims=True)
    return d * lax.rsqrt(var + LN_EPS) * g + b


def _params(sem, vmem_mb):
    return pltpu.CompilerParams(dimension_semantics=sem, vmem_limit_bytes=vmem_mb * 1024 * 1024)


def _ada_kernel(c_ref, w_ref, b_ref, o_ref):
    cond = jax.nn.silu(c_ref[...]).astype(BF16)
    o_ref[0] = _dot(cond, w_ref[0].astype(BF16)) + b_ref[0]


def _ada_mod(c, w_ada, b_ada):
    depth, d, n = w_ada.shape
    bsz = c.shape[0]
    tn = 1536
    return pl.pallas_call(
        _ada_kernel,
        grid=(depth, n // tn),
        in_specs=[
            pl.BlockSpec((bsz, d), lambda l, j: (0, 0)),
            pl.BlockSpec((1, d, tn), lambda l, j: (l, 0, j)),
            pl.BlockSpec((1, 1, tn), lambda l, j: (l, 0, j)),
        ],
        out_specs=pl.BlockSpec((1, bsz, tn), lambda l, j: (l, 0, j)),
        out_shape=jax.ShapeDtypeStruct((depth, bsz, n), F32),
        compiler_params=_params(("arbitrary", "arbitrary"), 40),
        name="ada_mod",
    )(c, w_ada, b_ada.reshape(depth, 1, n))


def _inproj_kernel(x_ref, mod_ref, wn_ref, wt_ref, pn_ref, kvc_ref, pt_ref, *, d):
    sh = mod_ref[0, :, 0:d]
    sc = mod_ref[0, :, d:2 * d]
    h = (x_ref[0] * (1.0 + sc) + sh).astype(BF16)
    pn = _dot(h, wn_ref[...])
    pn_ref[0] = pn[:, :PN_COLS]
    kvc_ref[0] = pn[:, PN_KVC:]
    pt_ref[0] = _dot_nt(wt_ref[...], h)


def _inproj(x, mod, wn, wt, tm=512):
    bsz, s, d = x.shape
    const = lambda b, j: (0, 0)
    return pl.pallas_call(
        functools.partial(_inproj_kernel, d=d),
        grid=(bsz, s // tm),
        in_specs=[
            pl.BlockSpec((1, tm, d), lambda b, j: (b, j, 0)),
            pl.BlockSpec((1, 1, 6 * d), lambda b, j: (b, 0, 0)),
            pl.BlockSpec(wn.shape, const),
            pl.BlockSpec(wt.shape, const),
        ],
        out_specs=[
            pl.BlockSpec((1, tm, PN_COLS), lambda b, j: (b, j, 0)),
            pl.BlockSpec((1, tm, 2 * C_KV_WIDTH), lambda b, j: (b, j, 0)),
            pl.BlockSpec((1, PT_ROWS, tm), lambda b, j: (b, 0, j)),
        ],
        out_shape=[
            jax.ShapeDtypeStruct((bsz, s, PN_COLS), F32),
            jax.ShapeDtypeStruct((bsz, s, 2 * C_KV_WIDTH), F32),
            jax.ShapeDtypeStruct((bsz, PT_ROWS, s), F32),
        ],
        compiler_params=_params(("arbitrary", "arbitrary"), 48),
        name="inproj",
    )(x, mod, wn, wt)


def _mixer_a_kernel(za_ref, w_ref, bias_ref, g_ref, b_ref, mavg_ref, gm_ref, o_ref, *, n_chunks):
    row = lax.broadcasted_iota(jnp.int32, (CHUNK, A_GROUPS * CHUNK), 0)
    col = lax.broadcasted_iota(jnp.int32, (CHUNK, A_GROUPS * CHUNK), 1)
    wc = jnp.where((col % CHUNK) <= row, w_ref[...], 0.0).astype(BF16)
    for c in range(n_chunks):
        rows = slice(c * CHUNK, (c + 1) * CHUNK)
        z = jax.nn.gelu(za_ref[0, rows, :])
        u = z[:, :A_WIDTH]
        v = z[:, A_WIDTH:]
        vn = _group_ln(v, mavg_ref[...], g_ref[...], b_ref[...])
        vstack = jnp.concatenate([vn * gm_ref[g:g + 1, :] for g in range(A_GROUPS)], axis=0).astype(BF16)
        vs = _dot(wc, vstack) + bias_ref[...]
        o_ref[0, rows, :] = u * vs


def _mixer_a(pn, wcat, bias, g, b, mavg, gm, tb=512):
    bsz, s, _ = pn.shape
    const = lambda bi, j: (0, 0)
    return pl.pallas_call(
        functools.partial(_mixer_a_kernel, n_chunks=tb // CHUNK),
        grid=(bsz, s // tb),
        in_specs=[
            pl.BlockSpec((1, tb, 2 * A_WIDTH), lambda bi, j: (bi, j, PN_ZA // (2 * A_WIDTH))),
            pl.BlockSpec(wcat.shape, const),
            pl.BlockSpec(bias.shape, const),
            pl.BlockSpec(g.shape, const),
            pl.BlockSpec(b.shape, const),
            pl.BlockSpec(mavg.shape, const),
            pl.BlockSpec(gm.shape, const),
        ],
        out_specs=pl.BlockSpec((1, tb, A_WIDTH), lambda bi, j: (bi, j, 0)),
        out_shape=jax.ShapeDtypeStruct((bsz, s, A_WIDTH), F32),
        compiler_params=_params(("arbitrary", "arbitrary"), 32),
        name="mixer_a",
    )(pn, wcat, bias, g, b, mavg, gm)


def _retention_kernel(x_ref, cos_ref, sin_ref, dec_ref, zeta_ref, xi_ref, cd_ref, qm_ref, vm_ref, bm_ref,
                      mavg_ref, g_ref, b_ref, o_ref, state_ref, *, n_chunks):
    @pl.when(pl.program_id(1) == 0)
    def _():
        state_ref[...] = jnp.zeros_like(state_ref)

    half = B_WIDTH // 2
    for c in range(n_chunks):
        rows = slice(c * CHUNK, (c + 1) * CHUNK)
        q = x_ref[0, rows, 0:B_WIDTH]
        k = x_ref[0, rows, B_WIDTH:2 * B_WIDTH]
        v = x_ref[0, rows, 2 * B_WIDTH:3 * B_WIDTH]
        gate = x_ref[0, rows, 3 * B_WIDTH:4 * B_WIDTH]
        cos = cos_ref[rows, :]
        sin = sin_ref[rows, :]

        def rot(t):
            t1 = t[:, :half]
            t2 = t[:, half:]
            return jnp.concatenate([t1 * cos - t2 * sin, t1 * sin + t2 * cos], axis=1)

        qr = rot(q)
        kr = rot(k) * (HEAD_DIM ** -0.5)
        qs = jnp.concatenate([qr * qm_ref[h:h + 1, :] for h in range(B_HEADS)], axis=0).astype(BF16)
        s = _dot_nt(qs, kr.astype(BF16)) * dec_ref[...]
        scat = jnp.concatenate([s[h * CHUNK:(h + 1) * CHUNK, :] for h in range(B_HEADS)], axis=1).astype(BF16)
        vstack = jnp.concatenate([v * vm_ref[h:h + 1, :] for h in range(B_HEADS)], axis=0).astype(BF16)
        o_inner = _dot(scat, vstack)
        state = state_ref[...]
        o_cross = _dot(qr.astype(BF16), state.astype(BF16)) * xi_ref[...]
        kz = (kr * zeta_ref[...]).astype(BF16)
        kv = _dot_tn(kz, v.astype(BF16)) * bm_ref[...]
        state_ref[...] = state * cd_ref[...] + kv
        o = _group_ln(o_inner + o_cross, mavg_ref[...], g_ref[...], b_ref[...])
        o_ref[0, rows, :] = jax.nn.silu(gate) * o


def _retention(pn, cos, sin, consts, mavg, g, b, tb=512):
    bsz, s, _ = pn.shape
    const = lambda bi, j: (0, 0)
    return pl.pallas_call(
        functools.partial(_retention_kernel, n_chunks=tb // CHUNK),
        grid=(bsz, s // tb),
        in_specs=[
            pl.BlockSpec((1, tb, 4 * B_WIDTH), lambda bi, j: (bi, j, PN_RET // (4 * B_WIDTH))),
            pl.BlockSpec((tb, B_WIDTH // 2), lambda bi, j: (j, 0)),
            pl.BlockSpec((tb, B_WIDTH // 2), lambda bi, j: (j, 0)),
        ] + [pl.BlockSpec(a.shape, const) for a in consts] + [
            pl.BlockSpec(mavg.shape, const),
            pl.BlockSpec(g.shape, const),
            pl.BlockSpec(b.shape, const),
        ],
        out_specs=pl.BlockSpec((1, tb, B_WIDTH), lambda bi, j: (bi, j, 0)),
        out_shape=jax.ShapeDtypeStruct((bsz, s, B_WIDTH), F32),
        scratch_shapes=[pltpu.VMEM((B_WIDTH, B_WIDTH), F32)],
        compiler_params=_params(("arbitrary", "arbitrary"), 32),
        name="retention",
    )(pn, cos, sin, *consts, mavg, g, b)


def _retention_consts():
    h_n, d, l_n = B_HEADS, HEAD_DIM, CHUNK
    log_gamma = jnp.log1p(-jnp.exp2(-5.0 - jnp.arange(h_n, dtype=F32)))
    idx = jnp.arange(l_n, dtype=F32)
    diff = idx[:, None] - idx[None, :]
    decay_in = jnp.where(diff >= 0, jnp.exp(log_gamma[:, None, None] * jnp.maximum(diff, 0.0)), 0.0)
    xi = jnp.exp(log_gamma[:, None] * (idx + 1.0))
    zeta = jnp.exp(log_gamma[:, None] * (l_n - 1.0 - idx))
    chunk_decay = jnp.exp(log_gamma * l_n)
    cols = np.arange(B_WIDTH)
    head_perm = (cols % (B_WIDTH // 2)) // (d // 2)
    head_std = cols // d
    dec = decay_in.reshape(h_n * l_n, l_n)
    zeta_t = zeta.T[:, head_perm]
    xi_t = xi.T[:, head_std]
    cd = chunk_decay[head_std][None, :]
    qm = jnp.asarray((head_perm[None, :] == np.arange(h_n)[:, None]).astype(np.float32))
    vm = jnp.asarray((head_std[None, :] == np.arange(h_n)[:, None]).astype(np.float32))
    bm = jnp.asarray((head_perm[:, None] == head_std[None, :]).astype(np.float32))
    return [dec, zeta_t, xi_t, cd, qm, vm, bm]


def _rotary_tables(s):
    half = HEAD_DIM // 2
    inv = jnp.power(ROPE_BASE, -jnp.arange(half, dtype=F32) / half)
    ang = jnp.arange(s).astype(F32)[:, None] * inv[None, :]
    return jnp.tile(jnp.cos(ang), (1, B_HEADS)), jnp.tile(jnp.sin(ang), (1, B_HEADS))


def _compress_kernel(x_ref, posk_ref, posv_ref, w1k_ref, w1v_ref, w2k_ref, w2vt_ref, kc_ref, vct_ref, *, nseg):
    half = CMP_BLOCK // 2
    width = 2 * C_KV_WIDTH
    acc = [jnp.zeros((nseg, C_KV_WIDTH), F32) for _ in range(4)]
    for l in range(half):
        xk = x_ref[0, :, l * width:l * width + C_KV_WIDTH]
        xv = x_ref[0, :, l * width + C_KV_WIDTH:(l + 1) * width]
        acc[0] += _dot((xk + posk_ref[l:l + 1, :]).astype(BF16), w1k_ref[l])
        acc[1] += _dot((xk + posk_ref[half + l:half + l + 1, :]).astype(BF16), w1k_ref[half + l])
        acc[2] += _dot((xv + posv_ref[l:l + 1, :]).astype(BF16), w1v_ref[l])
        acc[3] += _dot((xv + posv_ref[half + l:half + l + 1, :]).astype(BF16), w1v_ref[half + l])
    hk = jax.nn.gelu(acc[0] + pltpu.roll(acc[1], nseg - 1, 0))
    hv = jax.nn.gelu(acc[2] + pltpu.roll(acc[3], nseg - 1, 0))
    kc_ref[0] = _dot(hk.astype(BF16), w2k_ref[...])
    vct_ref[0] = _dot_nt(w2vt_ref[...], hv.astype(BF16))


def _compress(kvc, posk, posv, w1k, w1v, w2k, w2v):
    bsz, s, width = kvc.shape
    nseg = s // CMP_STRIDE
    x = kvc.reshape(bsz, nseg, CMP_STRIDE * width)
    c2 = lambda bi: (0, 0)
    c3 = lambda bi: (0, 0, 0)
    return pl.pallas_call(
        functools.partial(_compress_kernel, nseg=nseg),
        grid=(bsz,),
        in_specs=[
            pl.BlockSpec((1, nseg, CMP_STRIDE * width), lambda bi: (bi, 0, 0)),
            pl.BlockSpec(posk.shape, c2),
            pl.BlockSpec(posv.shape, c2),
            pl.BlockSpec(w1k.shape, c3),
            pl.BlockSpec(w1v.shape, c3),
            pl.BlockSpec(w2k.shape, c2),
            pl.BlockSpec(w2v.shape, c2),
        ],
        out_specs=[
            pl.BlockSpec((1, nseg, C_KV_WIDTH), lambda bi: (bi, 0, 0)),
            pl.BlockSpec((1, C_KV_WIDTH, nseg), lambda bi: (bi, 0, 0)),
        ],
        out_shape=[
            jax.ShapeDtypeStruct((bsz, nseg, C_KV_WIDTH), F32),
            jax.ShapeDtypeStruct((bsz, C_KV_WIDTH, nseg), F32),
        ],
        compiler_params=_params(("arbitrary",), 32),
        name="nsa_compress",
    )(x, posk, posv, w1k, w1v, w2k, w2v)


def _blockdiag2(w):
    z = jnp.zeros_like(w)
    return jnp.concatenate([jnp.concatenate([w, z], axis=-1), jnp.concatenate([z, w], axis=-1)], axis=-2)


def _overlap_t(nseg, ns):
    nc = nseg - 1
    c0 = np.arange(nc)[None, :] * CMP_STRIDE
    s0 = np.arange(ns)[:, None] * SEL_BLOCK
    ov = np.clip(np.minimum(c0 + CMP_BLOCK, s0 + SEL_BLOCK) - np.maximum(c0, s0), 0, None) / CMP_BLOCK
    out = np.zeros((ns, nseg), np.float32)
    out[:, :nc] = ov
    return jnp.asarray(out, dtype=BF16)


def _nsa_kernel(qt_ref, gt_ref, ks_ref, vst_ref, kw_ref, vwt_ref, kc_ref, vct_ref, ovt_ref, o_ref,
                qall_ref, selw_ref, m_ref, l_ref, acc_ref, res_ref, *, nseg, ns, n_sel):
    i = pl.program_id(1)
    s0 = i * CHUNK
    width = C_HEADS * CHUNK
    zero = jnp.zeros((HEAD_DIM, CHUNK), F32)
    for h in range(C_HEADS):
        qh = qt_ref[0, h * HEAD_DIM:(h + 1) * HEAD_DIM, :] * (HEAD_DIM ** -0.5)
        blk = jnp.concatenate([qh, zero] if h < C_REP else [zero, qh], axis=0)
        qall_ref[:, h * CHUNK:(h + 1) * CHUNK] = blk.astype(BF16)

    def lane_t(shape):
        return s0 + lax.broadcasted_iota(jnp.int32, shape, 1) % CHUNK

    qall = qall_ref[...]
    sc = _dot(kc_ref[0].astype(BF16), qall)
    crow = lax.broadcasted_iota(jnp.int32, (nseg, width), 0)
    cmask = crow * CMP_STRIDE + (CMP_BLOCK - 1) <= lane_t((nseg, width))
    scm = jnp.where(cmask, sc, NEG)
    e = jnp.where(cmask, jnp.exp(scm - jnp.max(scm, axis=0, keepdims=True)), 0.0)
    den = jnp.sum(e, axis=0, keepdims=True)
    p = e / jnp.where(den > 0.0, den, 1.0)
    res_ref[0] = _dot(vct_ref[0].astype(BF16), p.astype(BF16))

    j = lax.broadcasted_iota(jnp.int32, (ns, CHUNK), 0)
    cur = (s0 + lax.broadcasted_iota(jnp.int32, (ns, CHUNK), 1)) // SEL_BLOCK
    forced = (j == 0) | (j == cur) | (j == cur - 1)
    future = j > cur
    for g in range(C_KV_HEADS):
        ps = p[:, g * C_REP * CHUNK:(g * C_REP + 1) * CHUNK]
        for r in range(1, C_REP):
            ps = ps + p[:, (g * C_REP + r) * CHUNK:(g * C_REP + r + 1) * CHUNK]
        imp = _split_dot_left(ovt_ref[...], ps)
        imp = jnp.where(forced, BIG, jnp.where(future, NEG, imp))
        rank = jnp.zeros((ns, CHUNK), F32)
        for i2 in range(ns):
            r_i = imp[i2:i2 + 1, :]
            beats = (r_i > imp) | ((r_i == imp) & (j > i2))
            rank = rank + jnp.where(beats, 1.0, 0.0)
        sel = jnp.where(rank < n_sel, 1.0, 0.0)
        for r in range(C_REP):
            h = g * C_REP + r
            selw_ref[:, h * CHUNK:(h + 1) * CHUNK] = sel

    row = lax.broadcasted_iota(jnp.int32, (CHUNK, width), 0)
    t_q = lane_t((CHUNK, width))

    def attend(k_ref, vt_ref, lo, hi, mask_fn, slot):
        m_ref[...] = jnp.full(m_ref.shape, NEG, F32)
        l_ref[...] = jnp.zeros(l_ref.shape, F32)
        acc_ref[...] = jnp.zeros(acc_ref.shape, F32)

        def body(c, carry):
            off = pl.multiple_of(c * CHUNK, CHUNK)
            s = _dot(k_ref[0, pl.ds(off, CHUNK), :].astype(BF16), qall_ref[...])
            msk = mask_fn(c, c * CHUNK + row)
            sm = jnp.where(msk, s, NEG)
            m_old = m_ref[...]
            m_new = jnp.maximum(m_old, jnp.max(sm, axis=0, keepdims=True))
            pe = jnp.where(msk, jnp.exp(sm - m_new), 0.0)
            alpha = jnp.exp(m_old - m_new)
            l_ref[...] = alpha * l_ref[...] + jnp.sum(pe, axis=0, keepdims=True)
            vt = vt_ref[0, :, pl.ds(off, CHUNK)].astype(BF16)
            acc_ref[...] = alpha * acc_ref[...] + _dot(vt, pe.astype(BF16))
            m_ref[...] = m_new
            return carry

        lax.fori_loop(lo, hi, body, 0)
        l_fin = l_ref[...]
        res_ref[slot] = acc_ref[...] / jnp.where(l_fin > 0.0, l_fin, 1.0)

    def sel_mask(c, kpos):
        r0 = selw_ref[pl.ds(2 * c, 1), :]
        r1 = selw_ref[pl.ds(2 * c + 1, 1), :]
        picked = jnp.where(row < SEL_BLOCK, r0, r1) > 0.5
        return picked & (kpos <= t_q)

    def win_mask(c, kpos):
        rel = t_q - kpos
        return (rel >= 0) & (rel < WINDOW)

    attend(ks_ref, vst_ref, 0, i + 1, sel_mask, 1)
    attend(kw_ref, vwt_ref, jnp.maximum(i - WINDOW // CHUNK, 0), i + 1, win_mask, 2)

    gates = jax.nn.sigmoid(gt_ref[0])
    for pair in range(C_HEADS // 2):
        pieces = []
        for h in (2 * pair, 2 * pair + 1):
            g = h // C_REP
            rs = slice(g * HEAD_DIM, (g + 1) * HEAD_DIM)
            cs = slice(h * CHUNK, (h + 1) * CHUNK)
            pieces.append(gates[3 * h:3 * h + 1, :] * res_ref[0, rs, cs]
                          + gates[3 * h + 1:3 * h + 2, :] * res_ref[1, rs, cs]
                          + gates[3 * h + 2:3 * h + 3, :] * res_ref[2, rs, cs])
        o_ref[0, :, pair * 2 * HEAD_DIM:(pair + 1) * 2 * HEAD_DIM] = jnp.concatenate(pieces, axis=0).T


def _nsa(pt, pn, kc, vct, ovt):
    bsz, s, _ = pn.shape
    nseg = s // CMP_STRIDE
    ns = s // SEL_BLOCK
    width = C_HEADS * CHUNK
    g_rows = PT_ROWS - PT_G
    return pl.pallas_call(
        functools.partial(_nsa_kernel, nseg=nseg, ns=ns, n_sel=min(N_SEL, ns)),
        grid=(bsz, s // CHUNK),
        in_specs=[
            pl.BlockSpec((1, C_WIDTH, CHUNK), lambda b, i: (b, PT_Q // C_WIDTH, i)),
            pl.BlockSpec((1, g_rows, CHUNK), lambda b, i: (b, PT_G // g_rows, i)),
            pl.BlockSpec((1, s, C_KV_WIDTH), lambda b, i: (b, 0, PN_KS // C_KV_WIDTH)),
            pl.BlockSpec((1, C_KV_WIDTH, s), lambda b, i: (b, PT_VS // C_KV_WIDTH, 0)),
            pl.BlockSpec((1, s, C_KV_WIDTH), lambda b, i: (b, 0, PN_KW // C_KV_WIDTH)),
            pl.BlockSpec((1, C_KV_WIDTH, s), lambda b, i: (b, PT_VW // C_KV_WIDTH, 0)),
            pl.BlockSpec((1, nseg, C_KV_WIDTH), lambda b, i: (b, 0, 0)),
            pl.BlockSpec((1, C_KV_WIDTH, nseg), lambda b, i: (b, 0, 0)),
            pl.BlockSpec(ovt.shape, lambda b, i: (0, 0)),
        ],
        out_specs=pl.BlockSpec((1, CHUNK, C_WIDTH), lambda b, i: (b, i, 0)),
        out_shape=jax.ShapeDtypeStruct((bsz, s, C_WIDTH), F32),
        scratch_shapes=[
            pltpu.VMEM((C_KV_WIDTH, width), BF16),
            pltpu.VMEM((ns, width), F32),
            pltpu.VMEM((1, width), F32),
            pltpu.VMEM((1, width), F32),
            pltpu.VMEM((C_KV_WIDTH, width), F32),
            pltpu.VMEM((3, C_KV_WIDTH, width), F32),
        ],
        compiler_params=_params(("arbitrary", "arbitrary"), 40),
        name="nsa_attention",
    )(pt, pt, pn, pt, pn, pt, kc, vct, ovt)


def _outproj_kernel(ya_ref, yb_ref, yc_ref, x_ref, mod_ref, wo_ref, g_ref, b_ref, o_ref, *, d, alpha):
    y = _dot(ya_ref[0].astype(BF16), wo_ref[0:A_WIDTH, :])
    y += _dot(yb_ref[0].astype(BF16), wo_ref[A_WIDTH:A_WIDTH + B_WIDTH, :])
    y += _dot(yc_ref[0].astype(BF16), wo_ref[A_WIDTH + B_WIDTH:, :])
    gate = mod_ref[0, :, 2 * d:3 * d]
    o_ref[0] = _row_ln(alpha * x_ref[0] + gate * y, g_ref[...], b_ref[...])


def _outproj(ya, yb, yc, x, mod, wo, g, b, alpha, tm=512):
    bsz, s, d = x.shape
    const = lambda bi, j: (0, 0)
    tile = lambda w: pl.BlockSpec((1, tm, w), lambda bi, j: (bi, j, 0))
    return pl.pallas_call(
        functools.partial(_outproj_kernel, d=d, alpha=alpha),
        grid=(bsz, s // tm),
        in_specs=[
            tile(A_WIDTH), tile(B_WIDTH), tile(C_WIDTH), tile(d),
            pl.BlockSpec((1, 1, 6 * d), lambda bi, j: (bi, 0, 0)),
            pl.BlockSpec(wo.shape, const),
            pl.BlockSpec(g.shape, const),
            pl.BlockSpec(b.shape, const),
        ],
        out_specs=tile(d),
        out_shape=jax.ShapeDtypeStruct((bsz, s, d), F32),
        compiler_params=_params(("arbitrary", "arbitrary"), 40),
        name="outproj_ln",
    )(ya, yb, yc, x, mod, wo, g, b)


def _ffn_kernel(x_ref, mod_ref, wup_ref, cw_ref, cb_ref, wdn_ref, g_ref, b_ref, o_ref,
                work_ref, carry_ref, y_ref, *, d, dff, cwid, tm, alpha):
    @pl.when(pl.program_id(1) == 0)
    def _():
        carry_ref[...] = jnp.zeros_like(carry_ref)

    x = x_ref[0]
    sh = mod_ref[0, :, 3 * d:4 * d]
    sc = mod_ref[0, :, 4 * d:5 * d]
    gate = mod_ref[0, :, 5 * d:6 * d]
    h = (x * (1.0 + sc) + sh).astype(BF16)
    y_ref[...] = jnp.zeros_like(y_ref)
    pad = SUBLANES
    for ci in range(dff // cwid):
        conv = []
        for part in range(2):
            cs = slice(part * dff + ci * cwid, part * dff + (ci + 1) * cwid)
            a = _dot(h, wup_ref[:, cs])
            work_ref[part, 0:pad, :] = carry_ref[:, cs]
            work_ref[part, pad:pad + tm, :] = a
            carry_ref[:, cs] = a[tm - pad:tm, :]
            a1 = work_ref[part, pad - 1:pad - 1 + tm, :]
            a2 = work_ref[part, pad - 2:pad - 2 + tm, :]
            conv.append(cw_ref[0:1, cs] * a2 + cw_ref[1:2, cs] * a1 + cw_ref[2:3, cs] * a + cb_ref[:, cs])
        act = (jax.nn.silu(conv[0]) * conv[1]).astype(BF16)
        y_ref[...] += _dot(act, wdn_ref[ci * cwid:(ci + 1) * cwid, :])
    o_ref[0] = _row_ln(alpha * x + gate * y_ref[...], g_ref[...], b_ref[...])


def _ffn(x, mod, wup, cw, cb, wdn, g, b, alpha, tm=512, cwid=256):
    bsz, s, d = x.shape
    dff = wdn.shape[0]
    const = lambda bi, j: (0, 0)
    single = dict(pipeline_mode=pl.Buffered(1))
    return pl.pallas_call(
        functools.partial(_ffn_kernel, d=d, dff=dff, cwid=cwid, tm=tm, alpha=alpha),
        grid=(bsz, s // tm),
        in_specs=[
            pl.BlockSpec((1, tm, d), lambda bi, j: (bi, j, 0)),
            pl.BlockSpec((1, 1, 6 * d), lambda bi, j: (bi, 0, 0)),
            pl.BlockSpec(wup.shape, const, **single),
            pl.BlockSpec(cw.shape, const),
            pl.BlockSpec(cb.shape, const),
            pl.BlockSpec(wdn.shape, const, **single),
            pl.BlockSpec(g.shape, const),
            pl.BlockSpec(b.shape, const),
        ],
        out_specs=pl.BlockSpec((1, tm, d), lambda bi, j: (bi, j, 0)),
        out_shape=jax.ShapeDtypeStruct((bsz, s, d), F32),
        scratch_shapes=[
            pltpu.VMEM((2, tm + SUBLANES, cwid), F32),
            pltpu.VMEM((SUBLANES, 2 * dff), F32),
            pltpu.VMEM((tm, d), F32),
        ],
        compiler_params=_params(("arbitrary", "arbitrary"), 48),
        name="conv_ffn_ln",
    )(x, mod, wup, cw, cb, wdn, g, b)


def _inproj_weights(w):
    cols = np.arange(B_WIDTH)
    half = cols // (B_WIDTH // 2)
    perm = ((cols % (B_WIDTH // 2)) // (HEAD_DIM // 2)) * HEAD_DIM + half * (HEAD_DIM // 2) + cols % (HEAD_DIM // 2)
    za = w[:, 0:512]
    qb, kb, vb, gb = (w[:, 512 + i * B_WIDTH:512 + (i + 1) * B_WIDTH] for i in range(4))
    qc = w[:, 1536:2048]
    kcm, vcm, ksl, vsl, kwn, vwn = (w[:, 2048 + i * C_KV_WIDTH:2048 + (i + 1) * C_KV_WIDTH] for i in range(6))
    gc = w[:, 2816:2840]
    wn = jnp.concatenate([qb[:, perm], kb[:, perm], vb, gb, za, ksl, kwn, kcm, vcm], axis=1)
    pad = jnp.zeros((w.shape[0], PT_ROWS - PT_G - gc.shape[1]), w.dtype)
    wt = jnp.concatenate([qc, vsl, vwn, gc, pad], axis=1).T
    return wn.astype(BF16), wt.astype(BF16)


def kernel(x, c, w_ada, b_ada, w_in, a_ln_g, a_ln_b, a_ws, a_bs, b_gn_g, b_gn_b, c_pos_k, c_w1_k, c_w2_k,
           c_pos_v, c_w1_v, c_w2_v, w_out, ln1_g, ln1_b, w_up, conv_w, conv_b, w_down, ln2_g, ln2_b):
    depth = w_in.shape[0]
    bsz, s, d = x.shape
    alpha = (2 * depth) ** 0.25
    nseg = s // CMP_STRIDE
    ns = s // SEL_BLOCK

    lanes = np.arange(A_WIDTH)
    mavg = jnp.asarray((lanes[:, None] // HEAD_DIM == lanes[None, :] // HEAD_DIM) / HEAD_DIM, dtype=BF16)
    gm = jnp.asarray((lanes[None, :] // HEAD_DIM == np.arange(A_GROUPS)[:, None]).astype(np.float32))
    ret_consts = _retention_consts()
    cos, sin = _rotary_tables(s)
    ovt = _overlap_t(nseg, ns)

    mods = _ada_mod(c, w_ada, b_ada)
    for l in range(depth):
        mod = mods[l][:, None, :]
        wn, wt = _inproj_weights(w_in[l])
        pn, kvc, pt = _inproj(x, mod, wn, wt)

        wcat = jnp.transpose(a_ws[l], (1, 0, 2)).reshape(CHUNK, A_GROUPS * CHUNK)
        bias = jnp.repeat(a_bs[l].T, HEAD_DIM, axis=1)
        ya = _mixer_a(pn, wcat, bias, a_ln_g[l].reshape(1, A_WIDTH), a_ln_b[l].reshape(1, A_WIDTH), mavg, gm)

        yb = _retention(pn, cos, sin, ret_consts, mavg,
                        b_gn_g[l].reshape(1, B_WIDTH), b_gn_b[l].reshape(1, B_WIDTH))

        w1k = _blockdiag2(c_w1_k[l].reshape(CMP_BLOCK, HEAD_DIM, HEAD_DIM)).astype(BF16)
        w1v = _blockdiag2(c_w1_v[l].reshape(CMP_BLOCK, HEAD_DIM, HEAD_DIM)).astype(BF16)
        kc, vct = _compress(kvc, jnp.tile(c_pos_k[l], (1, C_KV_HEADS)), jnp.tile(c_pos_v[l], (1, C_KV_HEADS)),
                            w1k, w1v, _blockdiag2(c_w2_k[l]).astype(BF16), _blockdiag2(c_w2_v[l]).T.astype(BF16))
        yc = _nsa(pt, pn, kc, vct, ovt)

        x = _outproj(ya, yb, yc, x, mod, w_out[l].astype(BF16), ln1_g[l].reshape(1, d), ln1_b[l].reshape(1, d), alpha)
        x = _ffn(x, mod, w_up[l].astype(BF16), conv_w[l], conv_b[l].reshape(1, -1), w_down[l].astype(BF16),
                 ln2_g[l].reshape(1, d), ln2_b[l].reshape(1, d), alpha)
    return x
```

```python
import functools

import numpy as np
import jax
import jax.numpy as jnp
from jax import lax
from jax.experimental import pallas as pl
from jax.experimental.pallas import tpu as pltpu

F32 = jnp.float32
BF16 = jnp.bfloat16

HEAD_DIM = 64
A_GROUPS = 4
A_WIDTH = A_GROUPS * HEAD_DIM
CHUNK = 128
B_HEADS = 4
B_WIDTH = B_HEADS * HEAD_DIM
ROPE_BASE = 10000.0
C_HEADS = 8
C_KV_HEADS = 2
C_REP = C_HEADS // C_KV_HEADS
C_WIDTH = C_HEADS * HEAD_DIM
C_KV_WIDTH = C_KV_HEADS * HEAD_DIM
CMP_BLOCK = 32
CMP_STRIDE = 16
SEL_BLOCK = 64
N_SEL = 8
WINDOW = 512
CONV_WIDTH = 3
LN_EPS = 1e-5
NEG = -1e30
BIG = 1e30
LOG2E = 1.4426950408889634

LANES = 128
SUBLANES = 8
VMEM_BYTES = 64 * 1024 * 1024

PN_RET, PN_ZA, PN_KS, PN_KW, PN_KVC = 0, 1024, 1536, 1664, 1792
PN_COLS = 1792
PT_Q, PT_VS, PT_VW, PT_G = 0, 512, 640, 768
PT_ROWS = 800


def _dot(a, b):
    return jnp.dot(a, b, preferred_element_type=F32)


def _dot_nt(a, b):
    return lax.dot_general(a, b, (((1,), (1,)), ((), ())), preferred_element_type=F32)


def _dot_tn(a, b):
    return lax.dot_general(a, b, (((0,), (0,)), ((), ())), preferred_element_type=F32)


def _split_dot(x, m):
    hi = x.astype(BF16)
    lo = (x - hi.astype(F32)).astype(BF16)
    return _dot(hi, m) + _dot(lo, m)


def _split_dot_left(m, x):
    hi = x.astype(BF16)
    lo = (x - hi.astype(F32)).astype(BF16)
    return _dot(m, hi) + _dot(m, lo)


def _group_ln(x, mavg, g, b):
    mu = _split_dot(x, mavg)
    d = x - mu
    var = _split_dot(d * d, mavg)
    return d * lax.rsqrt(var + LN_EPS) * g + b


def _row_ln(x, g, b):
    mu = jnp.mean(x, axis=-1, keepdims=True)
    d = x - mu
    var = jnp.mean(d * d, axis=-1, keepdims=True)
    return d * lax.rsqrt(var + LN_EPS) * g + b


def _params(sem, vmem_mb):
    return pltpu.CompilerParams(dimension_semantics=sem, vmem_limit_bytes=vmem_mb * 1024 * 1024)


def _ada_kernel(c_ref, w_ref, b_ref, o_ref):
    cond = jax.nn.silu(c_ref[...]).astype(BF16)
    o_ref[0] = _dot(cond, w_ref[0].astype(BF16)) + b_ref[0]


def _ada_mod(c, w_ada, b_ada):
    depth, d, n = w_ada.shape
    bsz = c.shape[0]
    tn = 1536
    return pl.pallas_call(
        _ada_kernel,
        grid=(depth, n // tn),
        in_specs=[
            pl.BlockSpec((bsz, d), lambda l, j: (0, 0)),
            pl.BlockSpec((1, d, tn), lambda l, j: (l, 0, j)),
            pl.BlockSpec((1, 1, tn), lambda l, j: (l, 0, j)),
        ],
        out_specs=pl.BlockSpec((1, bsz, tn), lambda l, j: (l, 0, j)),
        out_shape=jax.ShapeDtypeStruct((depth, bsz, n), F32),
        compiler_params=_params(("arbitrary", "arbitrary"), 40),
        name="ada_mod",
    )(c, w_ada, b_ada.reshape(depth, 1, n))


def _inproj_kernel(x_ref, mod_ref, wn_ref, wt_ref, pn_ref, kvc_ref, pt_ref, *, d):
    sh = mod_ref[0, :, 0:d]
    sc = mod_ref[0, :, d:2 * d]
    h = (x_ref[0] * (1.0 + sc) + sh).astype(BF16)
    pn = _dot(h, wn_ref[...])
    pn_ref[0] = pn[:, :PN_COLS]
    kvc_ref[0] = pn[:, PN_KVC:]
    pt_ref[0] = _dot_nt(wt_ref[...], h)


def _inproj(x, mod, wn, wt, tm=512):
    bsz, s, d = x.shape
    const = lambda b, j: (0, 0)
    return pl.pallas_call(
        functools.partial(_inproj_kernel, d=d),
        grid=(bsz, s // tm),
        in_specs=[
            pl.BlockSpec((1, tm, d), lambda b, j: (b, j, 0)),
            pl.BlockSpec((1, 1, 6 * d), lambda b, j: (b, 0, 0)),
            pl.BlockSpec(wn.shape, const),
            pl.BlockSpec(wt.shape, const),
        ],
        out_specs=[
            pl.BlockSpec((1, tm, PN_COLS), lambda b, j: (b, j, 0)),
            pl.BlockSpec((1, tm, 2 * C_KV_WIDTH), lambda b, j: (b, j, 0)),
            pl.BlockSpec((1, PT_ROWS, tm), lambda b, j: (b, 0, j)),
        ],
        out_shape=[
            jax.ShapeDtypeStruct((bsz, s, PN_COLS), F32),
            jax.ShapeDtypeStruct((bsz, s, 2 * C_KV_WIDTH), F32),
            jax.ShapeDtypeStruct((bsz, PT_ROWS, s), F32),
        ],
        compiler_params=_params(("arbitrary", "arbitrary"), 48),
        name="inproj",
    )(x, mod, wn, wt)


def _mixer_a_kernel(za_ref, w_ref, bias_ref, g_ref, b_ref, mavg_ref, gm_ref, o_ref, *, n_chunks):
    row = lax.broadcasted_iota(jnp.int32, (CHUNK, A_GROUPS * CHUNK), 0)
    col = lax.broadcasted_iota(jnp.int32, (CHUNK, A_GROUPS * CHUNK), 1)
    wc = jnp.where((col % CHUNK) <= row, w_ref[...], 0.0).astype(BF16)
    for c in range(n_chunks):
        rows = slice(c * CHUNK, (c + 1) * CHUNK)
        z = jax.nn.gelu(za_ref[0, rows, :])
        u = z[:, :A_WIDTH]
        v = z[:, A_WIDTH:]
        vn = _group_ln(v, mavg_ref[...], g_ref[...], b_ref[...])
        vstack = jnp.concatenate([vn * gm_ref[g:g + 1, :] for g in range(A_GROUPS)], axis=0).astype(BF16)
        vs = _dot(wc, vstack) + bias_ref[...]
        o_ref[0, rows, :] = u * vs


def _mixer_a(pn, wcat, bias, g, b, mavg, gm, tb=512):
    bsz, s, _ = pn.shape
    const = lambda bi, j: (0, 0)
    return pl.pallas_call(
        functools.partial(_mixer_a_kernel, n_chunks=tb // CHUNK),
        grid=(bsz, s // tb),
        in_specs=[
            pl.BlockSpec((1, tb, 2 * A_WIDTH), lambda bi, j: (bi, j, PN_ZA // (2 * A_WIDTH))),
            pl.BlockSpec(wcat.shape, const),
            pl.BlockSpec(bias.shape, const),
            pl.BlockSpec(g.shape, const),
            pl.BlockSpec(b.shape, const),
            pl.BlockSpec(mavg.shape, const),
            pl.BlockSpec(gm.shape, const),
        ],
        out_specs=pl.BlockSpec((1, tb, A_WIDTH), lambda bi, j: (bi, j, 0)),
        out_shape=jax.ShapeDtypeStruct((bsz, s, A_WIDTH), F32),
        compiler_params=_params(("arbitrary", "arbitrary"), 32),
        name="mixer_a",
    )(pn, wcat, bias, g, b, mavg, gm)


def _retention_kernel(x_ref, cos_ref, sin_ref, dec_ref, zeta_ref, xi_ref, cd_ref, qm_ref, vm_ref, bm_ref,
                      mavg_ref, g_ref, b_ref, o_ref, state_ref, *, n_chunks):
    @pl.when(pl.program_id(1) == 0)
    def _():
        state_ref[...] = jnp.zeros_like(state_ref)

    half = B_WIDTH // 2
    for c in range(n_chunks):
        rows = slice(c * CHUNK, (c + 1) * CHUNK)
        q = x_ref[0, rows, 0:B_WIDTH]
        k = x_ref[0, rows, B_WIDTH:2 * B_WIDTH]
        v = x_ref[0, rows, 2 * B_WIDTH:3 * B_WIDTH]
        gate = x_ref[0, rows, 3 * B_WIDTH:4 * B_WIDTH]
        cos = cos_ref[rows, :]
        sin = sin_ref[rows, :]

        def rot(t):
            t1 = t[:, :half]
            t2 = t[:, half:]
            return jnp.concatenate([t1 * cos - t2 * sin, t1 * sin + t2 * cos], axis=1)

        qr = rot(q)
        kr = rot(k) * (HEAD_DIM ** -0.5)
        qs = jnp.concatenate([qr * qm_ref[h:h + 1, :] for h in range(B_HEADS)], axis=0).astype(BF16)
        s = _dot_nt(qs, kr.astype(BF16)) * dec_ref[...]
        scat = jnp.concatenate([s[h * CHUNK:(h + 1) * CHUNK, :] for h in range(B_HEADS)], axis=1).astype(BF16)
        vstack = jnp.concatenate([v * vm_ref[h:h + 1, :] for h in range(B_HEADS)], axis=0).astype(BF16)
        o_inner = _dot(scat, vstack)
        state = state_ref[...]
        o_cross = _dot(qr.astype(BF16), state.astype(BF16)) * xi_ref[...]
        kz = (kr * zeta_ref[...]).astype(BF16)
        kv = _dot_tn(kz, v.astype(BF16)) * bm_ref[...]
        state_ref[...] = state * cd_ref[...] + kv
        o = _group_ln(o_inner + o_cross, mavg_ref[...], g_ref[...], b_ref[...])
        o_ref[0, rows, :] = jax.nn.silu(gate) * o


def _retention(pn, cos, sin, consts, mavg, g, b, tb=512):
    bsz, s, _ = pn.shape
    const = lambda bi, j: (0, 0)
    return pl.pallas_call(
        functools.partial(_retention_kernel, n_chunks=tb // CHUNK),
        grid=(bsz, s // tb),
        in_specs=[
            pl.BlockSpec((1, tb, 4 * B_WIDTH), lambda bi, j: (bi, j, PN_RET // (4 * B_WIDTH))),
            pl.BlockSpec((tb, B_WIDTH // 2), lambda bi, j: (j, 0)),
            pl.BlockSpec((tb, B_WIDTH // 2), lambda bi, j: (j, 0)),
        ] + [pl.BlockSpec(a.shape, const) for a in consts] + [
            pl.BlockSpec(mavg.shape, const),
            pl.BlockSpec(g.shape, const),
            pl.BlockSpec(b.shape, const),
        ],
        out_specs=pl.BlockSpec((1, tb, B_WIDTH), lambda bi, j: (bi, j, 0)),
        out_shape=jax.ShapeDtypeStruct((bsz, s, B_WIDTH), F32),
        scratch_shapes=[pltpu.VMEM((B_WIDTH, B_WIDTH), F32)],
        compiler_params=_params(("arbitrary", "arbitrary"), 32),
        name="retention",
    )(pn, cos, sin, *consts, mavg, g, b)


def _retention_consts():
    h_n, d, l_n = B_HEADS, HEAD_DIM, CHUNK
    log_gamma = jnp.log1p(-jnp.exp2(-5.0 - jnp.arange(h_n, dtype=F32)))
    idx = jnp.arange(l_n, dtype=F32)
    diff = idx[:, None] - idx[None, :]
    decay_in = jnp.where(diff >= 0, jnp.exp(log_gamma[:, None, None] * jnp.maximum(diff, 0.0)), 0.0)
    xi = jnp.exp(log_gamma[:, None] * (idx + 1.0))
    zeta = jnp.exp(log_gamma[:, None] * (l_n - 1.0 - idx))
    chunk_decay = jnp.exp(log_gamma * l_n)
    cols = np.arange(B_WIDTH)
    head_perm = (cols % (B_WIDTH // 2)) // (d // 2)
    head_std = cols // d
    dec = decay_in.reshape(h_n * l_n, l_n)
    zeta_t = zeta.T[:, head_perm]
    xi_t = xi.T[:, head_std]
    cd = chunk_decay[head_std][None, :]
    qm = jnp.asarray((head_perm[None, :] == np.arange(h_n)[:, None]).astype(np.float32))
    vm = jnp.asarray((head_std[None, :] == np.arange(h_n)[:, None]).astype(np.float32))
    bm = jnp.asarray((head_perm[:, None] == head_std[None, :]).astype(np.float32))
    return [dec, zeta_t, xi_t, cd, qm, vm, bm]


def _rotary_tables(s):
    half = HEAD_DIM // 2
    inv = jnp.power(ROPE_BASE, -jnp.arange(half, dtype=F32) / half)
    ang = jnp.arange(s).astype(F32)[:, None] * inv[None, :]
    return jnp.tile(jnp.cos(ang), (1, B_HEADS)), jnp.tile(jnp.sin(ang), (1, B_HEADS))


def _compress_kernel(x_ref, posk_ref, posv_ref, w1k_ref, w1v_ref, w2k_ref, w2vt_ref, kc_ref, vct_ref, *, nseg):
    half = CMP_BLOCK // 2
    width = 2 * C_KV_WIDTH
    acc = [jnp.zeros((nseg, C_KV_WIDTH), F32) for _ in range(4)]
    for l in range(half):
        xk = x_ref[0, :, l * width:l * width + C_KV_WIDTH]
        xv = x_ref[0, :, l * width + C_KV_WIDTH:(l + 1) * width]
        acc[0] += _dot((xk + posk_ref[l:l + 1, :]).astype(BF16), w1k_ref[l])
        acc[1] += _dot((xk + posk_ref[half + l:half + l + 1, :]).astype(BF16), w1k_ref[half + l])
        acc[2] += _dot((xv + posv_ref[l:l + 1, :]).astype(BF16), w1v_ref[l])
        acc[3] += _dot((xv + posv_ref[half + l:half + l + 1, :]).astype(BF16), w1v_ref[half + l])
    hk = jax.nn.gelu(acc[0] + pltpu.roll(acc[1], nseg - 1, 0))
    hv = jax.nn.gelu(acc[2] + pltpu.roll(acc[3], nseg - 1, 0))
    kc_ref[0] = _dot(hk.astype(BF16), w2k_ref[...])
    vct_ref[0] = _dot_nt(w2vt_ref[...], hv.astype(BF16))


def _compress(kvc, posk, posv, w1k, w1v, w2k, w2v):
    bsz, s, width = kvc.shape
    nseg = s // CMP_STRIDE
    x = kvc.reshape(bsz, nseg, CMP_STRIDE * width)
    c2 = lambda bi: (0, 0)
    c3 = lambda bi: (0, 0, 0)
    return pl.pallas_call(
        functools.partial(_compress_kernel, nseg=nseg),
        grid=(bsz,),
        in_specs=[
            pl.BlockSpec((1, nseg, CMP_STRIDE * width), lambda bi: (bi, 0, 0)),
            pl.BlockSpec(posk.shape, c2),
            pl.BlockSpec(posv.shape, c2),
            pl.BlockSpec(w1k.shape, c3),
            pl.BlockSpec(w1v.shape, c3),
            pl.BlockSpec(w2k.shape, c2),
            pl.BlockSpec(w2v.shape, c2),
        ],
        out_specs=[
            pl.BlockSpec((1, nseg, C_KV_WIDTH), lambda bi: (bi, 0, 0)),
            pl.BlockSpec((1, C_KV_WIDTH, nseg), lambda bi: (bi, 0, 0)),
        ],
        out_shape=[
            jax.ShapeDtypeStruct((bsz, nseg, C_KV_WIDTH), F32),
            jax.ShapeDtypeStruct((bsz, C_KV_WIDTH, nseg), F32),
        ],
        compiler_params=_params(("arbitrary",), 32),
        name="nsa_compress",
    )(x, posk, posv, w1k, w1v, w2k, w2v)


def _blockdiag2(w):
    z = jnp.zeros_like(w)
    return jnp.concatenate([jnp.concatenate([w, z], axis=-1), jnp.concatenate([z, w], axis=-1)], axis=-2)


def _overlap_t(nseg, ns):
    nc = nseg - 1
    c0 = np.arange(nc)[None, :] * CMP_STRIDE
    s0 = np.arange(ns)[:, None] * SEL_BLOCK
    ov = np.clip(np.minimum(c0 + CMP_BLOCK, s0 + SEL_BLOCK) - np.maximum(c0, s0), 0, None) / CMP_BLOCK
    out = np.zeros((ns, nseg), np.float32)
    out[:, :nc] = ov
    return jnp.asarray(out, dtype=BF16)


def _nsa_kernel(qt_ref, gt_ref, ks_ref, vst_ref, kw_ref, vwt_ref, kc_ref, vct_ref, ovt_ref, o_ref,
                qall_ref, selb_ref, m_ref, l_ref, alpha_ref, acc_ref, res_ref, sbuf_ref, pbuf_ref,
                *, nseg, ns, n_sel):
    i = pl.program_id(1)
    s0 = i * CHUNK
    blk_w = 2 * CHUNK
    n_blk = C_HEADS // 2
    blk_per_g = n_blk // C_KV_HEADS
    zero = jnp.zeros((HEAD_DIM, CHUNK), F32)
    for h in range(C_HEADS):
        qh = qt_ref[0, h * HEAD_DIM:(h + 1) * HEAD_DIM, :] * (HEAD_DIM ** -0.5 * LOG2E)
        blk = jnp.concatenate([qh, zero] if h < C_REP else [zero, qh], axis=0)
        qall_ref[:, h * CHUNK:(h + 1) * CHUNK] = blk.astype(BF16)

    krow = lax.broadcasted_iota(jnp.int32, (CHUNK, blk_w), 0)
    t_loc = lax.broadcasted_iota(jnp.int32, (CHUNK, blk_w), 1) % CHUNK
    tri_diag = jnp.where(krow <= t_loc, 0.0, NEG)
    tri_old = jnp.where(krow > t_loc, 0.0, NEG)

    crow = lax.broadcasted_iota(jnp.int32, (nseg, blk_w), 0)
    c_t = s0 + lax.broadcasted_iota(jnp.int32, (nseg, blk_w), 1) % CHUNK
    cbias = jnp.where(crow * CMP_STRIDE + (CMP_BLOCK - 1) <= c_t, 0.0, NEG)
    kc = kc_ref[0].astype(BF16)
    vct = vct_ref[0].astype(BF16)
    psum = [None] * C_KV_HEADS
    for b in range(n_blk):
        g = b // blk_per_g
        cols = slice(b * blk_w, (b + 1) * blk_w)
        sc = _dot(kc, qall_ref[:, cols]) + cbias
        mx = jnp.max(sc, axis=0, keepdims=True)
        e = jnp.exp2(sc - mx)
        den = jnp.sum(e, axis=0, keepdims=True)
        p = e * jnp.where(mx > 0.5 * NEG, 1.0 / den, 0.0)
        res_ref[0, :, cols] = _dot(vct[g * HEAD_DIM:(g + 1) * HEAD_DIM, :], p.astype(BF16))
        both = p[:, :CHUNK] + p[:, CHUNK:]
        psum[g] = both if psum[g] is None else psum[g] + both

    j = lax.broadcasted_iota(jnp.int32, (ns, CHUNK), 0)
    cur = (s0 + lax.broadcasted_iota(jnp.int32, (ns, CHUNK), 1)) // SEL_BLOCK
    forced = (j == 0) | (j == cur) | (j == cur - 1)
    future = j > cur
    for g in range(C_KV_HEADS):
        imp = _split_dot_left(ovt_ref[...], psum[g])
        imp = jnp.where(forced, BIG, jnp.where(future, NEG, imp))
        rank = jnp.zeros((ns, CHUNK), F32)
        for i2 in range(ns):
            r_i = imp[i2:i2 + 1, :]
            beats = (r_i > imp) | ((r_i == imp) & (j > i2))
            rank = rank + jnp.where(beats, 1.0, 0.0)
        sel_bias = jnp.where(rank < n_sel, 0.0, NEG)
        for r in range(C_REP):
            h = g * C_REP + r
            selb_ref[:, h * CHUNK:(h + 1) * CHUNK] = sel_bias

    def reset():
        m_ref[...] = jnp.full(m_ref.shape, NEG, F32)
        l_ref[...] = jnp.zeros(l_ref.shape, F32)
        acc_ref[...] = jnp.zeros(acc_ref.shape, F32)
        alpha_ref[...] = jnp.ones(alpha_ref.shape, F32)
        pbuf_ref[...] = jnp.zeros(pbuf_ref.shape, BF16)

    def scores(k_ref, c, slot):
        off = pl.multiple_of(c * CHUNK, CHUNK)
        sbuf_ref[slot] = _dot(k_ref[0, pl.ds(off, CHUNK), :].astype(BF16), qall_ref[...])

    def softmax(slot, sel_c, tri, extra):
        for b in range(n_blk):
            cols = slice(b * blk_w, (b + 1) * blk_w)
            s = sbuf_ref[slot, :, cols]
            if sel_c is not None:
                b0 = selb_ref[pl.ds(2 * sel_c, 1), cols]
                b1 = selb_ref[pl.ds(2 * sel_c + 1, 1), cols]
                s = jnp.concatenate([s[:SEL_BLOCK] + b0, s[SEL_BLOCK:] + b1], axis=0)
            if tri is not None:
                s = s + tri
            if extra is not None:
                s = s + extra
            m_old = m_ref[:, cols]
            m_new = jnp.maximum(m_old, jnp.max(s, axis=0, keepdims=True))
            p = jnp.exp2(s - m_new)
            alpha = jnp.exp2(m_old - m_new)
            alpha_ref[:, cols] = alpha
            l_ref[:, cols] = alpha * l_ref[:, cols] + jnp.sum(p, axis=0, keepdims=True)
            pbuf_ref[:, cols] = p.astype(BF16)
            m_ref[:, cols] = m_new

    def accumulate(vt_ref, c):
        off = pl.multiple_of(c * CHUNK, CHUNK)
        vt = vt_ref[0, :, pl.ds(off, CHUNK)].astype(BF16)
        for b in range(n_blk):
            g = b // blk_per_g
            cols = slice(b * blk_w, (b + 1) * blk_w)
            pv = _dot(vt[g * HEAD_DIM:(g + 1) * HEAD_DIM, :], pbuf_ref[:, cols])
            acc_ref[:, cols] = alpha_ref[:, cols] * acc_ref[:, cols] + pv

    def finish(slot):
        l_fin = l_ref[...]
        res_ref[slot] = acc_ref[...] * jnp.where(l_fin > 0.0, 1.0 / l_fin, 0.0)

    reset()
    scores(ks_ref, 0, 0)

    def sel_body(k, carry):
        accumulate(vst_ref, jnp.maximum(k - 1, 0))
        softmax(k % 2, k, None, None)
        scores(ks_ref, k + 1, (k + 1) % 2)
        return carry

    lax.fori_loop(0, i, sel_body, 0)
    accumulate(vst_ref, jnp.maximum(i - 1, 0))
    softmax(i % 2, i, tri_diag, None)
    accumulate(vst_ref, i)
    finish(1)

    n_back = WINDOW // CHUNK
    reset()
    chunk = [jnp.maximum(i - n_back + w, 0) for w in range(n_back + 1)]
    gone = [jnp.where(i - n_back + w < 0, NEG, 0.0) for w in range(n_back)]
    scores(kw_ref, chunk[0], 0)
    for w in range(n_back + 1):
        if w >= 1:
            accumulate(vwt_ref, chunk[w - 1])
        tri = tri_old if w == 0 else (tri_diag if w == n_back else None)
        softmax(w % 2, None, tri, gone[w] if w < n_back else None)
        if w < n_back:
            scores(kw_ref, chunk[w + 1], (w + 1) % 2)
    accumulate(vwt_ref, chunk[n_back])
    finish(2)

    gates = jax.nn.sigmoid(gt_ref[0])
    for pair in range(C_HEADS // 2):
        pieces = []
        for h in (2 * pair, 2 * pair + 1):
            cs = slice(h * CHUNK, (h + 1) * CHUNK)
            pieces.append(gates[3 * h:3 * h + 1, :] * res_ref[0, :, cs]
                          + gates[3 * h + 1:3 * h + 2, :] * res_ref[1, :, cs]
                          + gates[3 * h + 2:3 * h + 3, :] * res_ref[2, :, cs])
        o_ref[0, :, pair * 2 * HEAD_DIM:(pair + 1) * 2 * HEAD_DIM] = jnp.concatenate(pieces, axis=0).T


def _nsa(pt, pn, kc, vct, ovt):
    bsz, s, _ = pn.shape
    nseg = s // CMP_STRIDE
    ns = s // SEL_BLOCK
    width = C_HEADS * CHUNK
    g_rows = PT_ROWS - PT_G
    return pl.pallas_call(
        functools.partial(_nsa_kernel, nseg=nseg, ns=ns, n_sel=min(N_SEL, ns)),
        grid=(bsz, s // CHUNK),
        in_specs=[
            pl.BlockSpec((1, C_WIDTH, CHUNK), lambda b, i: (b, PT_Q // C_WIDTH, i)),
            pl.BlockSpec((1, g_rows, CHUNK), lambda b, i: (b, PT_G // g_rows, i)),
            pl.BlockSpec((1, s, C_KV_WIDTH), lambda b, i: (b, 0, PN_KS // C_KV_WIDTH)),
            pl.BlockSpec((1, C_KV_WIDTH, s), lambda b, i: (b, PT_VS // C_KV_WIDTH, 0)),
            pl.BlockSpec((1, s, C_KV_WIDTH), lambda b, i: (b, 0, PN_KW // C_KV_WIDTH)),
            pl.BlockSpec((1, C_KV_WIDTH, s), lambda b, i: (b, PT_VW // C_KV_WIDTH, 0)),
            pl.BlockSpec((1, nseg, C_KV_WIDTH), lambda b, i: (b, 0, 0)),
            pl.BlockSpec((1, C_KV_WIDTH, nseg), lambda b, i: (b, 0, 0)),
            pl.BlockSpec(ovt.shape, lambda b, i: (0, 0)),
        ],
        out_specs=pl.BlockSpec((1, CHUNK, C_WIDTH), lambda b, i: (b, i, 0)),
        out_shape=jax.ShapeDtypeStruct((bsz, s, C_WIDTH), F32),
        scratch_shapes=[
            pltpu.VMEM((C_KV_WIDTH, width), BF16),
            pltpu.VMEM((ns, width), F32),
            pltpu.VMEM((1, width), F32),
            pltpu.VMEM((1, width), F32),
            pltpu.VMEM((1, width), F32),
            pltpu.VMEM((HEAD_DIM, width), F32),
            pltpu.VMEM((3, HEAD_DIM, width), F32),
            pltpu.VMEM((2, CHUNK, width), F32),
            pltpu.VMEM((CHUNK, width), BF16),
        ],
        compiler_params=_params(("arbitrary", "arbitrary"), 40),
        name="nsa_attention",
    )(pt, pt, pn, pt, pn, pt, kc, vct, ovt)


def _outproj_kernel(ya_ref, yb_ref, yc_ref, x_ref, mod_ref, wo_ref, g_ref, b_ref, o_ref, *, d, alpha):
    y = _dot(ya_ref[0].astype(BF16), wo_ref[0:A_WIDTH, :])
    y += _dot(yb_ref[0].astype(BF16), wo_ref[A_WIDTH:A_WIDTH + B_WIDTH, :])
    y += _dot(yc_ref[0].astype(BF16), wo_ref[A_WIDTH + B_WIDTH:, :])
    gate = mod_ref[0, :, 2 * d:3 * d]
    o_ref[0] = _row_ln(alpha * x_ref[0] + gate * y, g_ref[...], b_ref[...])


def _outproj(ya, yb, yc, x, mod, wo, g, b, alpha, tm=512):
    bsz, s, d = x.shape
    const = lambda bi, j: (0, 0)
    tile = lambda w: pl.BlockSpec((1, tm, w), lambda bi, j: (bi, j, 0))
    return pl.pallas_call(
        functools.partial(_outproj_kernel, d=d, alpha=alpha),
        grid=(bsz, s // tm),
        in_specs=[
            tile(A_WIDTH), tile(B_WIDTH), tile(C_WIDTH), tile(d),
            pl.BlockSpec((1, 1, 6 * d), lambda bi, j: (bi, 0, 0)),
            pl.BlockSpec(wo.shape, const),
            pl.BlockSpec(g.shape, const),
            pl.BlockSpec(b.shape, const),
        ],
        out_specs=tile(d),
        out_shape=jax.ShapeDtypeStruct((bsz, s, d), F32),
        compiler_params=_params(("arbitrary", "arbitrary"), 40),
        name="outproj_ln",
    )(ya, yb, yc, x, mod, wo, g, b)


def _ffn_kernel(x_ref, mod_ref, wup_ref, cw_ref, cb_ref, wdn_ref, g_ref, b_ref, o_ref,
                work_ref, carry_ref, y_ref, *, d, dff, cwid, tm, alpha):
    @pl.when(pl.program_id(1) == 0)
    def _():
        carry_ref[...] = jnp.zeros_like(carry_ref)

    x = x_ref[0]
    sh = mod_ref[0, :, 3 * d:4 * d]
    sc = mod_ref[0, :, 4 * d:5 * d]
    gate = mod_ref[0, :, 5 * d:6 * d]
    h = (x * (1.0 + sc) + sh).astype(BF16)
    y_ref[...] = jnp.zeros_like(y_ref)
    pad = SUBLANES
    for ci in range(dff // cwid):
        conv = []
        for part in range(2):
            cs = slice(part * dff + ci * cwid, part * dff + (ci + 1) * cwid)
            a = _dot(h, wup_ref[:, cs])
            work_ref[part, 0:pad, :] = carry_ref[:, cs]
            work_ref[part, pad:pad + tm, :] = a
            carry_ref[:, cs] = a[tm - pad:tm, :]
            a1 = work_ref[part, pad - 1:pad - 1 + tm, :]
            a2 = work_ref[part, pad - 2:pad - 2 + tm, :]
            conv.append(cw_ref[0:1, cs] * a2 + cw_ref[1:2, cs] * a1 + cw_ref[2:3, cs] * a + cb_ref[:, cs])
        act = (jax.nn.silu(conv[0]) * conv[1]).astype(BF16)
        y_ref[...] += _dot(act, wdn_ref[ci * cwid:(ci + 1) * cwid, :])
    o_ref[0] = _row_ln(alpha * x + gate * y_ref[...], g_ref[...], b_ref[...])


def _ffn(x, mod, wup, cw, cb, wdn, g, b, alpha, tm=512, cwid=256):
    bsz, s, d = x.shape
    dff = wdn.shape[0]
    const = lambda bi, j: (0, 0)
    single = dict(pipeline_mode=pl.Buffered(1))
    return pl.pallas_call(
        functools.partial(_ffn_kernel, d=d, dff=dff, cwid=cwid, tm=tm, alpha=alpha),
        grid=(bsz, s // tm),
        in_specs=[
            pl.BlockSpec((1, tm, d), lambda bi, j: (bi, j, 0)),
            pl.BlockSpec((1, 1, 6 * d), lambda bi, j: (bi, 0, 0)),
            pl.BlockSpec(wup.shape, const, **single),
            pl.BlockSpec(cw.shape, const),
            pl.BlockSpec(cb.shape, const),
            pl.BlockSpec(wdn.shape, const, **single),
            pl.BlockSpec(g.shape, const),
            pl.BlockSpec(b.shape, const),
        ],
        out_specs=pl.BlockSpec((1, tm, d), lambda bi, j: (bi, j, 0)),
        out_shape=jax.ShapeDtypeStruct((bsz, s, d), F32),
        scratch_shapes=[
            pltpu.VMEM((2, tm + SUBLANES, cwid), F32),
            pltpu.VMEM((SUBLANES, 2 * dff), F32),
            pltpu.VMEM((tm, d), F32),
        ],
        compiler_params=_params(("arbitrary", "arbitrary"), 48),
        name="conv_ffn_ln",
    )(x, mod, wup, cw, cb, wdn, g, b)


def _inproj_weights(w):
    cols = np.arange(B_WIDTH)
    half = cols // (B_WIDTH // 2)
    perm = ((cols % (B_WIDTH // 2)) // (HEAD_DIM // 2)) * HEAD_DIM + half * (HEAD_DIM // 2) + cols % (HEAD_DIM // 2)
    za = w[:, 0:512]
    qb, kb, vb, gb = (w[:, 512 + i * B_WIDTH:512 + (i + 1) * B_WIDTH] for i in range(4))
    qc = w[:, 1536:2048]
    kcm, vcm, ksl, vsl, kwn, vwn = (w[:, 2048 + i * C_KV_WIDTH:2048 + (i + 1) * C_KV_WIDTH] for i in range(6))
    gc = w[:, 2816:2840]
    wn = jnp.concatenate([qb[:, perm], kb[:, perm], vb, gb, za, ksl, kwn, kcm, vcm], axis=1)
    pad = jnp.zeros((w.shape[0], PT_ROWS - PT_G - gc.shape[1]), w.dtype)
    wt = jnp.concatenate([qc, vsl, vwn, gc, pad], axis=1).T
    return wn.astype(BF16), wt.astype(BF16)


def kernel(x, c, w_ada, b_ada, w_in, a_ln_g, a_ln_b, a_ws, a_bs, b_gn_g, b_gn_b, c_pos_k, c_w1_k, c_w2_k,
           c_pos_v, c_w1_v, c_w2_v, w_out, ln1_g, ln1_b, w_up, conv_w, conv_b, w_down, ln2_g, ln2_b):
    depth = w_in.shape[0]
    bsz, s, d = x.shape
    alpha = (2 * depth) ** 0.25
    nseg = s // CMP_STRIDE
    ns = s // SEL_BLOCK

    lanes = np.arange(A_WIDTH)
    mavg = jnp.asarray((lanes[:, None] // HEAD_DIM == lanes[None, :] // HEAD_DIM) / HEAD_DIM, dtype=BF16)
    gm = jnp.asarray((lanes[None, :] // HEAD_DIM == np.arange(A_GROUPS)[:, None]).astype(np.float32))
    ret_consts = _retention_consts()
    cos, sin = _rotary_tables(s)
    ovt = _overlap_t(nseg, ns)

    mods = _ada_mod(c, w_ada, b_ada)
    for l in range(depth):
        mod = mods[l][:, None, :]
        wn, wt = _inproj_weights(w_in[l])
        pn, kvc, pt = _inproj(x, mod, wn, wt)

        wcat = jnp.transpose(a_ws[l], (1, 0, 2)).reshape(CHUNK, A_GROUPS * CHUNK)
        bias = jnp.repeat(a_bs[l].T, HEAD_DIM, axis=1)
        ya = _mixer_a(pn, wcat, bias, a_ln_g[l].reshape(1, A_WIDTH), a_ln_b[l].reshape(1, A_WIDTH), mavg, gm)

        yb = _retention(pn, cos, sin, ret_consts, mavg,
                        b_gn_g[l].reshape(1, B_WIDTH), b_gn_b[l].reshape(1, B_WIDTH))

        w1k = _blockdiag2(c_w1_k[l].reshape(CMP_BLOCK, HEAD_DIM, HEAD_DIM)).astype(BF16)
        w1v = _blockdiag2(c_w1_v[l].reshape(CMP_BLOCK, HEAD_DIM, HEAD_DIM)).astype(BF16)
        kc, vct = _compress(kvc, jnp.tile(c_pos_k[l], (1, C_KV_HEADS)), jnp.tile(c_pos_v[l], (1, C_KV_HEADS)),
                            w1k, w1v, _blockdiag2(c_w2_k[l]).astype(BF16), _blockdiag2(c_w2_v[l]).T.astype(BF16))
        yc = _nsa(pt, pn, kc, vct, ovt)

        x = _outproj(ya, yb, yc, x, mod, w_out[l].astype(BF16), ln1_g[l].reshape(1, d), ln1_b[l].reshape(1, d), alpha)
        x = _ffn(x, mod, w_up[l].astype(BF16), conv_w[l], conv_b[l].reshape(1, -1), w_down[l].astype(BF16),
                 ln2_g[l].reshape(1, d), ln2_b[l].reshape(1, d), alpha)
    return x
```

```python
import functools

import numpy as np
import jax
import jax.numpy as jnp
from jax import lax
from jax.experimental import pallas as pl
from jax.experimental.pallas import tpu as pltpu

F32 = jnp.float32
BF16 = jnp.bfloat16

HEAD_DIM = 64
A_GROUPS = 4
A_WIDTH = A_GROUPS * HEAD_DIM
CHUNK = 128
B_HEADS = 4
B_WIDTH = B_HEADS * HEAD_DIM
ROPE_BASE = 10000.0
C_HEADS = 8
C_KV_HEADS = 2
C_REP = C_HEADS // C_KV_HEADS
C_WIDTH = C_HEADS * HEAD_DIM
C_KV_WIDTH = C_KV_HEADS * HEAD_DIM
CMP_BLOCK = 32
CMP_STRIDE = 16
SEL_BLOCK = 64
N_SEL = 8
WINDOW = 512
CONV_WIDTH = 3
LN_EPS = 1e-5
NEG = -1e30
BIG = 1e30
LOG2E = 1.4426950408889634

LANES = 128
SUBLANES = 8
VMEM_BYTES = 64 * 1024 * 1024

PN_RET, PN_ZA, PN_KS, PN_KW, PN_KVC = 0, 1024, 1536, 1664, 1792
PN_COLS = 1792
PT_Q, PT_VS, PT_VW, PT_G = 0, 512, 640, 768
PT_ROWS = 800


def _dot(a, b):
    return jnp.dot(a, b, preferred_element_type=F32)


def _dot_nt(a, b):
    return lax.dot_general(a, b, (((1,), (1,)), ((), ())), preferred_element_type=F32)


def _dot_tn(a, b):
    return lax.dot_general(a, b, (((0,), (0,)), ((), ())), preferred_element_type=F32)


def _split_dot(x, m):
    hi = x.astype(BF16)
    lo = (x - hi.astype(F32)).astype(BF16)
    return _dot(hi, m) + _dot(lo, m)


def _split_dot_left(m, x):
    hi = x.astype(BF16)
    lo = (x - hi.astype(F32)).astype(BF16)
    return _dot(m, hi) + _dot(m, lo)


def _group_ln(x, mavg, g, b):
    mu = _split_dot(x, mavg)
    d = x - mu
    var = _split_dot(d * d, mavg)
    return d * lax.rsqrt(var + LN_EPS) * g + b


def _row_ln(x, g, b):
    mu = jnp.mean(x, axis=-1, keepdims=True)
    d = x - mu
    var = jnp.mean(d * d, axis=-1, keepdims=True)
    return d * lax.rsqrt(var + LN_EPS) * g + b


def _params(sem, vmem_mb):
    return pltpu.CompilerParams(dimension_semantics=sem, vmem_limit_bytes=vmem_mb * 1024 * 1024)


def _ada_kernel(c_ref, w_ref, b_ref, o_ref):
    cond = jax.nn.silu(c_ref[...]).astype(BF16)
    o_ref[0] = _dot(cond, w_ref[0].astype(BF16)) + b_ref[0]


def _ada_mod(c, w_ada, b_ada):
    depth, d, n = w_ada.shape
    bsz = c.shape[0]
    tn = 1536
    return pl.pallas_call(
        _ada_kernel,
        grid=(depth, n // tn),
        in_specs=[
            pl.BlockSpec((bsz, d), lambda l, j: (0, 0)),
            pl.BlockSpec((1, d, tn), lambda l, j: (l, 0, j)),
            pl.BlockSpec((1, 1, tn), lambda l, j: (l, 0, j)),
        ],
        out_specs=pl.BlockSpec((1, bsz, tn), lambda l, j: (l, 0, j)),
        out_shape=jax.ShapeDtypeStruct((depth, bsz, n), F32),
        compiler_params=_params(("arbitrary", "arbitrary"), 40),
        name="ada_mod",
    )(c, w_ada, b_ada.reshape(depth, 1, n))


def _inproj_kernel(x_ref, mod_ref, wn_ref, wt_ref, pn_ref, kvc_ref, pt_ref, *, d):
    sh = mod_ref[0, :, 0:d]
    sc = mod_ref[0, :, d:2 * d]
    h = (x_ref[0] * (1.0 + sc) + sh).astype(BF16)
    pn = _dot(h, wn_ref[...])
    pn_ref[0] = pn[:, :PN_COLS]
    kvc_ref[0] = pn[:, PN_KVC:]
    pt_ref[0] = _dot_nt(wt_ref[...], h)


def _inproj(x, mod, wn, wt, tm=512):
    bsz, s, d = x.shape
    const = lambda b, j: (0, 0)
    return pl.pallas_call(
        functools.partial(_inproj_kernel, d=d),
        grid=(bsz, s // tm),
        in_specs=[
            pl.BlockSpec((1, tm, d), lambda b, j: (b, j, 0)),
            pl.BlockSpec((1, 1, 6 * d), lambda b, j: (b, 0, 0)),
            pl.BlockSpec(wn.shape, const),
            pl.BlockSpec(wt.shape, const),
        ],
        out_specs=[
            pl.BlockSpec((1, tm, PN_COLS), lambda b, j: (b, j, 0)),
            pl.BlockSpec((1, tm, 2 * C_KV_WIDTH), lambda b, j: (b, j, 0)),
            pl.BlockSpec((1, PT_ROWS, tm), lambda b, j: (b, 0, j)),
        ],
        out_shape=[
            jax.ShapeDtypeStruct((bsz, s, PN_COLS), F32),
            jax.ShapeDtypeStruct((bsz, s, 2 * C_KV_WIDTH), F32),
            jax.ShapeDtypeStruct((bsz, PT_ROWS, s), F32),
        ],
        compiler_params=_params(("arbitrary", "arbitrary"), 48),
        name="inproj",
    )(x, mod, wn, wt)


def _mixer_a_kernel(za_ref, w_ref, bias_ref, g_ref, b_ref, mavg_ref, gm_ref, o_ref, *, n_chunks):
    row = lax.broadcasted_iota(jnp.int32, (CHUNK, A_GROUPS * CHUNK), 0)
    col = lax.broadcasted_iota(jnp.int32, (CHUNK, A_GROUPS * CHUNK), 1)
    wc = jnp.where((col % CHUNK) <= row, w_ref[...], 0.0).astype(BF16)
    for c in range(n_chunks):
        rows = slice(c * CHUNK, (c + 1) * CHUNK)
        z = jax.nn.gelu(za_ref[0, rows, :])
        u = z[:, :A_WIDTH]
        v = z[:, A_WIDTH:]
        vn = _group_ln(v, mavg_ref[...], g_ref[...], b_ref[...])
        vstack = jnp.concatenate([vn * gm_ref[g:g + 1, :] for g in range(A_GROUPS)], axis=0).astype(BF16)
        vs = _dot(wc, vstack) + bias_ref[...]
        o_ref[0, rows, :] = (u * vs).astype(o_ref.dtype)


def _mixer_a(pn, wcat, bias, g, b, mavg, gm, tb=512):
    bsz, s, _ = pn.shape
    const = lambda bi, j: (0, 0)
    return pl.pallas_call(
        functools.partial(_mixer_a_kernel, n_chunks=tb // CHUNK),
        grid=(bsz, s // tb),
        in_specs=[
            pl.BlockSpec((1, tb, 2 * A_WIDTH), lambda bi, j: (bi, j, PN_ZA // (2 * A_WIDTH))),
            pl.BlockSpec(wcat.shape, const),
            pl.BlockSpec(bias.shape, const),
            pl.BlockSpec(g.shape, const),
            pl.BlockSpec(b.shape, const),
            pl.BlockSpec(mavg.shape, const),
            pl.BlockSpec(gm.shape, const),
        ],
        out_specs=pl.BlockSpec((1, tb, A_WIDTH), lambda bi, j: (bi, j, 0)),
        out_shape=jax.ShapeDtypeStruct((bsz, s, A_WIDTH), BF16),
        compiler_params=_params(("arbitrary", "arbitrary"), 32),
        name="mixer_a",
    )(pn, wcat, bias, g, b, mavg, gm)


def _retention_kernel(x_ref, cos_ref, sin_ref, dec_ref, zeta_ref, xi_ref, cd_ref, qm_ref, vm_ref, bm_ref,
                      mavg_ref, g_ref, b_ref, o_ref, state_ref, *, n_chunks):
    @pl.when(pl.program_id(1) == 0)
    def _():
        state_ref[...] = jnp.zeros_like(state_ref)

    half = B_WIDTH // 2
    for c in range(n_chunks):
        rows = slice(c * CHUNK, (c + 1) * CHUNK)
        q = x_ref[0, rows, 0:B_WIDTH]
        k = x_ref[0, rows, B_WIDTH:2 * B_WIDTH]
        v = x_ref[0, rows, 2 * B_WIDTH:3 * B_WIDTH]
        gate = x_ref[0, rows, 3 * B_WIDTH:4 * B_WIDTH]
        cos = cos_ref[rows, :]
        sin = sin_ref[rows, :]

        def rot(t):
            t1 = t[:, :half]
            t2 = t[:, half:]
            return jnp.concatenate([t1 * cos - t2 * sin, t1 * sin + t2 * cos], axis=1)

        qr = rot(q)
        kr = rot(k) * (HEAD_DIM ** -0.5)
        qs = jnp.concatenate([qr * qm_ref[h:h + 1, :] for h in range(B_HEADS)], axis=0).astype(BF16)
        s = _dot_nt(qs, kr.astype(BF16)) * dec_ref[...]
        scat = jnp.concatenate([s[h * CHUNK:(h + 1) * CHUNK, :] for h in range(B_HEADS)], axis=1).astype(BF16)
        vstack = jnp.concatenate([v * vm_ref[h:h + 1, :] for h in range(B_HEADS)], axis=0).astype(BF16)
        o_inner = _dot(scat, vstack)
        state = state_ref[...]
        o_cross = _dot(qr.astype(BF16), state.astype(BF16)) * xi_ref[...]
        kz = (kr * zeta_ref[...]).astype(BF16)
        kv = _dot_tn(kz, v.astype(BF16)) * bm_ref[...]
        state_ref[...] = state * cd_ref[...] + kv
        o = _group_ln(o_inner + o_cross, mavg_ref[...], g_ref[...], b_ref[...])
        o_ref[0, rows, :] = (jax.nn.silu(gate) * o).astype(o_ref.dtype)


def _retention(pn, cos, sin, consts, mavg, g, b, tb=512):
    bsz, s, _ = pn.shape
    const = lambda bi, j: (0, 0)
    return pl.pallas_call(
        functools.partial(_retention_kernel, n_chunks=tb // CHUNK),
        grid=(bsz, s // tb),
        in_specs=[
            pl.BlockSpec((1, tb, 4 * B_WIDTH), lambda bi, j: (bi, j, PN_RET // (4 * B_WIDTH))),
            pl.BlockSpec((tb, B_WIDTH // 2), lambda bi, j: (j, 0)),
            pl.BlockSpec((tb, B_WIDTH // 2), lambda bi, j: (j, 0)),
        ] + [pl.BlockSpec(a.shape, const) for a in consts] + [
            pl.BlockSpec(mavg.shape, const),
            pl.BlockSpec(g.shape, const),
            pl.BlockSpec(b.shape, const),
        ],
        out_specs=pl.BlockSpec((1, tb, B_WIDTH), lambda bi, j: (bi, j, 0)),
        out_shape=jax.ShapeDtypeStruct((bsz, s, B_WIDTH), BF16),
        scratch_shapes=[pltpu.VMEM((B_WIDTH, B_WIDTH), F32)],
        compiler_params=_params(("arbitrary", "arbitrary"), 32),
        name="retention",
    )(pn, cos, sin, *consts, mavg, g, b)


def _retention_consts():
    h_n, d, l_n = B_HEADS, HEAD_DIM, CHUNK
    log_gamma = jnp.log1p(-jnp.exp2(-5.0 - jnp.arange(h_n, dtype=F32)))
    idx = jnp.arange(l_n, dtype=F32)
    diff = idx[:, None] - idx[None, :]
    decay_in = jnp.where(diff >= 0, jnp.exp(log_gamma[:, None, None] * jnp.maximum(diff, 0.0)), 0.0)
    xi = jnp.exp(log_gamma[:, None] * (idx + 1.0))
    zeta = jnp.exp(log_gamma[:, None] * (l_n - 1.0 - idx))
    chunk_decay = jnp.exp(log_gamma * l_n)
    cols = np.arange(B_WIDTH)
    head_perm = (cols % (B_WIDTH // 2)) // (d // 2)
    head_std = cols // d
    dec = decay_in.reshape(h_n * l_n, l_n)
    zeta_t = zeta.T[:, head_perm]
    xi_t = xi.T[:, head_std]
    cd = chunk_decay[head_std][None, :]
    qm = jnp.asarray((head_perm[None, :] == np.arange(h_n)[:, None]).astype(np.float32))
    vm = jnp.asarray((head_std[None, :] == np.arange(h_n)[:, None]).astype(np.float32))
    bm = jnp.asarray((head_perm[:, None] == head_std[None, :]).astype(np.float32))
    return [dec, zeta_t, xi_t, cd, qm, vm, bm]


def _rotary_tables(s):
    half = HEAD_DIM // 2
    inv = jnp.power(ROPE_BASE, -jnp.arange(half, dtype=F32) / half)
    ang = jnp.arange(s).astype(F32)[:, None] * inv[None, :]
    return jnp.tile(jnp.cos(ang), (1, B_HEADS)), jnp.tile(jnp.sin(ang), (1, B_HEADS))


def _compress_kernel(xk_ref, xv_ref, posk_ref, posv_ref, w1k_ref, w1v_ref, w2k_ref, w2vt_ref, kc_ref, vct_ref,
                     *, nseg):
    half = CMP_BLOCK // 2
    acc = [jnp.zeros((nseg, C_KV_WIDTH), F32) for _ in range(4)]
    for l in range(half):
        xk = xk_ref[0, pl.ds(l, nseg, stride=CMP_STRIDE), :]
        xv = xv_ref[0, pl.ds(l, nseg, stride=CMP_STRIDE), :]
        acc[0] += _dot((xk + posk_ref[l:l + 1, :]).astype(BF16), w1k_ref[l])
        acc[1] += _dot((xk + posk_ref[half + l:half + l + 1, :]).astype(BF16), w1k_ref[half + l])
        acc[2] += _dot((xv + posv_ref[l:l + 1, :]).astype(BF16), w1v_ref[l])
        acc[3] += _dot((xv + posv_ref[half + l:half + l + 1, :]).astype(BF16), w1v_ref[half + l])
    hk = jax.nn.gelu(acc[0] + pltpu.roll(acc[1], nseg - 1, 0))
    hv = jax.nn.gelu(acc[2] + pltpu.roll(acc[3], nseg - 1, 0))
    kc_ref[0] = _dot(hk.astype(BF16), w2k_ref[...])
    vct_ref[0] = _dot_nt(w2vt_ref[...], hv.astype(BF16))


def _compress(kvc, posk, posv, w1k, w1v, w2k, w2v):
    bsz, s, width = kvc.shape
    nseg = s // CMP_STRIDE
    c2 = lambda bi: (0, 0)
    c3 = lambda bi: (0, 0, 0)
    return pl.pallas_call(
        functools.partial(_compress_kernel, nseg=nseg),
        grid=(bsz,),
        in_specs=[
            pl.BlockSpec((1, s, C_KV_WIDTH), lambda bi: (bi, 0, 0)),
            pl.BlockSpec((1, s, C_KV_WIDTH), lambda bi: (bi, 0, 1)),
            pl.BlockSpec(posk.shape, c2),
            pl.BlockSpec(posv.shape, c2),
            pl.BlockSpec(w1k.shape, c3),
            pl.BlockSpec(w1v.shape, c3),
            pl.BlockSpec(w2k.shape, c2),
            pl.BlockSpec(w2v.shape, c2),
        ],
        out_specs=[
            pl.BlockSpec((1, nseg, C_KV_WIDTH), lambda bi: (bi, 0, 0)),
            pl.BlockSpec((1, C_KV_WIDTH, nseg), lambda bi: (bi, 0, 0)),
        ],
        out_shape=[
            jax.ShapeDtypeStruct((bsz, nseg, C_KV_WIDTH), F32),
            jax.ShapeDtypeStruct((bsz, C_KV_WIDTH, nseg), F32),
        ],
        compiler_params=_params(("arbitrary",), 32),
        name="nsa_compress",
    )(kvc, kvc, posk, posv, w1k, w1v, w2k, w2v)


def _blockdiag2(w):
    z = jnp.zeros_like(w)
    return jnp.concatenate([jnp.concatenate([w, z], axis=-1), jnp.concatenate([z, w], axis=-1)], axis=-2)


def _overlap_t(nseg, ns):
    nc = nseg - 1
    c0 = np.arange(nc)[None, :] * CMP_STRIDE
    s0 = np.arange(ns)[:, None] * SEL_BLOCK
    ov = np.clip(np.minimum(c0 + CMP_BLOCK, s0 + SEL_BLOCK) - np.maximum(c0, s0), 0, None) / CMP_BLOCK
    out = np.zeros((ns, nseg), np.float32)
    out[:, :nc] = ov
    return jnp.asarray(out, dtype=BF16)


def _nsa_kernel(qt_ref, gt_ref, ks_ref, vst_ref, kw_ref, vwt_ref, kc_ref, vct_ref, ovt_ref, o_ref,
                qall_ref, selb_ref, m_ref, l_ref, alpha_ref, acc_ref, res_ref, sbuf_ref, pbuf_ref,
                cbuf_ref, pcmp_ref, *, nseg, ns, n_sel):
    i = pl.program_id(1)
    s0 = i * CHUNK
    blk_w = 2 * CHUNK
    n_blk = C_HEADS // 2
    blk_per_g = n_blk // C_KV_HEADS
    zero = jnp.zeros((HEAD_DIM, CHUNK), F32)
    for h in range(C_HEADS):
        qh = qt_ref[0, h * HEAD_DIM:(h + 1) * HEAD_DIM, :] * (HEAD_DIM ** -0.5 * LOG2E)
        blk = jnp.concatenate([qh, zero] if h < C_REP else [zero, qh], axis=0)
        qall_ref[:, h * CHUNK:(h + 1) * CHUNK] = blk.astype(BF16)

    krow = lax.broadcasted_iota(jnp.int32, (CHUNK, blk_w), 0)
    t_loc = lax.broadcasted_iota(jnp.int32, (CHUNK, blk_w), 1) % CHUNK
    tri_diag = jnp.where(krow <= t_loc, 0.0, NEG)
    tri_old = jnp.where(krow > t_loc, 0.0, NEG)

    def cmp_scores():
        cbuf_ref[...] = _dot(kc_ref[0].astype(BF16), qall_ref[...])

    def cmp_softmax():
        crow = lax.broadcasted_iota(jnp.int32, (nseg, blk_w), 0)
        c_t = s0 + lax.broadcasted_iota(jnp.int32, (nseg, blk_w), 1) % CHUNK
        cbias = jnp.where(crow * CMP_STRIDE + (CMP_BLOCK - 1) <= c_t, 0.0, NEG)
        psum = [None] * C_KV_HEADS
        for b in range(n_blk):
            g = b // blk_per_g
            cols = slice(b * blk_w, (b + 1) * blk_w)
            sc = cbuf_ref[:, cols] + cbias
            mx = jnp.max(sc, axis=0, keepdims=True)
            e = jnp.exp2(sc - mx)
            den = jnp.sum(e, axis=0, keepdims=True)
            p = e * jnp.where(mx > 0.5 * NEG, 1.0 / den, 0.0)
            pcmp_ref[:, cols] = p.astype(BF16)
            both = p[:, :CHUNK] + p[:, CHUNK:]
            psum[g] = both if psum[g] is None else psum[g] + both
        return [_split_dot_left(ovt_ref[...], ps) for ps in psum]

    def cmp_output():
        vct = vct_ref[0].astype(BF16)
        for b in range(n_blk):
            g = b // blk_per_g
            cols = slice(b * blk_w, (b + 1) * blk_w)
            res_ref[0, :, cols] = _dot(vct[g * HEAD_DIM:(g + 1) * HEAD_DIM, :], pcmp_ref[:, cols])

    def select_blocks(imps):
        j = lax.broadcasted_iota(jnp.int32, (ns, CHUNK), 0)
        cur = (s0 + lax.broadcasted_iota(jnp.int32, (ns, CHUNK), 1)) // SEL_BLOCK
        forced = (j == 0) | (j == cur) | (j == cur - 1)
        future = j > cur
        for g in range(C_KV_HEADS):
            imp = jnp.where(forced, BIG, jnp.where(future, NEG, imps[g]))
            rank = jnp.zeros((ns, CHUNK), F32)
            for i2 in range(ns):
                r_i = imp[i2:i2 + 1, :]
                beats = (r_i > imp) | ((r_i == imp) & (j > i2))
                rank = rank + jnp.where(beats, 1.0, 0.0)
            sel_bias = jnp.where(rank < n_sel, 0.0, NEG)
            for r in range(C_REP):
                h = g * C_REP + r
                selb_ref[:, h * CHUNK:(h + 1) * CHUNK] = sel_bias

    def reset():
        m_ref[...] = jnp.full(m_ref.shape, NEG, F32)
        l_ref[...] = jnp.zeros(l_ref.shape, F32)
        acc_ref[...] = jnp.zeros(acc_ref.shape, F32)
        alpha_ref[...] = jnp.ones(alpha_ref.shape, F32)
        pbuf_ref[...] = jnp.zeros(pbuf_ref.shape, BF16)

    def scores(k_ref, c, slot):
        off = pl.multiple_of(c * CHUNK, CHUNK)
        sbuf_ref[slot] = _dot(k_ref[0, pl.ds(off, CHUNK), :].astype(BF16), qall_ref[...])

    def softmax(slot, sel_c, tri, extra):
        for b in range(n_blk):
            cols = slice(b * blk_w, (b + 1) * blk_w)
            s = sbuf_ref[slot, :, cols]
            if sel_c is not None:
                b0 = selb_ref[pl.ds(2 * sel_c, 1), cols]
                b1 = selb_ref[pl.ds(2 * sel_c + 1, 1), cols]
                s = jnp.concatenate([s[:SEL_BLOCK] + b0, s[SEL_BLOCK:] + b1], axis=0)
            if tri is not None:
                s = s + tri
            if extra is not None:
                s = s + extra
            m_old = m_ref[:, cols]
            m_new = jnp.maximum(m_old, jnp.max(s, axis=0, keepdims=True))
            p = jnp.exp2(s - m_new)
            alpha = jnp.exp2(m_old - m_new)
            alpha_ref[:, cols] = alpha
            l_ref[:, cols] = alpha * l_ref[:, cols] + jnp.sum(p, axis=0, keepdims=True)
            pbuf_ref[:, cols] = p.astype(BF16)
            m_ref[:, cols] = m_new

    def accumulate(vt_ref, c):
        off = pl.multiple_of(c * CHUNK, CHUNK)
        vt = vt_ref[0, :, pl.ds(off, CHUNK)].astype(BF16)
        for b in range(n_blk):
            g = b // blk_per_g
            cols = slice(b * blk_w, (b + 1) * blk_w)
            pv = _dot(vt[g * HEAD_DIM:(g + 1) * HEAD_DIM, :], pbuf_ref[:, cols])
            acc_ref[:, cols] = alpha_ref[:, cols] * acc_ref[:, cols] + pv

    def finish(slot):
        l_fin = l_ref[...]
        res_ref[slot] = acc_ref[...] * jnp.where(l_fin > 0.0, 1.0 / l_fin, 0.0)

    n_back = WINDOW // CHUNK
    cmp_scores()
    reset()
    chunk = [jnp.maximum(i - n_back + w, 0) for w in range(n_back + 1)]
    gone = [jnp.where(i - n_back + w < 0, NEG, 0.0) for w in range(n_back)]
    scores(kw_ref, chunk[0], 0)
    imps = None
    for w in range(n_back + 1):
        if w >= 1:
            accumulate(vwt_ref, chunk[w - 1])
        if w == 0:
            imps = cmp_softmax()
        tri = tri_old if w == 0 else (tri_diag if w == n_back else None)
        softmax(w % 2, None, tri, gone[w] if w < n_back else None)
        if w < n_back:
            scores(kw_ref, chunk[w + 1], (w + 1) % 2)
        if w == 1:
            select_blocks(imps)
    accumulate(vwt_ref, chunk[n_back])
    cmp_output()
    finish(2)

    reset()
    scores(ks_ref, 0, 0)

    def sel_body(k, carry):
        accumulate(vst_ref, jnp.maximum(k - 1, 0))
        softmax(k % 2, k, None, None)
        scores(ks_ref, k + 1, (k + 1) % 2)
        return carry

    lax.fori_loop(0, i, sel_body, 0)
    accumulate(vst_ref, jnp.maximum(i - 1, 0))
    softmax(i % 2, i, tri_diag, None)
    accumulate(vst_ref, i)
    finish(1)

    gates = jax.nn.sigmoid(gt_ref[0])
    for pair in range(C_HEADS // 2):
        pieces = []
        for h in (2 * pair, 2 * pair + 1):
            cs = slice(h * CHUNK, (h + 1) * CHUNK)
            pieces.append(gates[3 * h:3 * h + 1, :] * res_ref[0, :, cs]
                          + gates[3 * h + 1:3 * h + 2, :] * res_ref[1, :, cs]
                          + gates[3 * h + 2:3 * h + 3, :] * res_ref[2, :, cs])
        o_ref[0, :, pair * 2 * HEAD_DIM:(pair + 1) * 2 * HEAD_DIM] = jnp.concatenate(pieces, axis=0).T.astype(o_ref.dtype)


def _nsa(pt, pn, kc, vct, ovt):
    bsz, s, _ = pn.shape
    nseg = s // CMP_STRIDE
    ns = s // SEL_BLOCK
    width = C_HEADS * CHUNK
    g_rows = PT_ROWS - PT_G
    return pl.pallas_call(
        functools.partial(_nsa_kernel, nseg=nseg, ns=ns, n_sel=min(N_SEL, ns)),
        grid=(bsz, s // CHUNK),
        in_specs=[
            pl.BlockSpec((1, C_WIDTH, CHUNK), lambda b, i: (b, PT_Q // C_WIDTH, i)),
            pl.BlockSpec((1, g_rows, CHUNK), lambda b, i: (b, PT_G // g_rows, i)),
            pl.BlockSpec((1, s, C_KV_WIDTH), lambda b, i: (b, 0, PN_KS // C_KV_WIDTH)),
            pl.BlockSpec((1, C_KV_WIDTH, s), lambda b, i: (b, PT_VS // C_KV_WIDTH, 0)),
            pl.BlockSpec((1, s, C_KV_WIDTH), lambda b, i: (b, 0, PN_KW // C_KV_WIDTH)),
            pl.BlockSpec((1, C_KV_WIDTH, s), lambda b, i: (b, PT_VW // C_KV_WIDTH, 0)),
            pl.BlockSpec((1, nseg, C_KV_WIDTH), lambda b, i: (b, 0, 0)),
            pl.BlockSpec((1, C_KV_WIDTH, nseg), lambda b, i: (b, 0, 0)),
            pl.BlockSpec(ovt.shape, lambda b, i: (0, 0)),
        ],
        out_specs=pl.BlockSpec((1, CHUNK, C_WIDTH), lambda b, i: (b, i, 0)),
        out_shape=jax.ShapeDtypeStruct((bsz, s, C_WIDTH), BF16),
        scratch_shapes=[
            pltpu.VMEM((C_KV_WIDTH, width), BF16),
            pltpu.VMEM((ns, width), F32),
            pltpu.VMEM((1, width), F32),
            pltpu.VMEM((1, width), F32),
            pltpu.VMEM((1, width), F32),
            pltpu.VMEM((HEAD_DIM, width), F32),
            pltpu.VMEM((3, HEAD_DIM, width), F32),
            pltpu.VMEM((2, CHUNK, width), F32),
            pltpu.VMEM((CHUNK, width), BF16),
            pltpu.VMEM((nseg, width), F32),
            pltpu.VMEM((nseg, width), BF16),
        ],
        compiler_params=_params(("arbitrary", "arbitrary"), 40),
        name="nsa_attention",
    )(pt, pt, pn, pt, pn, pt, kc, vct, ovt)


def _tail_kernel(ya_ref, yb_ref, yc_ref, x_ref, mod_ref, wo_ref, g1_ref, b1_ref,
                 wup_ref, cw_ref, cb_ref, wdn_ref, g2_ref, b2_ref, o_ref,
                 work_ref, carry_ref, x1_ref, h_ref, act_ref, *, d, dff, cwid, tm, alpha):
    @pl.when(pl.program_id(1) == 0)
    def _():
        carry_ref[...] = jnp.zeros_like(carry_ref)

    y1 = _dot(ya_ref[0], wo_ref[0:A_WIDTH, :])
    y1 += _dot(yb_ref[0], wo_ref[A_WIDTH:A_WIDTH + B_WIDTH, :])
    y1 += _dot(yc_ref[0], wo_ref[A_WIDTH + B_WIDTH:, :])
    x1 = _row_ln(alpha * x_ref[0] + mod_ref[0, :, 2 * d:3 * d] * y1, g1_ref[...], b1_ref[...])
    x1_ref[...] = x1

    sh = mod_ref[0, :, 3 * d:4 * d]
    sc = mod_ref[0, :, 4 * d:5 * d]
    gate = mod_ref[0, :, 5 * d:6 * d]
    h_ref[...] = (x1 * (1.0 + sc) + sh).astype(BF16)
    pad = SUBLANES
    n_chunks = dff // cwid

    def up(ci):
        slot = ci % 2
        for part in range(2):
            cs = slice(part * dff + ci * cwid, part * dff + (ci + 1) * cwid)
            work_ref[slot, part, 0:pad, :] = carry_ref[:, cs]
            work_ref[slot, part, pad:pad + tm, :] = _dot(h_ref[...], wup_ref[:, cs])

    def conv_act(ci):
        slot = ci % 2
        conv = []
        for part in range(2):
            cs = slice(part * dff + ci * cwid, part * dff + (ci + 1) * cwid)
            buf = work_ref[slot, part]
            carry_ref[:, cs] = buf[tm:tm + pad, :]
            a = buf[pad:, :]
            a1 = pltpu.roll(buf, 1, 0)[pad:, :]
            a2 = pltpu.roll(buf, 2, 0)[pad:, :]
            conv.append(cw_ref[0:1, cs] * a2 + cw_ref[1:2, cs] * a1 + cw_ref[2:3, cs] * a + cb_ref[:, cs])
        half = 0.5 * conv[0]
        silu = half + half * jnp.tanh(half)
        act_ref[:, ci * cwid:(ci + 1) * cwid] = (silu * conv[1]).astype(BF16)

    split = (n_chunks // 2) * cwid
    up(0)
    y = None
    for ci in range(n_chunks):
        if ci + 1 < n_chunks:
            up(ci + 1)
        conv_act(ci)
        if (ci + 1) * cwid == split:
            y = _dot(act_ref[:, :split], wdn_ref[:split, :])
    y = y + _dot(act_ref[:, split:], wdn_ref[split:, :])
    o_ref[0] = _row_ln(alpha * x1_ref[...] + gate * y, g2_ref[...], b2_ref[...])


def _tail(ya, yb, yc, x, mod, layer, wo, g1, b1, wup, cw, cb, wdn, g2, b2, alpha, tm=512, cwid=256):
    bsz, s, d = x.shape
    dff = wdn.shape[1]
    const = lambda bi, j: (0, 0)
    tile = lambda w: pl.BlockSpec((1, tm, w), lambda bi, j: (bi, j, 0))
    weight = lambda w: pl.BlockSpec((None,) + w.shape[1:], lambda bi, j: (layer, 0, 0),
                                    pipeline_mode=pl.Buffered(1))
    small = lambda a: pl.BlockSpec(a.shape, const)
    return pl.pallas_call(
        functools.partial(_tail_kernel, d=d, dff=dff, cwid=cwid, tm=tm, alpha=alpha),
        grid=(bsz, s // tm),
        in_specs=[
            tile(A_WIDTH), tile(B_WIDTH), tile(C_WIDTH), tile(d),
            pl.BlockSpec((1, 1, 6 * d), lambda bi, j: (bi, 0, 0)),
            weight(wo), small(g1), small(b1),
            weight(wup), small(cw), small(cb), weight(wdn), small(g2), small(b2),
        ],
        out_specs=tile(d),
        out_shape=jax.ShapeDtypeStruct((bsz, s, d), F32),
        scratch_shapes=[
            pltpu.VMEM((2, 2, tm + SUBLANES, cwid), F32),
            pltpu.VMEM((SUBLANES, 2 * dff), F32),
            pltpu.VMEM((tm, d), F32),
            pltpu.VMEM((tm, d), BF16),
            pltpu.VMEM((tm, dff), BF16),
        ],
        compiler_params=_params(("arbitrary", "arbitrary"), 52),
        name="layer_tail",
    )(ya, yb, yc, x, mod, wo, g1, b1, wup, cw, cb, wdn, g2, b2)


def _inproj_weights(w):
    cols = np.arange(B_WIDTH)
    half = cols // (B_WIDTH // 2)
    perm = ((cols % (B_WIDTH // 2)) // (HEAD_DIM // 2)) * HEAD_DIM + half * (HEAD_DIM // 2) + cols % (HEAD_DIM // 2)
    za = w[:, 0:512]
    qb, kb, vb, gb = (w[:, 512 + i * B_WIDTH:512 + (i + 1) * B_WIDTH] for i in range(4))
    qc = w[:, 1536:2048]
    kcm, vcm, ksl, vsl, kwn, vwn = (w[:, 2048 + i * C_KV_WIDTH:2048 + (i + 1) * C_KV_WIDTH] for i in range(6))
    gc = w[:, 2816:2840]
    wn = jnp.concatenate([qb[:, perm], kb[:, perm], vb, gb, za, ksl, kwn, kcm, vcm], axis=1)
    pad = jnp.zeros((w.shape[0], PT_ROWS - PT_G - gc.shape[1]), w.dtype)
    wt = jnp.concatenate([qc, vsl, vwn, gc, pad], axis=1).T
    return wn.astype(BF16), wt.astype(BF16)


def kernel(x, c, w_ada, b_ada, w_in, a_ln_g, a_ln_b, a_ws, a_bs, b_gn_g, b_gn_b, c_pos_k, c_w1_k, c_w2_k,
           c_pos_v, c_w1_v, c_w2_v, w_out, ln1_g, ln1_b, w_up, conv_w, conv_b, w_down, ln2_g, ln2_b):
    depth = w_in.shape[0]
    bsz, s, d = x.shape
    alpha = (2 * depth) ** 0.25
    nseg = s // CMP_STRIDE
    ns = s // SEL_BLOCK

    lanes = np.arange(A_WIDTH)
    mavg = jnp.asarray((lanes[:, None] // HEAD_DIM == lanes[None, :] // HEAD_DIM) / HEAD_DIM, dtype=BF16)
    gm = jnp.asarray((lanes[None, :] // HEAD_DIM == np.arange(A_GROUPS)[:, None]).astype(np.float32))
    ret_consts = _retention_consts()
    cos, sin = _rotary_tables(s)
    ovt = _overlap_t(nseg, ns)

    wo_bf, wup_bf, wdn_bf = w_out.astype(BF16), w_up.astype(BF16), w_down.astype(BF16)
    mods = _ada_mod(c, w_ada, b_ada)
    for l in range(depth):
        mod = mods[l][:, None, :]
        wn, wt = _inproj_weights(w_in[l])
        pn, kvc, pt = _inproj(x, mod, wn, wt)

        wcat = jnp.transpose(a_ws[l], (1, 0, 2)).reshape(CHUNK, A_GROUPS * CHUNK)
        bias = jnp.repeat(a_bs[l].T, HEAD_DIM, axis=1)
        ya = _mixer_a(pn, wcat, bias, a_ln_g[l].reshape(1, A_WIDTH), a_ln_b[l].reshape(1, A_WIDTH), mavg, gm)

        yb = _retention(pn, cos, sin, ret_consts, mavg,
                        b_gn_g[l].reshape(1, B_WIDTH), b_gn_b[l].reshape(1, B_WIDTH))

        w1k = _blockdiag2(c_w1_k[l].reshape(CMP_BLOCK, HEAD_DIM, HEAD_DIM)).astype(BF16)
        w1v = _blockdiag2(c_w1_v[l].reshape(CMP_BLOCK, HEAD_DIM, HEAD_DIM)).astype(BF16)
        kc, vct = _compress(kvc, jnp.tile(c_pos_k[l], (1, C_KV_HEADS)), jnp.tile(c_pos_v[l], (1, C_KV_HEADS)),
                            w1k, w1v, _blockdiag2(c_w2_k[l]).astype(BF16), _blockdiag2(c_w2_v[l]).T.astype(BF16))
        yc = _nsa(pt, pn, kc, vct, ovt)

        x = _tail(ya, yb, yc, x, mod, l, wo_bf, ln1_g[l].reshape(1, d), ln1_b[l].reshape(1, d),
                  wup_bf, conv_w[l], conv_b[l].reshape(1, -1), wdn_bf,
                  ln2_g[l].reshape(1, d), ln2_b[l].reshape(1, d), alpha)
    return x
```

```python
import functools

import numpy as np
import jax
import jax.numpy as jnp
from jax import lax
from jax.experimental import pallas as pl
from jax.experimental.pallas import tpu as pltpu

F32 = jnp.float32
BF16 = jnp.bfloat16

HEAD_DIM = 64
A_GROUPS = 4
A_WIDTH = A_GROUPS * HEAD_DIM
CHUNK = 128
B_HEADS = 4
B_WIDTH = B_HEADS * HEAD_DIM
ROPE_BASE = 10000.0
C_HEADS = 8
C_KV_HEADS = 2
C_REP = C_HEADS // C_KV_HEADS
C_WIDTH = C_HEADS * HEAD_DIM
C_KV_WIDTH = C_KV_HEADS * HEAD_DIM
CMP_BLOCK = 32
CMP_STRIDE = 16
SEL_BLOCK = 64
N_SEL = 8
WINDOW = 512
CONV_WIDTH = 3
LN_EPS = 1e-5
NEG = -1e30
BIG = 1e30
LOG2E = 1.4426950408889634

LANES = 128
SUBLANES = 8
VMEM_BYTES = 64 * 1024 * 1024

PN_RET, PN_ZA, PN_KS, PN_KW, PN_KVC = 0, 1024, 1536, 1664, 1792
PN_COLS = 1536
PT_Q, PT_VS, PT_VW, PT_G = 0, 512, 640, 768
PT_ROWS = 800


def _dot(a, b):
    return jnp.dot(a, b, preferred_element_type=F32)


def _dot_nt(a, b):
    return lax.dot_general(a, b, (((1,), (1,)), ((), ())), preferred_element_type=F32)


def _dot_tn(a, b):
    return lax.dot_general(a, b, (((0,), (0,)), ((), ())), preferred_element_type=F32)


def _split_dot(x, m):
    hi = x.astype(BF16)
    lo = (x - hi.astype(F32)).astype(BF16)
    return _dot(hi, m) + _dot(lo, m)


def _split_dot_left(m, x):
    hi = x.astype(BF16)
    lo = (x - hi.astype(F32)).astype(BF16)
    return _dot(m, hi) + _dot(m, lo)


def _group_ln(x, mavg, g, b):
    mu = _split_dot(x, mavg)
    d = x - mu
    var = _split_dot(d * d, mavg)
    return d * lax.rsqrt(var + LN_EPS) * g + b


def _row_ln(x, g, b):
    mu = jnp.mean(x, axis=-1, keepdims=True)
    d = x - mu
    var = jnp.mean(d * d, axis=-1, keepdims=True)
    return d * lax.rsqrt(var + LN_EPS) * g + b


def _params(sem, vmem_mb):
    return pltpu.CompilerParams(dimension_semantics=sem, vmem_limit_bytes=vmem_mb * 1024 * 1024)


def _ada_kernel(c_ref, w_ref, b_ref, o_ref):
    cond = jax.nn.silu(c_ref[...]).astype(BF16)
    o_ref[0] = _dot(cond, w_ref[0].astype(BF16)) + b_ref[0]


def _ada_mod(c, w_ada, b_ada):
    depth, d, n = w_ada.shape
    bsz = c.shape[0]
    tn = 1536
    return pl.pallas_call(
        _ada_kernel,
        grid=(depth, n // tn),
        in_specs=[
            pl.BlockSpec((bsz, d), lambda l, j: (0, 0)),
            pl.BlockSpec((1, d, tn), lambda l, j: (l, 0, j)),
            pl.BlockSpec((1, 1, tn), lambda l, j: (l, 0, j)),
        ],
        out_specs=pl.BlockSpec((1, bsz, tn), lambda l, j: (l, 0, j)),
        out_shape=jax.ShapeDtypeStruct((depth, bsz, n), F32),
        compiler_params=_params(("arbitrary", "arbitrary"), 40),
        name="ada_mod",
    )(c, w_ada, b_ada.reshape(depth, 1, n))


def _inproj_kernel(x_ref, mod_ref, wn_ref, wt_ref, pn_ref, ksw_ref, kvc_ref, ptq_ref, gt_ref, *, d):
    sh = mod_ref[0, :, 0:d]
    sc = mod_ref[0, :, d:2 * d]
    h = (x_ref[0] * (1.0 + sc) + sh).astype(BF16)
    pn = _dot(h, wn_ref[...])
    pn_ref[0] = pn[:, :PN_COLS]
    ksw_ref[0] = pn[:, PN_COLS:PN_KVC].astype(BF16)
    kvc_ref[0] = pn[:, PN_KVC:]
    pt = _dot_nt(wt_ref[...], h)
    ptq_ref[0] = pt[:PT_G, :].astype(BF16)
    gt_ref[0] = pt[PT_G:, :]


def _inproj(x, mod, wn, wt, tm=512):
    bsz, s, d = x.shape
    const = lambda b, j: (0, 0)
    return pl.pallas_call(
        functools.partial(_inproj_kernel, d=d),
        grid=(bsz, s // tm),
        in_specs=[
            pl.BlockSpec((1, tm, d), lambda b, j: (b, j, 0)),
            pl.BlockSpec((1, 1, 6 * d), lambda b, j: (b, 0, 0)),
            pl.BlockSpec(wn.shape, const),
            pl.BlockSpec(wt.shape, const),
        ],
        out_specs=[
            pl.BlockSpec((1, tm, PN_COLS), lambda b, j: (b, j, 0)),
            pl.BlockSpec((1, tm, 2 * C_KV_WIDTH), lambda b, j: (b, j, 0)),
            pl.BlockSpec((1, tm, 2 * C_KV_WIDTH), lambda b, j: (b, j, 0)),
            pl.BlockSpec((1, PT_G, tm), lambda b, j: (b, 0, j)),
            pl.BlockSpec((1, PT_ROWS - PT_G, tm), lambda b, j: (b, 0, j)),
        ],
        out_shape=[
            jax.ShapeDtypeStruct((bsz, s, PN_COLS), F32),
            jax.ShapeDtypeStruct((bsz, s, 2 * C_KV_WIDTH), BF16),
            jax.ShapeDtypeStruct((bsz, s, 2 * C_KV_WIDTH), F32),
            jax.ShapeDtypeStruct((bsz, PT_G, s), BF16),
            jax.ShapeDtypeStruct((bsz, PT_ROWS - PT_G, s), F32),
        ],
        compiler_params=_params(("arbitrary", "arbitrary"), 48),
        name="inproj",
    )(x, mod, wn, wt)


def _mixer_a_kernel(za_ref, w_ref, bias_ref, g_ref, b_ref, mavg_ref, gm_ref, o_ref, *, n_chunks):
    row = lax.broadcasted_iota(jnp.int32, (CHUNK, A_GROUPS * CHUNK), 0)
    col = lax.broadcasted_iota(jnp.int32, (CHUNK, A_GROUPS * CHUNK), 1)
    wc = jnp.where((col % CHUNK) <= row, w_ref[...], 0.0).astype(BF16)
    for c in range(n_chunks):
        rows = slice(c * CHUNK, (c + 1) * CHUNK)
        z = jax.nn.gelu(za_ref[0, rows, :])
        u = z[:, :A_WIDTH]
        v = z[:, A_WIDTH:]
        vn = _group_ln(v, mavg_ref[...], g_ref[...], b_ref[...])
        vstack = jnp.concatenate([vn * gm_ref[g:g + 1, :] for g in range(A_GROUPS)], axis=0).astype(BF16)
        vs = _dot(wc, vstack) + bias_ref[...]
        o_ref[0, rows, :] = (u * vs).astype(o_ref.dtype)


def _mixer_a(pn, wcat, bias, g, b, mavg, gm, tb=512):
    bsz, s, _ = pn.shape
    const = lambda bi, j: (0, 0)
    return pl.pallas_call(
        functools.partial(_mixer_a_kernel, n_chunks=tb // CHUNK),
        grid=(bsz, s // tb),
        in_specs=[
            pl.BlockSpec((1, tb, 2 * A_WIDTH), lambda bi, j: (bi, j, PN_ZA // (2 * A_WIDTH))),
            pl.BlockSpec(wcat.shape, const),
            pl.BlockSpec(bias.shape, const),
            pl.BlockSpec(g.shape, const),
            pl.BlockSpec(b.shape, const),
            pl.BlockSpec(mavg.shape, const),
            pl.BlockSpec(gm.shape, const),
        ],
        out_specs=pl.BlockSpec((1, tb, A_WIDTH), lambda bi, j: (bi, j, 0)),
        out_shape=jax.ShapeDtypeStruct((bsz, s, A_WIDTH), BF16),
        compiler_params=_params(("arbitrary", "arbitrary"), 32),
        name="mixer_a",
    )(pn, wcat, bias, g, b, mavg, gm)


def _retention_kernel(x_ref, cos_ref, sin_ref, dec_ref, zeta_ref, xi_ref, cd_ref, qm_ref, vm_ref, bm_ref,
                      mavg_ref, g_ref, b_ref, o_ref, state_ref, *, n_chunks):
    @pl.when(pl.program_id(1) == 0)
    def _():
        state_ref[...] = jnp.zeros_like(state_ref)

    half = B_WIDTH // 2
    for c in range(n_chunks):
        rows = slice(c * CHUNK, (c + 1) * CHUNK)
        q = x_ref[0, rows, 0:B_WIDTH]
        k = x_ref[0, rows, B_WIDTH:2 * B_WIDTH]
        v = x_ref[0, rows, 2 * B_WIDTH:3 * B_WIDTH]
        gate = x_ref[0, rows, 3 * B_WIDTH:4 * B_WIDTH]
        cos = cos_ref[rows, :]
        sin = sin_ref[rows, :]

        def rot(t):
            t1 = t[:, :half]
            t2 = t[:, half:]
            return jnp.concatenate([t1 * cos - t2 * sin, t1 * sin + t2 * cos], axis=1)

        qr = rot(q)
        kr = rot(k) * (HEAD_DIM ** -0.5)
        qs = jnp.concatenate([qr * qm_ref[h:h + 1, :] for h in range(B_HEADS)], axis=0).astype(BF16)
        s = _dot_nt(qs, kr.astype(BF16)) * dec_ref[...]
        scat = jnp.concatenate([s[h * CHUNK:(h + 1) * CHUNK, :] for h in range(B_HEADS)], axis=1).astype(BF16)
        vstack = jnp.concatenate([v * vm_ref[h:h + 1, :] for h in range(B_HEADS)], axis=0).astype(BF16)
        o_inner = _dot(scat, vstack)
        state = state_ref[...]
        o_cross = _dot(qr.astype(BF16), state.astype(BF16)) * xi_ref[...]
        kz = (kr * zeta_ref[...]).astype(BF16)
        kv = _dot_tn(kz, v.astype(BF16)) * bm_ref[...]
        state_ref[...] = state * cd_ref[...] + kv
        o = _group_ln(o_inner + o_cross, mavg_ref[...], g_ref[...], b_ref[...])
        o_ref[0, rows, :] = (jax.nn.silu(gate) * o).astype(o_ref.dtype)


def _retention(pn, cos, sin, consts, mavg, g, b, tb=512):
    bsz, s, _ = pn.shape
    const = lambda bi, j: (0, 0)
    return pl.pallas_call(
        functools.partial(_retention_kernel, n_chunks=tb // CHUNK),
        grid=(bsz, s // tb),
        in_specs=[
            pl.BlockSpec((1, tb, 4 * B_WIDTH), lambda bi, j: (bi, j, PN_RET // (4 * B_WIDTH))),
            pl.BlockSpec((tb, B_WIDTH // 2), lambda bi, j: (j, 0)),
            pl.BlockSpec((tb, B_WIDTH // 2), lambda bi, j: (j, 0)),
        ] + [pl.BlockSpec(a.shape, const) for a in consts] + [
            pl.BlockSpec(mavg.shape, const),
            pl.BlockSpec(g.shape, const),
            pl.BlockSpec(b.shape, const),
        ],
        out_specs=pl.BlockSpec((1, tb, B_WIDTH), lambda bi, j: (bi, j, 0)),
        out_shape=jax.ShapeDtypeStruct((bsz, s, B_WIDTH), BF16),
        scratch_shapes=[pltpu.VMEM((B_WIDTH, B_WIDTH), F32)],
        compiler_params=_params(("arbitrary", "arbitrary"), 32),
        name="retention",
    )(pn, cos, sin, *consts, mavg, g, b)


def _retention_consts():
    h_n, d, l_n = B_HEADS, HEAD_DIM, CHUNK
    log_gamma = jnp.log1p(-jnp.exp2(-5.0 - jnp.arange(h_n, dtype=F32)))
    idx = jnp.arange(l_n, dtype=F32)
    diff = idx[:, None] - idx[None, :]
    decay_in = jnp.where(diff >= 0, jnp.exp(log_gamma[:, None, None] * jnp.maximum(diff, 0.0)), 0.0)
    xi = jnp.exp(log_gamma[:, None] * (idx + 1.0))
    zeta = jnp.exp(log_gamma[:, None] * (l_n - 1.0 - idx))
    chunk_decay = jnp.exp(log_gamma * l_n)
    cols = np.arange(B_WIDTH)
    head_perm = (cols % (B_WIDTH // 2)) // (d // 2)
    head_std = cols // d
    dec = decay_in.reshape(h_n * l_n, l_n)
    zeta_t = zeta.T[:, head_perm]
    xi_t = xi.T[:, head_std]
    cd = chunk_decay[head_std][None, :]
    qm = jnp.asarray((head_perm[None, :] == np.arange(h_n)[:, None]).astype(np.float32))
    vm = jnp.asarray((head_std[None, :] == np.arange(h_n)[:, None]).astype(np.float32))
    bm = jnp.asarray((head_perm[:, None] == head_std[None, :]).astype(np.float32))
    return [dec, zeta_t, xi_t, cd, qm, vm, bm]


def _rotary_tables(s):
    half = HEAD_DIM // 2
    inv = jnp.power(ROPE_BASE, -jnp.arange(half, dtype=F32) / half)
    ang = jnp.arange(s).astype(F32)[:, None] * inv[None, :]
    return jnp.tile(jnp.cos(ang), (1, B_HEADS)), jnp.tile(jnp.sin(ang), (1, B_HEADS))


def _compress_kernel(xk_ref, xv_ref, posk_ref, posv_ref, w1k_ref, w1v_ref, w2k_ref, w2vt_ref, kc_ref, vct_ref,
                     *, nseg):
    half = CMP_BLOCK // 2
    acc = [jnp.zeros((nseg, C_KV_WIDTH), F32) for _ in range(4)]
    for l in range(half):
        xk = xk_ref[0, pl.ds(l, nseg, stride=CMP_STRIDE), :]
        xv = xv_ref[0, pl.ds(l, nseg, stride=CMP_STRIDE), :]
        acc[0] += _dot((xk + posk_ref[l:l + 1, :]).astype(BF16), w1k_ref[l])
        acc[1] += _dot((xk + posk_ref[half + l:half + l + 1, :]).astype(BF16), w1k_ref[half + l])
        acc[2] += _dot((xv + posv_ref[l:l + 1, :]).astype(BF16), w1v_ref[l])
        acc[3] += _dot((xv + posv_ref[half + l:half + l + 1, :]).astype(BF16), w1v_ref[half + l])
    hk = jax.nn.gelu(acc[0] + pltpu.roll(acc[1], nseg - 1, 0))
    hv = jax.nn.gelu(acc[2] + pltpu.roll(acc[3], nseg - 1, 0))
    kc_ref[0] = _dot(hk.astype(BF16), w2k_ref[...])
    vct_ref[0] = _dot_nt(w2vt_ref[...], hv.astype(BF16))


def _compress(kvc, posk, posv, w1k, w1v, w2k, w2v):
    bsz, s, width = kvc.shape
    nseg = s // CMP_STRIDE
    c2 = lambda bi: (0, 0)
    c3 = lambda bi: (0, 0, 0)
    return pl.pallas_call(
        functools.partial(_compress_kernel, nseg=nseg),
        grid=(bsz,),
        in_specs=[
            pl.BlockSpec((1, s, C_KV_WIDTH), lambda bi: (bi, 0, 0)),
            pl.BlockSpec((1, s, C_KV_WIDTH), lambda bi: (bi, 0, 1)),
            pl.BlockSpec(posk.shape, c2),
            pl.BlockSpec(posv.shape, c2),
            pl.BlockSpec(w1k.shape, c3),
            pl.BlockSpec(w1v.shape, c3),
            pl.BlockSpec(w2k.shape, c2),
            pl.BlockSpec(w2v.shape, c2),
        ],
        out_specs=[
            pl.BlockSpec((1, nseg, C_KV_WIDTH), lambda bi: (bi, 0, 0)),
            pl.BlockSpec((1, C_KV_WIDTH, nseg), lambda bi: (bi, 0, 0)),
        ],
        out_shape=[
            jax.ShapeDtypeStruct((bsz, nseg, C_KV_WIDTH), F32),
            jax.ShapeDtypeStruct((bsz, C_KV_WIDTH, nseg), F32),
        ],
        compiler_params=_params(("arbitrary",), 32),
        name="nsa_compress",
    )(kvc, kvc, posk, posv, w1k, w1v, w2k, w2v)


def _blockdiag2(w):
    z = jnp.zeros_like(w)
    return jnp.concatenate([jnp.concatenate([w, z], axis=-1), jnp.concatenate([z, w], axis=-1)], axis=-2)


def _overlap_t(nseg, ns):
    nc = nseg - 1
    c0 = np.arange(nc)[None, :] * CMP_STRIDE
    s0 = np.arange(ns)[:, None] * SEL_BLOCK
    ov = np.clip(np.minimum(c0 + CMP_BLOCK, s0 + SEL_BLOCK) - np.maximum(c0, s0), 0, None) / CMP_BLOCK
    out = np.zeros((ns, nseg), np.float32)
    out[:, :nc] = ov
    return jnp.asarray(out, dtype=BF16)


def _nsa_kernel(qt_ref, gt_ref, ks_ref, vst_ref, kw_ref, vwt_ref, kc_ref, vct_ref, ovt_ref, e_ref, o_ref,
                qsel_ref, qwin_ref, m_ref, alpha_ref, acc_ref, res_ref, sbuf_ref, pbuf_ref,
                cbuf_ref, pcmp_ref, *, nseg, ns, n_sel, n_chunks):
    i = pl.program_id(1)
    s0 = i * CHUNK
    width = C_HEADS * CHUNK
    blk_w = 2 * CHUNK
    n_blk = C_HEADS // 2
    blk_per_g = n_blk // C_KV_HEADS
    zero = jnp.zeros((HEAD_DIM, CHUNK), BF16)
    for h in range(C_HEADS):
        qh = qt_ref[0, h * HEAD_DIM:(h + 1) * HEAD_DIM, :]
        blk = jnp.concatenate([qh, zero] if h < C_REP else [zero, qh], axis=0)
        qsel_ref[0:C_KV_WIDTH, h * CHUNK:(h + 1) * CHUNK] = blk
        qwin_ref[0:C_KV_WIDTH, h * CHUNK:(h + 1) * CHUNK] = blk
    brow = lax.broadcasted_iota(jnp.int32, (C_KV_WIDTH, width), 0)
    bias_rows = jnp.where(brow == ns + 1, NEG, 0.0).astype(BF16)
    qsel_ref[C_KV_WIDTH:, :] = bias_rows
    qwin_ref[C_KV_WIDTH:, :] = bias_rows
    ones_tile = jnp.where(lax.broadcasted_iota(jnp.int32, (2 * SUBLANES, CHUNK), 0) == 0, 1.0, 0.0).astype(BF16)

    krow = lax.broadcasted_iota(jnp.int32, (CHUNK, blk_w), 0)
    t_loc = lax.broadcasted_iota(jnp.int32, (CHUNK, blk_w), 1) % CHUNK
    tri_diag = jnp.where(krow <= t_loc, 0.0, NEG)
    tri_old = jnp.where(krow > t_loc, 0.0, NEG)

    def cmp_scores():
        cbuf_ref[...] = _dot(kc_ref[0].astype(BF16), qwin_ref[0:C_KV_WIDTH, :])

    def cmp_softmax():
        crow = lax.broadcasted_iota(jnp.int32, (nseg, blk_w), 0)
        c_t = s0 + lax.broadcasted_iota(jnp.int32, (nseg, blk_w), 1) % CHUNK
        cbias = jnp.where(crow * CMP_STRIDE + (CMP_BLOCK - 1) <= c_t, 0.0, NEG)
        psum = [None] * C_KV_HEADS
        for b in range(n_blk):
            g = b // blk_per_g
            cols = slice(b * blk_w, (b + 1) * blk_w)
            sc = cbuf_ref[:, cols] + cbias
            mx = jnp.max(sc, axis=0, keepdims=True)
            e = jnp.exp2(sc - mx)
            den = jnp.sum(e, axis=0, keepdims=True)
            p = e * jnp.where(mx > 0.5 * NEG, 1.0 / den, 0.0)
            pcmp_ref[:, cols] = p.astype(BF16)
            both = p[:, :CHUNK] + p[:, CHUNK:]
            psum[g] = both if psum[g] is None else psum[g] + both
        return [_split_dot_left(ovt_ref[...], ps) for ps in psum]

    def cmp_output():
        vct = vct_ref[0].astype(BF16)
        for b in range(n_blk):
            g = b // blk_per_g
            cols = slice(b * blk_w, (b + 1) * blk_w)
            res_ref[0, :, cols] = _dot(vct[g * HEAD_DIM:(g + 1) * HEAD_DIM, :], pcmp_ref[:, cols])

    def select_blocks(imps):
        j = lax.broadcasted_iota(jnp.int32, (ns, CHUNK), 0)
        cur = (s0 + lax.broadcasted_iota(jnp.int32, (ns, CHUNK), 1)) // SEL_BLOCK
        forced = (j == 0) | (j == cur) | (j == cur - 1)
        future = j > cur
        slab_rows = lax.broadcasted_iota(jnp.int32, (SUBLANES, CHUNK), 0)
        for g in range(C_KV_HEADS):
            imp = jnp.where(forced, BIG, jnp.where(future, NEG, imps[g]))
            slabs = [imp[v * SUBLANES:(v + 1) * SUBLANES, :] for v in range(ns // SUBLANES)]
            ranks = [jnp.zeros((SUBLANES, CHUNK), F32) for _ in slabs]
            for i2 in range(ns):
                r_i = imp[i2:i2 + 1, :]
                for v, slab in enumerate(slabs):
                    if (v + 1) * SUBLANES - 1 <= i2:
                        ranks[v] = ranks[v] + jnp.where(r_i > slab, 1.0, 0.0)
                    elif v * SUBLANES > i2:
                        ranks[v] = ranks[v] + jnp.where(r_i >= slab, 1.0, 0.0)
                    else:
                        tie = jnp.where(slab_rows > i2 - v * SUBLANES, 1.0, 0.0)
                        ranks[v] = ranks[v] + jnp.where(r_i > slab, 1.0, 0.0) + jnp.where(r_i == slab, tie, 0.0)
            rank = jnp.concatenate(ranks, axis=0)
            sel_bias = jnp.where(rank < n_sel, 0.0, NEG).astype(BF16)
            for r in range(C_REP):
                h = g * C_REP + r
                qsel_ref[C_KV_WIDTH:C_KV_WIDTH + ns, h * CHUNK:(h + 1) * CHUNK] = sel_bias

    def reset():
        m_ref[...] = jnp.full(m_ref.shape, NEG, F32)
        acc_ref[...] = jnp.zeros(acc_ref.shape, F32)
        alpha_ref[...] = jnp.ones(alpha_ref.shape, F32)
        pbuf_ref[...] = jnp.zeros(pbuf_ref.shape, BF16)

    def scores(k_ref, q_ref, c, tile, slot):
        off = pl.multiple_of(c * CHUNK, CHUNK)
        keys = jnp.concatenate([k_ref[0, pl.ds(off, CHUNK), :], e_ref[tile]], axis=1)
        sbuf_ref[slot] = _dot(keys, q_ref[...])

    def softmax(slot, tri):
        for b in range(n_blk):
            cols = slice(b * blk_w, (b + 1) * blk_w)
            s = sbuf_ref[slot, :, cols]
            if tri is not None:
                s = s + tri
            m_old = m_ref[:, cols]
            m_new = jnp.maximum(m_old, jnp.max(s, axis=0, keepdims=True))
            alpha_ref[:, cols] = jnp.exp2(m_old - m_new)
            pbuf_ref[:, cols] = jnp.exp2(s - m_new).astype(BF16)
            m_ref[:, cols] = m_new

    def accumulate(vt_ref, c):
        off = pl.multiple_of(c * CHUNK, CHUNK)
        vt = vt_ref[0, :, pl.ds(off, CHUNK)]
        vone = [jnp.concatenate([vt[g * HEAD_DIM:(g + 1) * HEAD_DIM, :], ones_tile], axis=0)
                for g in range(C_KV_HEADS)]
        for b in range(n_blk):
            cols = slice(b * blk_w, (b + 1) * blk_w)
            acc_ref[:, cols] = alpha_ref[:, cols] * acc_ref[:, cols] + _dot(vone[b // blk_per_g], pbuf_ref[:, cols])

    def finish(slot):
        l_fin = acc_ref[HEAD_DIM:HEAD_DIM + 1, :]
        res_ref[slot] = acc_ref[0:HEAD_DIM, :] * jnp.where(l_fin > 0.0, 1.0 / l_fin, 0.0)

    n_back = WINDOW // CHUNK
    cmp_scores()
    reset()
    chunk = [jnp.maximum(i - n_back + w, 0) for w in range(n_back + 1)]
    tile = [jnp.where(i - n_back + w < 0, n_chunks + 1, n_chunks) for w in range(n_back + 1)]
    scores(kw_ref, qwin_ref, chunk[0], tile[0], 0)
    imps = None
    for w in range(n_back + 1):
        if w >= 1:
            accumulate(vwt_ref, chunk[w - 1])
        if w == 0:
            imps = cmp_softmax()
        softmax(w % 2, tri_old if w == 0 else (tri_diag if w == n_back else None))
        if w < n_back:
            scores(kw_ref, qwin_ref, chunk[w + 1], tile[w + 1], (w + 1) % 2)
        if w == 1:
            select_blocks(imps)
    accumulate(vwt_ref, chunk[n_back])
    cmp_output()
    finish(2)

    def sel_stage(k, slot, slot_next):
        accumulate(vst_ref, jnp.maximum(k - 1, 0))
        softmax(slot, None)
        scores(ks_ref, qsel_ref, k + 1, k + 1, slot_next)

    def sel_pair(kk, carry):
        sel_stage(2 * kk, 0, 1)
        sel_stage(2 * kk + 1, 1, 0)
        return carry

    reset()
    scores(ks_ref, qsel_ref, 0, 0, 0)
    lax.fori_loop(0, i // 2, sel_pair, 0)

    @pl.when(i % 2 == 1)
    def _():
        sel_stage(i - 1, 0, 1)

    accumulate(vst_ref, jnp.maximum(i - 1, 0))
    softmax(i % 2, tri_diag)
    accumulate(vst_ref, i)
    finish(1)

    gates = jax.nn.sigmoid(gt_ref[0])
    for pair in range(C_HEADS // 2):
        pieces = []
        for h in (2 * pair, 2 * pair + 1):
            cs = slice(h * CHUNK, (h + 1) * CHUNK)
            pieces.append(gates[3 * h:3 * h + 1, :] * res_ref[0, :, cs]
                          + gates[3 * h + 1:3 * h + 2, :] * res_ref[1, :, cs]
                          + gates[3 * h + 2:3 * h + 3, :] * res_ref[2, :, cs])
        o_ref[0, :, pair * 2 * HEAD_DIM:(pair + 1) * 2 * HEAD_DIM] = jnp.concatenate(pieces, axis=0).T.astype(o_ref.dtype)


def _route_tiles(n_chunks, ns):
    e = np.zeros((n_chunks + 2, CHUNK, C_KV_WIDTH), np.float32)
    r = np.arange(CHUNK)
    for c in range(n_chunks):
        e[c, r, 2 * c + r // SEL_BLOCK] = 1.0
    e[n_chunks, :, ns] = 1.0
    e[n_chunks + 1, :, ns + 1] = 1.0
    return jnp.asarray(e, dtype=BF16)


def _nsa(ptq, gt, ksw, kc, vct, ovt, route):
    bsz, s, _ = ksw.shape
    nseg = s // CMP_STRIDE
    ns = s // SEL_BLOCK
    n_chunks = s // CHUNK
    width = C_HEADS * CHUNK
    return pl.pallas_call(
        functools.partial(_nsa_kernel, nseg=nseg, ns=ns, n_sel=min(N_SEL, ns), n_chunks=n_chunks),
        grid=(bsz, n_chunks),
        in_specs=[
            pl.BlockSpec((1, C_WIDTH, CHUNK), lambda b, i: (b, PT_Q // C_WIDTH, i)),
            pl.BlockSpec((1, gt.shape[1], CHUNK), lambda b, i: (b, 0, i)),
            pl.BlockSpec((1, s, C_KV_WIDTH), lambda b, i: (b, 0, 0)),
            pl.BlockSpec((1, C_KV_WIDTH, s), lambda b, i: (b, PT_VS // C_KV_WIDTH, 0)),
            pl.BlockSpec((1, s, C_KV_WIDTH), lambda b, i: (b, 0, 1)),
            pl.BlockSpec((1, C_KV_WIDTH, s), lambda b, i: (b, PT_VW // C_KV_WIDTH, 0)),
            pl.BlockSpec((1, nseg, C_KV_WIDTH), lambda b, i: (b, 0, 0)),
            pl.BlockSpec((1, C_KV_WIDTH, nseg), lambda b, i: (b, 0, 0)),
            pl.BlockSpec(ovt.shape, lambda b, i: (0, 0)),
            pl.BlockSpec(route.shape, lambda b, i: (0, 0, 0)),
        ],
        out_specs=pl.BlockSpec((1, CHUNK, C_WIDTH), lambda b, i: (b, i, 0)),
        out_shape=jax.ShapeDtypeStruct((bsz, s, C_WIDTH), BF16),
        scratch_shapes=[
            pltpu.VMEM((2 * C_KV_WIDTH, width), BF16),
            pltpu.VMEM((2 * C_KV_WIDTH, width), BF16),
            pltpu.VMEM((1, width), F32),
            pltpu.VMEM((1, width), F32),
            pltpu.VMEM((HEAD_DIM + 2 * SUBLANES, width), F32),
            pltpu.VMEM((3, HEAD_DIM, width), F32),
            pltpu.VMEM((2, CHUNK, width), F32),
            pltpu.VMEM((CHUNK, width), BF16),
            pltpu.VMEM((nseg, width), F32),
            pltpu.VMEM((nseg, width), BF16),
        ],
        compiler_params=_params(("arbitrary", "arbitrary"), 40),
        name="nsa_attention",
    )(ptq, gt, ksw, ptq, ksw, ptq, kc, vct, ovt, route)


def _tail_kernel(ya_ref, yb_ref, yc_ref, x_ref, mod_ref, wo_ref, g1_ref, b1_ref,
                 wup_ref, cw_ref, cb_ref, wdn_ref, g2_ref, b2_ref, o_ref,
                 work_ref, carry_ref, x1_ref, h_ref, act_ref, *, d, dff, cwid, tm, alpha):
    @pl.when(pl.program_id(1) == 0)
    def _():
        carry_ref[...] = jnp.zeros_like(carry_ref)

    y1 = _dot(ya_ref[0], wo_ref[0:A_WIDTH, :])
    y1 += _dot(yb_ref[0], wo_ref[A_WIDTH:A_WIDTH + B_WIDTH, :])
    y1 += _dot(yc_ref[0], wo_ref[A_WIDTH + B_WIDTH:, :])
    x1 = _row_ln(alpha * x_ref[0] + mod_ref[0, :, 2 * d:3 * d] * y1, g1_ref[...], b1_ref[...])
    x1_ref[...] = x1

    sh = mod_ref[0, :, 3 * d:4 * d]
    sc = mod_ref[0, :, 4 * d:5 * d]
    gate = mod_ref[0, :, 5 * d:6 * d]
    h_ref[...] = (x1 * (1.0 + sc) + sh).astype(BF16)
    pad = SUBLANES
    n_chunks = dff // cwid

    def up(ci):
        slot = ci % 2
        for part in range(2):
            cs = slice(part * dff + ci * cwid, part * dff + (ci + 1) * cwid)
            work_ref[slot, part, 0:pad, :] = carry_ref[:, cs]
            work_ref[slot, part, pad:pad + tm, :] = _dot(h_ref[...], wup_ref[:, cs])

    def conv_act(ci):
        slot = ci % 2
        conv = []
        for part in range(2):
            cs = slice(part * dff + ci * cwid, part * dff + (ci + 1) * cwid)
            buf = work_ref[slot, part]
            carry_ref[:, cs] = buf[tm:tm + pad, :]
            a = buf[pad:, :]
            a1 = pltpu.roll(buf, 1, 0)[pad:, :]
            a2 = pltpu.roll(buf, 2, 0)[pad:, :]
            conv.append(cw_ref[0:1, cs] * a2 + cw_ref[1:2, cs] * a1 + cw_ref[2:3, cs] * a + cb_ref[:, cs])
        half = 0.5 * conv[0]
        silu = half + half * jnp.tanh(half)
        act_ref[:, ci * cwid:(ci + 1) * cwid] = (silu * conv[1]).astype(BF16)

    split = (n_chunks // 2) * cwid
    up(0)
    y = None
    for ci in range(n_chunks):
        if ci + 1 < n_chunks:
            up(ci + 1)
        conv_act(ci)
        if (ci + 1) * cwid == split:
            y = _dot(act_ref[:, :split], wdn_ref[:split, :])
    y = y + _dot(act_ref[:, split:], wdn_ref[split:, :])
    o_ref[0] = _row_ln(alpha * x1_ref[...] + gate * y, g2_ref[...], b2_ref[...])


def _tail(ya, yb, yc, x, mod, layer, wo, g1, b1, wup, cw, cb, wdn, g2, b2, alpha, tm=512, cwid=256):
    bsz, s, d = x.shape
    dff = wdn.shape[1]
    const = lambda bi, j: (0, 0)
    tile = lambda w: pl.BlockSpec((1, tm, w), lambda bi, j: (bi, j, 0))
    weight = lambda w: pl.BlockSpec((None,) + w.shape[1:], lambda bi, j: (layer, 0, 0),
                                    pipeline_mode=pl.Buffered(1))
    small = lambda a: pl.BlockSpec(a.shape, const)
    return pl.pallas_call(
        functools.partial(_tail_kernel, d=d, dff=dff, cwid=cwid, tm=tm, alpha=alpha),
        grid=(bsz, s // tm),
        in_specs=[
            tile(A_WIDTH), tile(B_WIDTH), tile(C_WIDTH), tile(d),
            pl.BlockSpec((1, 1, 6 * d), lambda bi, j: (bi, 0, 0)),
            weight(wo), small(g1), small(b1),
            weight(wup), small(cw), small(cb), weight(wdn), small(g2), small(b2),
        ],
        out_specs=tile(d),
        out_shape=jax.ShapeDtypeStruct((bsz, s, d), F32),
        scratch_shapes=[
            pltpu.VMEM((2, 2, tm + SUBLANES, cwid), F32),
            pltpu.VMEM((SUBLANES, 2 * dff), F32),
            pltpu.VMEM((tm, d), F32),
            pltpu.VMEM((tm, d), BF16),
            pltpu.VMEM((tm, dff), BF16),
        ],
        compiler_params=_params(("arbitrary", "arbitrary"), 52),
        name="layer_tail",
    )(ya, yb, yc, x, mod, wo, g1, b1, wup, cw, cb, wdn, g2, b2)


def _inproj_weights(w):
    cols = np.arange(B_WIDTH)
    half = cols // (B_WIDTH // 2)
    perm = ((cols % (B_WIDTH // 2)) // (HEAD_DIM // 2)) * HEAD_DIM + half * (HEAD_DIM // 2) + cols % (HEAD_DIM // 2)
    za = w[:, 0:512]
    qb, kb, vb, gb = (w[:, 512 + i * B_WIDTH:512 + (i + 1) * B_WIDTH] for i in range(4))
    qc = w[:, 1536:2048]
    kcm, vcm, ksl, vsl, kwn, vwn = (w[:, 2048 + i * C_KV_WIDTH:2048 + (i + 1) * C_KV_WIDTH] for i in range(6))
    gc = w[:, 2816:2840]
    wn = jnp.concatenate([qb[:, perm], kb[:, perm], vb, gb, za, ksl, kwn, kcm, vcm], axis=1)
    pad = jnp.zeros((w.shape[0], PT_ROWS - PT_G - gc.shape[1]), w.dtype)
    wt = jnp.concatenate([qc * (HEAD_DIM ** -0.5 * LOG2E), vsl, vwn, gc, pad], axis=1).T
    return wn.astype(BF16), wt.astype(BF16)


def kernel(x, c, w_ada, b_ada, w_in, a_ln_g, a_ln_b, a_ws, a_bs, b_gn_g, b_gn_b, c_pos_k, c_w1_k, c_w2_k,
           c_pos_v, c_w1_v, c_w2_v, w_out, ln1_g, ln1_b, w_up, conv_w, conv_b, w_down, ln2_g, ln2_b):
    depth = w_in.shape[0]
    bsz, s, d = x.shape
    alpha = (2 * depth) ** 0.25
    nseg = s // CMP_STRIDE
    ns = s // SEL_BLOCK

    lanes = np.arange(A_WIDTH)
    mavg = jnp.asarray((lanes[:, None] // HEAD_DIM == lanes[None, :] // HEAD_DIM) / HEAD_DIM, dtype=BF16)
    gm = jnp.asarray((lanes[None, :] // HEAD_DIM == np.arange(A_GROUPS)[:, None]).astype(np.float32))
    ret_consts = _retention_consts()
    cos, sin = _rotary_tables(s)
    ovt = _overlap_t(nseg, ns)
    route = _route_tiles(s // CHUNK, ns)

    wo_bf, wup_bf, wdn_bf = w_out.astype(BF16), w_up.astype(BF16), w_down.astype(BF16)
    mods = _ada_mod(c, w_ada, b_ada)
    for l in range(depth):
        mod = mods[l][:, None, :]
        wn, wt = _inproj_weights(w_in[l])
        pn, ksw, kvc, ptq, gt = _inproj(x, mod, wn, wt)

        wcat = jnp.transpose(a_ws[l], (1, 0, 2)).reshape(CHUNK, A_GROUPS * CHUNK)
        bias = jnp.repeat(a_bs[l].T, HEAD_DIM, axis=1)
        ya = _mixer_a(pn, wcat, bias, a_ln_g[l].reshape(1, A_WIDTH), a_ln_b[l].reshape(1, A_WIDTH), mavg, gm)

        yb = _retention(pn, cos, sin, ret_consts, mavg,
                        b_gn_g[l].reshape(1, B_WIDTH), b_gn_b[l].reshape(1, B_WIDTH))

        w1k = _blockdiag2(c_w1_k[l].reshape(CMP_BLOCK, HEAD_DIM, HEAD_DIM)).astype(BF16)
        w1v = _blockdiag2(c_w1_v[l].reshape(CMP_BLOCK, HEAD_DIM, HEAD_DIM)).astype(BF16)
        kc, vct = _compress(kvc, jnp.tile(c_pos_k[l], (1, C_KV_HEADS)), jnp.tile(c_pos_v[l], (1, C_KV_HEADS)),
                            w1k, w1v, _blockdiag2(c_w2_k[l]).astype(BF16), _blockdiag2(c_w2_v[l]).T.astype(BF16))
        yc = _nsa(ptq, gt, ksw, kc, vct, ovt, route)

        x = _tail(ya, yb, yc, x, mod, l, wo_bf, ln1_g[l].reshape(1, d), ln1_b[l].reshape(1, d),
                  wup_bf, conv_w[l], conv_b[l].reshape(1, -1), wdn_bf,
                  ln2_g[l].reshape(1, d), ln2_b[l].reshape(1, d), alpha)
    return x
```

```python
import functools

import numpy as np
import jax
import jax.numpy as jnp
from jax import lax
from jax.experimental import pallas as pl
from jax.experimental.pallas import tpu as pltpu

F32 = jnp.float32
BF16 = jnp.bfloat16

HEAD_DIM = 64
A_GROUPS = 4
A_WIDTH = A_GROUPS * HEAD_DIM
CHUNK = 128
B_HEADS = 4
B_WIDTH = B_HEADS * HEAD_DIM
ROPE_BASE = 10000.0
C_HEADS = 8
C_KV_HEADS = 2
C_REP = C_HEADS // C_KV_HEADS
C_WIDTH = C_HEADS * HEAD_DIM
C_KV_WIDTH = C_KV_HEADS * HEAD_DIM
CMP_BLOCK = 32
CMP_STRIDE = 16
SEL_BLOCK = 64
N_SEL = 8
WINDOW = 512
CONV_WIDTH = 3
LN_EPS = 1e-5
NEG = -1e30
BIG = 1e30
LOG2E = 1.4426950408889634

LANES = 128
SUBLANES = 8
VMEM_BYTES = 64 * 1024 * 1024

PN_RET, PN_ZA, PN_KS, PN_KW, PN_KVC = 0, 1024, 1536, 1664, 1792
PN_COLS = 1536
PT_Q, PT_VS, PT_VW, PT_G = 0, 512, 640, 768
PT_ROWS = 800


def _dot(a, b):
    return jnp.dot(a, b, preferred_element_type=F32)


def _dot_nt(a, b):
    return lax.dot_general(a, b, (((1,), (1,)), ((), ())), preferred_element_type=F32)


def _dot_tn(a, b):
    return lax.dot_general(a, b, (((0,), (0,)), ((), ())), preferred_element_type=F32)


def _split_dot(x, m):
    hi = x.astype(BF16)
    lo = (x - hi.astype(F32)).astype(BF16)
    return _dot(hi, m) + _dot(lo, m)


def _split_dot_left(m, x):
    hi = x.astype(BF16)
    lo = (x - hi.astype(F32)).astype(BF16)
    return _dot(m, hi) + _dot(m, lo)


def _group_ln(x, mavg, g, b):
    mu = _split_dot(x, mavg)
    d = x - mu
    var = _split_dot(d * d, mavg)
    return d * lax.rsqrt(var + LN_EPS) * g + b


def _row_ln(x, g, b):
    mu = jnp.mean(x, axis=-1, keepdims=True)
    d = x - mu
    var = jnp.mean(d * d, axis=-1, keepdims=True)
    return d * lax.rsqrt(var + LN_EPS) * g + b


def _params(sem, vmem_mb):
    return pltpu.CompilerParams(dimension_semantics=sem, vmem_limit_bytes=vmem_mb * 1024 * 1024)


def _ada_kernel(c_ref, w_ref, b_ref, o_ref):
    cond = jax.nn.silu(c_ref[...]).astype(BF16)
    o_ref[0] = _dot(cond, w_ref[0].astype(BF16)) + b_ref[0]


def _ada_mod(c, w_ada, b_ada):
    depth, d, n = w_ada.shape
    bsz = c.shape[0]
    tn = 1536
    return pl.pallas_call(
        _ada_kernel,
        grid=(depth, n // tn),
        in_specs=[
            pl.BlockSpec((bsz, d), lambda l, j: (0, 0)),
            pl.BlockSpec((1, d, tn), lambda l, j: (l, 0, j)),
            pl.BlockSpec((1, 1, tn), lambda l, j: (l, 0, j)),
        ],
        out_specs=pl.BlockSpec((1, bsz, tn), lambda l, j: (l, 0, j)),
        out_shape=jax.ShapeDtypeStruct((depth, bsz, n), F32),
        compiler_params=_params(("arbitrary", "arbitrary"), 40),
        name="ada_mod",
    )(c, w_ada, b_ada.reshape(depth, 1, n))


def _inproj_kernel(x_ref, mod_ref, wn_ref, wt_ref, pn_ref, ksw_ref, kvc_ref, ptq_ref, gt_ref, *, d):
    sh = mod_ref[0, :, 0:d]
    sc = mod_ref[0, :, d:2 * d]
    h = (x_ref[0] * (1.0 + sc) + sh).astype(BF16)
    pn = _dot(h, wn_ref[...])
    pn_ref[0] = pn[:, :PN_COLS]
    ksw_ref[0] = pn[:, PN_COLS:PN_KVC].astype(BF16)
    kvc_ref[0] = pn[:, PN_KVC:]
    pt = _dot_nt(wt_ref[...], h)
    ptq_ref[0] = pt[:PT_G, :].astype(BF16)
    gt_ref[0] = pt[PT_G:, :]


def _inproj(x, mod, wn, wt, tm=512):
    bsz, s, d = x.shape
    const = lambda b, j: (0, 0)
    return pl.pallas_call(
        functools.partial(_inproj_kernel, d=d),
        grid=(bsz, s // tm),
        in_specs=[
            pl.BlockSpec((1, tm, d), lambda b, j: (b, j, 0)),
            pl.BlockSpec((1, 1, 6 * d), lambda b, j: (b, 0, 0)),
            pl.BlockSpec(wn.shape, const),
            pl.BlockSpec(wt.shape, const),
        ],
        out_specs=[
            pl.BlockSpec((1, tm, PN_COLS), lambda b, j: (b, j, 0)),
            pl.BlockSpec((1, tm, 2 * C_KV_WIDTH), lambda b, j: (b, j, 0)),
            pl.BlockSpec((1, tm, 2 * C_KV_WIDTH), lambda b, j: (b, j, 0)),
            pl.BlockSpec((1, PT_G, tm), lambda b, j: (b, 0, j)),
            pl.BlockSpec((1, PT_ROWS - PT_G, tm), lambda b, j: (b, 0, j)),
        ],
        out_shape=[
            jax.ShapeDtypeStruct((bsz, s, PN_COLS), F32),
            jax.ShapeDtypeStruct((bsz, s, 2 * C_KV_WIDTH), BF16),
            jax.ShapeDtypeStruct((bsz, s, 2 * C_KV_WIDTH), F32),
            jax.ShapeDtypeStruct((bsz, PT_G, s), BF16),
            jax.ShapeDtypeStruct((bsz, PT_ROWS - PT_G, s), F32),
        ],
        compiler_params=_params(("arbitrary", "arbitrary"), 48),
        name="inproj",
    )(x, mod, wn, wt)


def _mixer_a_kernel(za_ref, w_ref, bias_ref, g_ref, b_ref, mavg_ref, gm_ref, o_ref, *, n_chunks):
    row = lax.broadcasted_iota(jnp.int32, (CHUNK, A_GROUPS * CHUNK), 0)
    col = lax.broadcasted_iota(jnp.int32, (CHUNK, A_GROUPS * CHUNK), 1)
    wc = jnp.where((col % CHUNK) <= row, w_ref[...], 0.0).astype(BF16)
    chunks = range(n_chunks)
    rows = [slice(c * CHUNK, (c + 1) * CHUNK) for c in chunks]
    z = [jax.nn.gelu(za_ref[0, r, :]) for r in rows]
    v = [zc[:, A_WIDTH:] for zc in z]
    mavg = mavg_ref[...]
    mu = [_split_dot(vc, mavg) for vc in v]
    dev = [vc - m for vc, m in zip(v, mu)]
    var = [_split_dot(d * d, mavg) for d in dev]
    vn = [d * lax.rsqrt(s2 + LN_EPS) * g_ref[...] + b_ref[...] for d, s2 in zip(dev, var)]
    vstack = [jnp.concatenate([x * gm_ref[g:g + 1, :] for g in range(A_GROUPS)], axis=0).astype(BF16) for x in vn]
    vs = [_dot(wc, x) + bias_ref[...] for x in vstack]
    for c in chunks:
        o_ref[0, rows[c], :] = (z[c][:, :A_WIDTH] * vs[c]).astype(o_ref.dtype)


def _mixer_a(pn, wcat, bias, g, b, mavg, gm, tb=512):
    bsz, s, _ = pn.shape
    const = lambda bi, j: (0, 0)
    return pl.pallas_call(
        functools.partial(_mixer_a_kernel, n_chunks=tb // CHUNK),
        grid=(bsz, s // tb),
        in_specs=[
            pl.BlockSpec((1, tb, 2 * A_WIDTH), lambda bi, j: (bi, j, PN_ZA // (2 * A_WIDTH))),
            pl.BlockSpec(wcat.shape, const),
            pl.BlockSpec(bias.shape, const),
            pl.BlockSpec(g.shape, const),
            pl.BlockSpec(b.shape, const),
            pl.BlockSpec(mavg.shape, const),
            pl.BlockSpec(gm.shape, const),
        ],
        out_specs=pl.BlockSpec((1, tb, A_WIDTH), lambda bi, j: (bi, j, 0)),
        out_shape=jax.ShapeDtypeStruct((bsz, s, A_WIDTH), BF16),
        compiler_params=_params(("arbitrary", "arbitrary"), 32),
        name="mixer_a",
    )(pn, wcat, bias, g, b, mavg, gm)


def _retention_kernel(x_ref, cos_ref, sin_ref, dec_ref, zeta_ref, xi_ref, cd_ref, qm_ref, vm_ref, bm_ref,
                      mavg_ref, g_ref, b_ref, o_ref, state_ref, *, n_chunks):
    @pl.when(pl.program_id(1) == 0)
    def _():
        state_ref[...] = jnp.zeros_like(state_ref)

    half = B_WIDTH // 2
    chunks = range(n_chunks)
    rows = [slice(c * CHUNK, (c + 1) * CHUNK) for c in chunks]

    def rot(t, r):
        t1 = t[:, :half]
        t2 = t[:, half:]
        cos = cos_ref[r, :]
        sin = sin_ref[r, :]
        return jnp.concatenate([t1 * cos - t2 * sin, t1 * sin + t2 * cos], axis=1)

    qr = [rot(x_ref[0, r, 0:B_WIDTH], r) for r in rows]
    kr = [rot(x_ref[0, r, B_WIDTH:2 * B_WIDTH], r) * (HEAD_DIM ** -0.5) for r in rows]
    v = [x_ref[0, r, 2 * B_WIDTH:3 * B_WIDTH] for r in rows]
    qs = [jnp.concatenate([x * qm_ref[h:h + 1, :] for h in range(B_HEADS)], axis=0).astype(BF16) for x in qr]
    s = [_dot_nt(a, b.astype(BF16)) * dec_ref[...] for a, b in zip(qs, kr)]
    kv = [_dot_tn((a * zeta_ref[...]).astype(BF16), b.astype(BF16)) * bm_ref[...]
          for a, b in zip(kr, v)]
    scat = [jnp.concatenate([x[h * CHUNK:(h + 1) * CHUNK, :] for h in range(B_HEADS)], axis=1).astype(BF16)
            for x in s]
    vstack = [jnp.concatenate([x * vm_ref[h:h + 1, :] for h in range(B_HEADS)], axis=0).astype(BF16) for x in v]
    o_inner = [_dot(a, b) for a, b in zip(scat, vstack)]
    state = state_ref[...]
    before = []
    for c in chunks:
        before.append(state)
        state = state * cd_ref[...] + kv[c]
    state_ref[...] = state
    o = [oi + _dot(a.astype(BF16), st.astype(BF16)) * xi_ref[...] for oi, a, st in zip(o_inner, qr, before)]
    mavg = mavg_ref[...]
    mu = [_split_dot(x, mavg) for x in o]
    dev = [x - m for x, m in zip(o, mu)]
    var = [_split_dot(d * d, mavg) for d in dev]
    for c in chunks:
        normed = dev[c] * lax.rsqrt(var[c] + LN_EPS) * g_ref[...] + b_ref[...]
        gate = x_ref[0, rows[c], 3 * B_WIDTH:4 * B_WIDTH]
        o_ref[0, rows[c], :] = (jax.nn.silu(gate) * normed).astype(o_ref.dtype)


def _retention(pn, cos, sin, consts, mavg, g, b, tb=512):
    bsz, s, _ = pn.shape
    const = lambda bi, j: (0, 0)
    return pl.pallas_call(
        functools.partial(_retention_kernel, n_chunks=tb // CHUNK),
        grid=(bsz, s // tb),
        in_specs=[
            pl.BlockSpec((1, tb, 4 * B_WIDTH), lambda bi, j: (bi, j, PN_RET // (4 * B_WIDTH))),
            pl.BlockSpec((tb, B_WIDTH // 2), lambda bi, j: (j, 0)),
            pl.BlockSpec((tb, B_WIDTH // 2), lambda bi, j: (j, 0)),
        ] + [pl.BlockSpec(a.shape, const) for a in consts] + [
            pl.BlockSpec(mavg.shape, const),
            pl.BlockSpec(g.shape, const),
            pl.BlockSpec(b.shape, const),
        ],
        out_specs=pl.BlockSpec((1, tb, B_WIDTH), lambda bi, j: (bi, j, 0)),
        out_shape=jax.ShapeDtypeStruct((bsz, s, B_WIDTH), BF16),
        scratch_shapes=[pltpu.VMEM((B_WIDTH, B_WIDTH), F32)],
        compiler_params=_params(("arbitrary", "arbitrary"), 32),
        name="retention",
    )(pn, cos, sin, *consts, mavg, g, b)


def _retention_consts():
    h_n, d, l_n = B_HEADS, HEAD_DIM, CHUNK
    log_gamma = jnp.log1p(-jnp.exp2(-5.0 - jnp.arange(h_n, dtype=F32)))
    idx = jnp.arange(l_n, dtype=F32)
    diff = idx[:, None] - idx[None, :]
    decay_in = jnp.where(diff >= 0, jnp.exp(log_gamma[:, None, None] * jnp.maximum(diff, 0.0)), 0.0)
    xi = jnp.exp(log_gamma[:, None] * (idx + 1.0))
    zeta = jnp.exp(log_gamma[:, None] * (l_n - 1.0 - idx))
    chunk_decay = jnp.exp(log_gamma * l_n)
    cols = np.arange(B_WIDTH)
    head_perm = (cols % (B_WIDTH // 2)) // (d // 2)
    head_std = cols // d
    dec = decay_in.reshape(h_n * l_n, l_n)
    zeta_t = zeta.T[:, head_perm]
    xi_t = xi.T[:, head_std]
    cd = chunk_decay[head_std][None, :]
    qm = jnp.asarray((head_perm[None, :] == np.arange(h_n)[:, None]).astype(np.float32))
    vm = jnp.asarray((head_std[None, :] == np.arange(h_n)[:, None]).astype(np.float32))
    bm = jnp.asarray((head_perm[:, None] == head_std[None, :]).astype(np.float32))
    return [dec, zeta_t, xi_t, cd, qm, vm, bm]


def _rotary_tables(s):
    half = HEAD_DIM // 2
    inv = jnp.power(ROPE_BASE, -jnp.arange(half, dtype=F32) / half)
    ang = jnp.arange(s).astype(F32)[:, None] * inv[None, :]
    return jnp.tile(jnp.cos(ang), (1, B_HEADS)), jnp.tile(jnp.sin(ang), (1, B_HEADS))


def _compress_kernel(xk_ref, xv_ref, posk_ref, posv_ref, w1k_ref, w1v_ref, w2k_ref, w2vt_ref, kc_ref, vct_ref,
                     *, nseg):
    half = CMP_BLOCK // 2
    acc = [jnp.zeros((nseg, C_KV_WIDTH), F32) for _ in range(4)]
    for l in range(half):
        xk = xk_ref[0, pl.ds(l, nseg, stride=CMP_STRIDE), :]
        xv = xv_ref[0, pl.ds(l, nseg, stride=CMP_STRIDE), :]
        acc[0] += _dot((xk + posk_ref[l:l + 1, :]).astype(BF16), w1k_ref[l])
        acc[1] += _dot((xk + posk_ref[half + l:half + l + 1, :]).astype(BF16), w1k_ref[half + l])
        acc[2] += _dot((xv + posv_ref[l:l + 1, :]).astype(BF16), w1v_ref[l])
        acc[3] += _dot((xv + posv_ref[half + l:half + l + 1, :]).astype(BF16), w1v_ref[half + l])
    hk = jax.nn.gelu(acc[0] + pltpu.roll(acc[1], nseg - 1, 0))
    hv = jax.nn.gelu(acc[2] + pltpu.roll(acc[3], nseg - 1, 0))
    kc_ref[0] = _dot(hk.astype(BF16), w2k_ref[...])
    vct_ref[0] = _dot_nt(w2vt_ref[...], hv.astype(BF16))


def _compress(kvc, posk, posv, w1k, w1v, w2k, w2v):
    bsz, s, width = kvc.shape
    nseg = s // CMP_STRIDE
    c2 = lambda bi: (0, 0)
    c3 = lambda bi: (0, 0, 0)
    return pl.pallas_call(
        functools.partial(_compress_kernel, nseg=nseg),
        grid=(bsz,),
        in_specs=[
            pl.BlockSpec((1, s, C_KV_WIDTH), lambda bi: (bi, 0, 0)),
            pl.BlockSpec((1, s, C_KV_WIDTH), lambda bi: (bi, 0, 1)),
            pl.BlockSpec(posk.shape, c2),
            pl.BlockSpec(posv.shape, c2),
            pl.BlockSpec(w1k.shape, c3),
            pl.BlockSpec(w1v.shape, c3),
            pl.BlockSpec(w2k.shape, c2),
            pl.BlockSpec(w2v.shape, c2),
        ],
        out_specs=[
            pl.BlockSpec((1, nseg, C_KV_WIDTH), lambda bi: (bi, 0, 0)),
            pl.BlockSpec((1, C_KV_WIDTH, nseg), lambda bi: (bi, 0, 0)),
        ],
        out_shape=[
            jax.ShapeDtypeStruct((bsz, nseg, C_KV_WIDTH), F32),
            jax.ShapeDtypeStruct((bsz, C_KV_WIDTH, nseg), F32),
        ],
        compiler_params=_params(("arbitrary",), 32),
        name="nsa_compress",
    )(kvc, kvc, posk, posv, w1k, w1v, w2k, w2v)


def _blockdiag2(w):
    z = jnp.zeros_like(w)
    return jnp.concatenate([jnp.concatenate([w, z], axis=-1), jnp.concatenate([z, w], axis=-1)], axis=-2)


def _overlap_t(nseg, ns):
    nc = nseg - 1
    c0 = np.arange(nc)[None, :] * CMP_STRIDE
    s0 = np.arange(ns)[:, None] * SEL_BLOCK
    ov = np.clip(np.minimum(c0 + CMP_BLOCK, s0 + SEL_BLOCK) - np.maximum(c0, s0), 0, None) / CMP_BLOCK
    out = np.zeros((ns, nseg), np.float32)
    out[:, :nc] = ov
    return jnp.asarray(out, dtype=BF16)


def _nsa_kernel(qt_ref, gt_ref, ks_ref, vst_ref, kw_ref, vwt_ref, kc_ref, vct_ref, ovt_ref, e_ref, o_ref,
                qb_ref, m_ref, alpha_ref, acc_ref, res_ref, sbuf_ref, pbuf_ref,
                cbuf_ref, pcmp_ref, *, nseg, ns, n_sel, n_chunks):
    i = pl.program_id(1)
    s0 = i * CHUNK
    width = C_HEADS * CHUNK
    blk_w = 2 * CHUNK
    n_blk = C_HEADS // 2
    blk_per_g = n_blk // C_KV_HEADS

    @pl.when((pl.program_id(0) == 0) & (i == 0))
    def _():
        brow = lax.broadcasted_iota(jnp.int32, (C_KV_WIDTH, width), 0)
        qb_ref[0:C_KV_WIDTH, :] = jnp.zeros((C_KV_WIDTH, width), BF16)
        qb_ref[2 * C_KV_WIDTH:, :] = jnp.where(brow == 1, NEG, 0.0).astype(BF16)

    zero = jnp.zeros((HEAD_DIM, CHUNK), BF16)
    for h in range(C_HEADS):
        qh = qt_ref[0, h * HEAD_DIM:(h + 1) * HEAD_DIM, :]
        blk = jnp.concatenate([qh, zero] if h < C_REP else [zero, qh], axis=0)
        qb_ref[C_KV_WIDTH:2 * C_KV_WIDTH, h * CHUNK:(h + 1) * CHUNK] = blk
    ones_tile = jnp.where(lax.broadcasted_iota(jnp.int32, (2 * SUBLANES, CHUNK), 0) == 0, 1.0, 0.0).astype(BF16)

    krow = lax.broadcasted_iota(jnp.int32, (CHUNK, blk_w), 0)
    t_loc = lax.broadcasted_iota(jnp.int32, (CHUNK, blk_w), 1) % CHUNK
    tri_diag = jnp.where(krow <= t_loc, 0.0, NEG)
    tri_old = jnp.where(krow > t_loc, 0.0, NEG)

    def cmp_scores():
        cbuf_ref[...] = _dot(kc_ref[0].astype(BF16), qb_ref[C_KV_WIDTH:2 * C_KV_WIDTH, :])

    def cmp_softmax():
        crow = lax.broadcasted_iota(jnp.int32, (nseg, blk_w), 0)
        c_t = s0 + lax.broadcasted_iota(jnp.int32, (nseg, blk_w), 1) % CHUNK
        cbias = jnp.where(crow * CMP_STRIDE + (CMP_BLOCK - 1) <= c_t, 0.0, NEG)
        psum = [None] * C_KV_HEADS
        for b in range(n_blk):
            g = b // blk_per_g
            cols = slice(b * blk_w, (b + 1) * blk_w)
            sc = cbuf_ref[:, cols] + cbias
            mx = jnp.max(sc, axis=0, keepdims=True)
            e = jnp.exp2(sc - mx)
            den = jnp.sum(e, axis=0, keepdims=True)
            p = e * jnp.where(mx > 0.5 * NEG, 1.0 / den, 0.0)
            pcmp_ref[:, cols] = p.astype(BF16)
            both = p[:, :CHUNK] + p[:, CHUNK:]
            psum[g] = both if psum[g] is None else psum[g] + both
        return [_split_dot_left(ovt_ref[...], ps) for ps in psum]

    def cmp_output():
        vct = vct_ref[0].astype(BF16)
        for b in range(n_blk):
            g = b // blk_per_g
            cols = slice(b * blk_w, (b + 1) * blk_w)
            res_ref[:, cols] = _dot(vct[g * HEAD_DIM:(g + 1) * HEAD_DIM, :], pcmp_ref[:, cols])

    def select_blocks(imps):
        j = lax.broadcasted_iota(jnp.int32, (ns, CHUNK), 0)
        cur = (s0 + lax.broadcasted_iota(jnp.int32, (ns, CHUNK), 1)) // SEL_BLOCK
        forced = (j == 0) | (j == cur) | (j == cur - 1)
        future = j > cur
        slab_rows = lax.broadcasted_iota(jnp.int32, (SUBLANES, CHUNK), 0)
        for g in range(C_KV_HEADS):
            imp = jnp.where(forced, BIG, jnp.where(future, NEG, imps[g]))
            slabs = [imp[v * SUBLANES:(v + 1) * SUBLANES, :] for v in range(ns // SUBLANES)]
            ranks = [jnp.zeros((SUBLANES, CHUNK), F32) for _ in slabs]
            for i2 in range(ns):
                r_i = imp[i2:i2 + 1, :]
                for v, slab in enumerate(slabs):
                    if (v + 1) * SUBLANES - 1 <= i2:
                        ranks[v] = ranks[v] + jnp.where(r_i > slab, 1.0, 0.0)
                    elif v * SUBLANES > i2:
                        ranks[v] = ranks[v] + jnp.where(r_i >= slab, 1.0, 0.0)
                    else:
                        tie = jnp.where(slab_rows > i2 - v * SUBLANES, 1.0, 0.0)
                        ranks[v] = ranks[v] + jnp.where(r_i > slab, 1.0, 0.0) + jnp.where(r_i == slab, tie, 0.0)
            rank = jnp.concatenate(ranks, axis=0)
            sel_bias = jnp.where(rank < n_sel, 0.0, NEG).astype(BF16)
            for r in range(C_REP):
                h = g * C_REP + r
                qb_ref[0:ns, h * CHUNK:(h + 1) * CHUNK] = sel_bias

    def scores(br, k_ref, c, tile, slot):
        off = pl.multiple_of(c * CHUNK, CHUNK)
        kch = k_ref[0, pl.ds(off, CHUNK), :]
        if br == 0:
            sbuf_ref[0, slot] = _dot(jnp.concatenate([kch, e_ref[tile]], axis=1),
                                     qb_ref[C_KV_WIDTH:3 * C_KV_WIDTH, :])
        else:
            sbuf_ref[1, slot] = _dot(jnp.concatenate([e_ref[tile], kch], axis=1), qb_ref[0:2 * C_KV_WIDTH, :])

    def softmax(br, slot, tri):
        for b in range(n_blk):
            cols = slice(b * blk_w, (b + 1) * blk_w)
            s = sbuf_ref[br, slot, :, cols]
            if tri is not None:
                s = s + tri
            m_old = m_ref[br, :, cols]
            m_new = jnp.maximum(m_old, jnp.max(s, axis=0, keepdims=True))
            alpha_ref[br, :, cols] = jnp.exp2(m_old - m_new)
            pbuf_ref[br, :, cols] = jnp.exp2(s - m_new).astype(BF16)
            m_ref[br, :, cols] = m_new

    def accumulate(br, vt_ref, c, first=False):
        off = pl.multiple_of(c * CHUNK, CHUNK)
        vt = vt_ref[0, :, pl.ds(off, CHUNK)]
        vone = [jnp.concatenate([vt[g * HEAD_DIM:(g + 1) * HEAD_DIM, :], ones_tile], axis=0)
                for g in range(C_KV_HEADS)]
        for b in range(n_blk):
            cols = slice(b * blk_w, (b + 1) * blk_w)
            pv = _dot(vone[b // blk_per_g], pbuf_ref[br, :, cols])
            acc_ref[br, :, cols] = pv if first else alpha_ref[br, :, cols] * acc_ref[br, :, cols] + pv

    n_back = WINDOW // CHUNK
    chunk =[jnp.maximum(i - n_back + w, 0) for w in range(n_back + 1)]
    tile = [jnp.where(i - n_back + w < 0, n_chunks + 1, n_chunks) for w in range(n_back + 1)]

    def win_stage(w):
        if w >= 1:
            accumulate(0, vwt_ref, chunk[w - 1], first=(w == 1))
        softmax(0, w % 2, tri_old if w == 0 else (tri_diag if w == n_back else None))
        if w < n_back:
            scores(0, kw_ref, chunk[w + 1], tile[w + 1], (w + 1) % 2)

    cmp_scores()
    m_ref[0] = jnp.full(m_ref.shape[1:], NEG, F32)
    scores(0, kw_ref, chunk[0], tile[0], 0)
    imps = cmp_softmax()
    win_stage(0)
    win_stage(1)
    select_blocks(imps)

    m_ref[1] = jnp.full(m_ref.shape[1:], NEG, F32)
    acc_ref[1] = jnp.zeros(acc_ref.shape[1:], F32)
    alpha_ref[1] = jnp.ones(alpha_ref.shape[1:], F32)
    pbuf_ref[1] = jnp.zeros(pbuf_ref.shape[1:], BF16)
    scores(1, ks_ref, 0, 0, 0)
    win_stage(2)

    def sel_stage(k, slot, slot_next):
        accumulate(1, vst_ref, jnp.maximum(k - 1, 0))
        softmax(1, slot, None)
        scores(1, ks_ref, k + 1, k + 1, slot_next)

    def sel_pair(kk, carry):
        sel_stage(2 * kk, 0, 1)
        sel_stage(2 * kk + 1, 1, 0)
        return carry

    lax.fori_loop(0, i // 2, sel_pair, 0)

    @pl.when(i % 2 == 1)
    def _():
        sel_stage(i - 1, 0, 1)

    accumulate(1, vst_ref, jnp.maximum(i - 1, 0))
    win_stage(3)
    softmax(1, i % 2, tri_diag)
    win_stage(4)
    accumulate(1, vst_ref, i)
    accumulate(0, vwt_ref, chunk[n_back])
    cmp_output()

    gates = jax.nn.sigmoid(gt_ref[0])
    den = [acc_ref[br, HEAD_DIM:HEAD_DIM + 1, :] for br in (0, 1)]
    inv = [jnp.where(x > 0.0, 1.0 / x, 0.0) for x in den]
    for pair in range(C_HEADS // 2):
        pieces = []
        for h in (2 * pair, 2 * pair + 1):
            cs = slice(h * CHUNK, (h + 1) * CHUNK)
            pieces.append(gates[3 * h:3 * h + 1, :] * res_ref[:, cs]
                          + (gates[3 * h + 1:3 * h + 2, :] * inv[1][:, cs]) * acc_ref[1, 0:HEAD_DIM, cs]
                          + (gates[3 * h + 2:3 * h + 3, :] * inv[0][:, cs]) * acc_ref[0, 0:HEAD_DIM, cs])
        o_ref[0, :, pair * 2 * HEAD_DIM:(pair + 1) * 2 * HEAD_DIM] = jnp.concatenate(pieces, axis=0).T.astype(o_ref.dtype)


def _route_tiles(n_chunks, ns):
    e = np.zeros((n_chunks + 2, CHUNK, C_KV_WIDTH), np.float32)
    r = np.arange(CHUNK)
    for c in range(n_chunks):
        e[c, r, 2 * c + r // SEL_BLOCK] = 1.0
    e[n_chunks, :, 0] = 1.0
    e[n_chunks + 1, :, 1] = 1.0
    return jnp.asarray(e, dtype=BF16)


def _nsa(ptq, gt, ksw, kc, vct, ovt, route):
    bsz, s, _ = ksw.shape
    nseg = s // CMP_STRIDE
    ns = s // SEL_BLOCK
    n_chunks = s // CHUNK
    width = C_HEADS * CHUNK
    return pl.pallas_call(
        functools.partial(_nsa_kernel, nseg=nseg, ns=ns, n_sel=min(N_SEL, ns), n_chunks=n_chunks),
        grid=(bsz, n_chunks),
        in_specs=[
            pl.BlockSpec((1, C_WIDTH, CHUNK), lambda b, i: (b, PT_Q // C_WIDTH, i)),
            pl.BlockSpec((1, gt.shape[1], CHUNK), lambda b, i: (b, 0, i)),
            pl.BlockSpec((1, s, C_KV_WIDTH), lambda b, i: (b, 0, 0)),
            pl.BlockSpec((1, C_KV_WIDTH, s), lambda b, i: (b, PT_VS // C_KV_WIDTH, 0)),
            pl.BlockSpec((1, s, C_KV_WIDTH), lambda b, i: (b, 0, 1)),
            pl.BlockSpec((1, C_KV_WIDTH, s), lambda b, i: (b, PT_VW // C_KV_WIDTH, 0)),
            pl.BlockSpec((1, nseg, C_KV_WIDTH), lambda b, i: (b, 0, 0)),
            pl.BlockSpec((1, C_KV_WIDTH, nseg), lambda b, i: (b, 0, 0)),
            pl.BlockSpec(ovt.shape, lambda b, i: (0, 0)),
            pl.BlockSpec(route.shape, lambda b, i: (0, 0, 0)),
        ],
        out_specs=pl.BlockSpec((1, CHUNK, C_WIDTH), lambda b, i: (b, i, 0)),
        out_shape=jax.ShapeDtypeStruct((bsz, s, C_WIDTH), BF16),
        scratch_shapes=[
            pltpu.VMEM((3 * C_KV_WIDTH, width), BF16),
            pltpu.VMEM((2, 1, width), F32),
            pltpu.VMEM((2, 1, width), F32),
            pltpu.VMEM((2, HEAD_DIM + 2 * SUBLANES, width), F32),
            pltpu.VMEM((HEAD_DIM, width), F32),
            pltpu.VMEM((2, 2, CHUNK, width), F32),
            pltpu.VMEM((2, CHUNK, width), BF16),
            pltpu.VMEM((nseg, width), F32),
            pltpu.VMEM((nseg, width), BF16),
        ],
        compiler_params=_params(("arbitrary", "arbitrary"), 40),
        name="nsa_attention",
    )(ptq, gt, ksw, ptq, ksw, ptq, kc, vct, ovt, route)


def _tail_kernel(ya_ref, yb_ref, yc_ref, x_ref, mod_ref, wo_ref, g1_ref, b1_ref,
                 wup_ref, cw_ref, cb_ref, wdn_ref, g2_ref, b2_ref, o_ref,
                 work_ref, carry_ref, x1_ref, h_ref, act_ref, *, d, dff, cwid, tm, alpha):
    @pl.when(pl.program_id(1) == 0)
    def _():
        carry_ref[...] = jnp.zeros_like(carry_ref)

    y1 = _dot(ya_ref[0], wo_ref[0:A_WIDTH, :])
    y1 += _dot(yb_ref[0], wo_ref[A_WIDTH:A_WIDTH + B_WIDTH, :])
    y1 += _dot(yc_ref[0], wo_ref[A_WIDTH + B_WIDTH:, :])
    x1 = _row_ln(alpha * x_ref[0] + mod_ref[0, :, 2 * d:3 * d] * y1, g1_ref[...], b1_ref[...])
    x1_ref[...] = x1

    sh = mod_ref[0, :, 3 * d:4 * d]
    sc = mod_ref[0, :, 4 * d:5 * d]
    gate = mod_ref[0, :, 5 * d:6 * d]
    h_ref[...] = (x1 * (1.0 + sc) + sh).astype(BF16)
    pad = SUBLANES
    n_chunks = dff // cwid

    def up(ci):
        slot = ci % 2
        for part in range(2):
            cs = slice(part * dff + ci * cwid, part * dff + (ci + 1) * cwid)
            work_ref[slot, part, 0:pad, :] = carry_ref[:, cs]
            work_ref[slot, part, pad:pad + tm, :] = _dot(h_ref[...], wup_ref[:, cs])

    def conv_act(ci):
        slot = ci % 2
        conv = []
        for part in range(2):
            cs = slice(part * dff + ci * cwid, part * dff + (ci + 1) * cwid)
            buf = work_ref[slot, part]
            carry_ref[:, cs] = buf[tm:tm + pad, :]
            a = buf[pad:, :]
            a1 = pltpu.roll(buf, 1, 0)[pad:, :]
            a2 = pltpu.roll(buf, 2, 0)[pad:, :]
            conv.append(cw_ref[0:1, cs] * a2 + cw_ref[1:2, cs] * a1 + cw_ref[2:3, cs] * a + cb_ref[:, cs])
        half = 0.5 * conv[0]
        silu = half + half * jnp.tanh(half)
        act_ref[:, ci * cwid:(ci + 1) * cwid] = (silu * conv[1]).astype(BF16)

    split = (n_chunks // 2) * cwid
    up(0)
    y = None
    for ci in range(n_chunks):
        if ci + 1 < n_chunks:
            up(ci + 1)
        conv_act(ci)
        if (ci + 1) * cwid == split:
            y = _dot(act_ref[:, :split], wdn_ref[:split, :])
    y = y + _dot(act_ref[:, split:], wdn_ref[split:, :])
    o_ref[0] = _row_ln(alpha * x1_ref[...] + gate * y, g2_ref[...], b2_ref[...])


def _tail(ya, yb, yc, x, mod, layer, wo, g1, b1, wup, cw, cb, wdn, g2, b2, alpha, tm=512, cwid=256):
    bsz, s, d = x.shape
    dff = wdn.shape[1]
    const = lambda bi, j: (0, 0)
    tile = lambda w: pl.BlockSpec((1, tm, w), lambda bi, j: (bi, j, 0))
    weight = lambda w: pl.BlockSpec((None,) + w.shape[1:], lambda bi, j: (layer, 0, 0),
                                    pipeline_mode=pl.Buffered(1))
    small = lambda a: pl.BlockSpec(a.shape, const)
    return pl.pallas_call(
        functools.partial(_tail_kernel, d=d, dff=dff, cwid=cwid, tm=tm, alpha=alpha),
        grid=(bsz, s // tm),
        in_specs=[
            tile(A_WIDTH), tile(B_WIDTH), tile(C_WIDTH), tile(d),
            pl.BlockSpec((1, 1, 6 * d), lambda bi, j: (bi, 0, 0)),
            weight(wo), small(g1), small(b1),
            weight(wup), small(cw), small(cb), weight(wdn), small(g2), small(b2),
        ],
        out_specs=tile(d),
        out_shape=jax.ShapeDtypeStruct((bsz, s, d), F32),
        scratch_shapes=[
            pltpu.VMEM((2, 2, tm + SUBLANES, cwid), F32),
            pltpu.VMEM((SUBLANES, 2 * dff), F32),
            pltpu.VMEM((tm, d), F32),
            pltpu.VMEM((tm, d), BF16),
            pltpu.VMEM((tm, dff), BF16),
        ],
        compiler_params=_params(("arbitrary", "arbitrary"), 52),
        name="layer_tail",
    )(ya, yb, yc, x, mod, wo, g1, b1, wup, cw, cb, wdn, g2, b2)


def _inproj_weights(w):
    cols = np.arange(B_WIDTH)
    half = cols // (B_WIDTH // 2)
    perm = ((cols % (B_WIDTH // 2)) // (HEAD_DIM // 2)) * HEAD_DIM + half * (HEAD_DIM // 2) + cols % (HEAD_DIM // 2)
    za = w[:, 0:512]
    qb, kb, vb, gb = (w[:, 512 + i * B_WIDTH:512 + (i + 1) * B_WIDTH] for i in range(4))
    qc = w[:, 1536:2048]
    kcm, vcm, ksl, vsl, kwn, vwn = (w[:, 2048 + i * C_KV_WIDTH:2048 + (i + 1) * C_KV_WIDTH] for i in range(6))
    gc = w[:, 2816:2840]
    wn = jnp.concatenate([qb[:, perm], kb[:, perm], vb, gb, za, ksl, kwn, kcm, vcm], axis=1)
    pad = jnp.zeros((w.shape[0], PT_ROWS - PT_G - gc.shape[1]), w.dtype)
    wt = jnp.concatenate([qc * (HEAD_DIM ** -0.5 * LOG2E), vsl, vwn, gc, pad], axis=1).T
    return wn.astype(BF16), wt.astype(BF16)


def kernel(x, c, w_ada, b_ada, w_in, a_ln_g, a_ln_b, a_ws, a_bs, b_gn_g, b_gn_b, c_pos_k, c_w1_k, c_w2_k,
           c_pos_v, c_w1_v, c_w2_v, w_out, ln1_g, ln1_b, w_up, conv_w, conv_b, w_down, ln2_g, ln2_b):
    depth = w_in.shape[0]
    bsz, s, d = x.shape
    alpha = (2 * depth) ** 0.25
    nseg = s // CMP_STRIDE
    ns = s // SEL_BLOCK

    lanes = np.arange(A_WIDTH)
    mavg = jnp.asarray((lanes[:, None] // HEAD_DIM == lanes[None, :] // HEAD_DIM) / HEAD_DIM, dtype=BF16)
    gm = jnp.asarray((lanes[None, :] // HEAD_DIM == np.arange(A_GROUPS)[:, None]).astype(np.float32))
    ret_consts = _retention_consts()
    cos, sin = _rotary_tables(s)
    ovt = _overlap_t(nseg, ns)
    route = _route_tiles(s // CHUNK, ns)

    wo_bf, wup_bf, wdn_bf = w_out.astype(BF16), w_up.astype(BF16), w_down.astype(BF16)
    mods = _ada_mod(c, w_ada, b_ada)
    for l in range(depth):
        mod = mods[l][:, None, :]
        wn, wt = _inproj_weights(w_in[l])
        pn, ksw, kvc, ptq, gt = _inproj(x, mod, wn, wt)

        wcat = jnp.transpose(a_ws[l], (1, 0, 2)).reshape(CHUNK, A_GROUPS * CHUNK)
        bias = jnp.repeat(a_bs[l].T, HEAD_DIM, axis=1)
        ya = _mixer_a(pn, wcat, bias, a_ln_g[l].reshape(1, A_WIDTH), a_ln_b[l].reshape(1, A_WIDTH), mavg, gm)

        yb = _retention(pn, cos, sin, ret_consts, mavg,
                        b_gn_g[l].reshape(1, B_WIDTH), b_gn_b[l].reshape(1, B_WIDTH))

        w1k = _blockdiag2(c_w1_k[l].reshape(CMP_BLOCK, HEAD_DIM, HEAD_DIM)).astype(BF16)
        w1v = _blockdiag2(c_w1_v[l].reshape(CMP_BLOCK, HEAD_DIM, HEAD_DIM)).astype(BF16)
        kc, vct = _compress(kvc, jnp.tile(c_pos_k[l], (1, C_KV_HEADS)), jnp.tile(c_pos_v[l], (1, C_KV_HEADS)),
                            w1k, w1v, _blockdiag2(c_w2_k[l]).astype(BF16), _blockdiag2(c_w2_v[l]).T.astype(BF16))
        yc = _nsa(ptq, gt, ksw, kc, vct, ovt, route)

        x = _tail(ya, yb, yc, x, mod, l, wo_bf, ln1_g[l].reshape(1, d), ln1_b[l].reshape(1, d),
                  wup_bf, conv_w[l], conv_b[l].reshape(1, -1), wdn_bf,
                  ln2_g[l].reshape(1, d), ln2_b[l].reshape(1, d), alpha)
    return x
```

```python
import functools

import numpy as np
import jax
import jax.numpy as jnp
from jax import lax
from jax.experimental import pallas as pl
from jax.experimental.pallas import tpu as pltpu

F32 = jnp.float32
BF16 = jnp.bfloat16

HEAD_DIM = 64
A_GROUPS = 4
A_WIDTH = A_GROUPS * HEAD_DIM
CHUNK = 128
B_HEADS = 4
B_WIDTH = B_HEADS * HEAD_DIM
ROPE_BASE = 10000.0
C_HEADS = 8
C_KV_HEADS = 2
C_REP = C_HEADS // C_KV_HEADS
C_WIDTH = C_HEADS * HEAD_DIM
C_KV_WIDTH = C_KV_HEADS * HEAD_DIM
CMP_BLOCK = 32
CMP_STRIDE = 16
SEL_BLOCK = 64
N_SEL = 8
WINDOW = 512
CONV_WIDTH = 3
LN_EPS = 1e-5
NEG = -1e30
BIG = 1e30
LOG2E = 1.4426950408889634

LANES = 128
SUBLANES = 8
VMEM_BYTES = 64 * 1024 * 1024

PN_RET, PN_ZA, PN_KS, PN_KW, PN_KVC = 0, 1024, 1536, 1664, 1792
PN_COLS = 1536
PT_Q, PT_VS, PT_VW, PT_G = 0, 512, 640, 768
PT_ROWS = 800


def _dot(a, b):
    return jnp.dot(a, b, preferred_element_type=F32)


def _dot_nt(a, b):
    return lax.dot_general(a, b, (((1,), (1,)), ((), ())), preferred_element_type=F32)


def _dot_tn(a, b):
    return lax.dot_general(a, b, (((0,), (0,)), ((), ())), preferred_element_type=F32)


def _split_dot(x, m):
    hi = x.astype(BF16)
    lo = (x - hi.astype(F32)).astype(BF16)
    return _dot(hi, m) + _dot(lo, m)


def _split_dot_left(m, x):
    hi = x.astype(BF16)
    lo = (x - hi.astype(F32)).astype(BF16)
    return _dot(m, hi) + _dot(m, lo)


def _group_ln(x, mavg, g, b):
    mu = _split_dot(x, mavg)
    d = x - mu
    var = _split_dot(d * d, mavg)
    return d * lax.rsqrt(var + LN_EPS) * g + b


def _row_ln(x, g, b):
    mu = jnp.mean(x, axis=-1, keepdims=True)
    d = x - mu
    var = jnp.mean(d * d, axis=-1, keepdims=True)
    return d * lax.rsqrt(var + LN_EPS) * g + b


def _params(sem, vmem_mb):
    return pltpu.CompilerParams(dimension_semantics=sem, vmem_limit_bytes=vmem_mb * 1024 * 1024)


def _ada_kernel(c_ref, w_ref, b_ref, o_ref):
    cond = jax.nn.silu(c_ref[...]).astype(BF16)
    o_ref[0] = _dot(cond, w_ref[0].astype(BF16)) + b_ref[0]


def _ada_mod(c, w_ada, b_ada):
    depth, d, n = w_ada.shape
    bsz = c.shape[0]
    tn = 1536
    return pl.pallas_call(
        _ada_kernel,
        grid=(depth, n // tn),
        in_specs=[
            pl.BlockSpec((bsz, d), lambda l, j: (0, 0)),
            pl.BlockSpec((1, d, tn), lambda l, j: (l, 0, j)),
            pl.BlockSpec((1, 1, tn), lambda l, j: (l, 0, j)),
        ],
        out_specs=pl.BlockSpec((1, bsz, tn), lambda l, j: (l, 0, j)),
        out_shape=jax.ShapeDtypeStruct((depth, bsz, n), F32),
        compiler_params=_params(("arbitrary", "arbitrary"), 40),
        name="ada_mod",
    )(c, w_ada, b_ada.reshape(depth, 1, n))


def _inproj_kernel(x_ref, mod_ref, wn_ref, wt_ref, pn_ref, ksw_ref, kvc_ref, ptq_ref, gt_ref, *, d):
    sh = mod_ref[0, :, 0:d]
    sc = mod_ref[0, :, d:2 * d]
    h = (x_ref[0] * (1.0 + sc) + sh).astype(BF16)
    pn = _dot(h, wn_ref[...])
    pn_ref[0] = pn[:, :PN_COLS]
    ksw_ref[0] = pn[:, PN_COLS:PN_KVC].astype(BF16)
    kvc_ref[0] = pn[:, PN_KVC:]
    pt = _dot_nt(wt_ref[...], h)
    ptq_ref[0] = pt[:PT_G, :].astype(BF16)
    gt_ref[0] = pt[PT_G:, :]


def _inproj(x, mod, wn, wt, tm=512):
    bsz, s, d = x.shape
    const = lambda b, j: (0, 0)
    return pl.pallas_call(
        functools.partial(_inproj_kernel, d=d),
        grid=(bsz, s // tm),
        in_specs=[
            pl.BlockSpec((1, tm, d), lambda b, j: (b, j, 0)),
            pl.BlockSpec((1, 1, 6 * d), lambda b, j: (b, 0, 0)),
            pl.BlockSpec(wn.shape, const),
            pl.BlockSpec(wt.shape, const),
        ],
        out_specs=[
            pl.BlockSpec((1, tm, PN_COLS), lambda b, j: (b, j, 0)),
            pl.BlockSpec((1, tm, 2 * C_KV_WIDTH), lambda b, j: (b, j, 0)),
            pl.BlockSpec((1, tm, 2 * C_KV_WIDTH), lambda b, j: (b, j, 0)),
            pl.BlockSpec((1, PT_G, tm), lambda b, j: (b, 0, j)),
            pl.BlockSpec((1, PT_ROWS - PT_G, tm), lambda b, j: (b, 0, j)),
        ],
        out_shape=[
            jax.ShapeDtypeStruct((bsz, s, PN_COLS), F32),
            jax.ShapeDtypeStruct((bsz, s, 2 * C_KV_WIDTH), BF16),
            jax.ShapeDtypeStruct((bsz, s, 2 * C_KV_WIDTH), F32),
            jax.ShapeDtypeStruct((bsz, PT_G, s), BF16),
            jax.ShapeDtypeStruct((bsz, PT_ROWS - PT_G, s), F32),
        ],
        compiler_params=_params(("arbitrary", "arbitrary"), 48),
        name="inproj",
    )(x, mod, wn, wt)


def _mixer_a_kernel(za_ref, w_ref, bias_ref, g_ref, b_ref, mavg_ref, gm_ref, o_ref, *, n_chunks):
    row = lax.broadcasted_iota(jnp.int32, (CHUNK, A_GROUPS * CHUNK), 0)
    col = lax.broadcasted_iota(jnp.int32, (CHUNK, A_GROUPS * CHUNK), 1)
    wc = jnp.where((col % CHUNK) <= row, w_ref[...], 0.0).astype(BF16)
    chunks = range(n_chunks)
    rows = [slice(c * CHUNK, (c + 1) * CHUNK) for c in chunks]
    z = [jax.nn.gelu(za_ref[0, r, :]) for r in rows]
    v = [zc[:, A_WIDTH:] for zc in z]
    mavg = mavg_ref[...]
    mu = [_split_dot(vc, mavg) for vc in v]
    dev = [vc - m for vc, m in zip(v, mu)]
    var = [_split_dot(d * d, mavg) for d in dev]
    vn = [d * lax.rsqrt(s2 + LN_EPS) * g_ref[...] + b_ref[...] for d, s2 in zip(dev, var)]
    vstack = [jnp.concatenate([x * gm_ref[g:g + 1, :] for g in range(A_GROUPS)], axis=0).astype(BF16) for x in vn]
    vs = [_dot(wc, x) + bias_ref[...] for x in vstack]
    for c in chunks:
        o_ref[0, rows[c], :] = (z[c][:, :A_WIDTH] * vs[c]).astype(o_ref.dtype)


def _mixer_a(pn, wcat, bias, g, b, mavg, gm, tb=512):
    bsz, s, _ = pn.shape
    const = lambda bi, j: (0, 0)
    return pl.pallas_call(
        functools.partial(_mixer_a_kernel, n_chunks=tb // CHUNK),
        grid=(bsz, s // tb),
        in_specs=[
            pl.BlockSpec((1, tb, 2 * A_WIDTH), lambda bi, j: (bi, j, PN_ZA // (2 * A_WIDTH))),
            pl.BlockSpec(wcat.shape, const),
            pl.BlockSpec(bias.shape, const),
            pl.BlockSpec(g.shape, const),
            pl.BlockSpec(b.shape, const),
            pl.BlockSpec(mavg.shape, const),
            pl.BlockSpec(gm.shape, const),
        ],
        out_specs=pl.BlockSpec((1, tb, A_WIDTH), lambda bi, j: (bi, j, 0)),
        out_shape=jax.ShapeDtypeStruct((bsz, s, A_WIDTH), BF16),
        compiler_params=_params(("arbitrary", "arbitrary"), 32),
        name="mixer_a",
    )(pn, wcat, bias, g, b, mavg, gm)


def _retention_kernel(x_ref, cos_ref, sin_ref, dec_ref, zeta_ref, xi_ref, cd_ref, qm_ref, vm_ref, bm_ref,
                      mavg_ref, g_ref, b_ref, o_ref, state_ref, *, n_chunks):
    @pl.when(pl.program_id(1) == 0)
    def _():
        state_ref[...] = jnp.zeros_like(state_ref)

    half = B_WIDTH // 2
    chunks = range(n_chunks)
    rows = [slice(c * CHUNK, (c + 1) * CHUNK) for c in chunks]

    def rot(t, r):
        t1 = t[:, :half]
        t2 = t[:, half:]
        cos = cos_ref[r, :]
        sin = sin_ref[r, :]
        return jnp.concatenate([t1 * cos - t2 * sin, t1 * sin + t2 * cos], axis=1)

    qr = [rot(x_ref[0, r, 0:B_WIDTH], r) for r in rows]
    kr = [rot(x_ref[0, r, B_WIDTH:2 * B_WIDTH], r) * (HEAD_DIM ** -0.5) for r in rows]
    v = [x_ref[0, r, 2 * B_WIDTH:3 * B_WIDTH] for r in rows]
    qs = [jnp.concatenate([x * qm_ref[h:h + 1, :] for h in range(B_HEADS)], axis=0).astype(BF16) for x in qr]
    s = [_dot_nt(a, b.astype(BF16)) * dec_ref[...] for a, b in zip(qs, kr)]
    kv = [_dot_tn((a * zeta_ref[...]).astype(BF16), b.astype(BF16)) * bm_ref[...]
          for a, b in zip(kr, v)]
    scat = [jnp.concatenate([x[h * CHUNK:(h + 1) * CHUNK, :] for h in range(B_HEADS)], axis=1).astype(BF16)
            for x in s]
    vstack = [jnp.concatenate([x * vm_ref[h:h + 1, :] for h in range(B_HEADS)], axis=0).astype(BF16) for x in v]
    o_inner = [_dot(a, b) for a, b in zip(scat, vstack)]
    state = state_ref[...]
    before = []
    for c in chunks:
        before.append(state)
        state = state * cd_ref[...] + kv[c]
    state_ref[...] = state
    o = [oi + _dot(a.astype(BF16), st.astype(BF16)) * xi_ref[...] for oi, a, st in zip(o_inner, qr, before)]
    mavg = mavg_ref[...]
    mu = [_split_dot(x, mavg) for x in o]
    dev = [x - m for x, m in zip(o, mu)]
    var = [_split_dot(d * d, mavg) for d in dev]
    for c in chunks:
        normed = dev[c] * lax.rsqrt(var[c] + LN_EPS) * g_ref[...] + b_ref[...]
        gate = x_ref[0, rows[c], 3 * B_WIDTH:4 * B_WIDTH]
        o_ref[0, rows[c], :] = (jax.nn.silu(gate) * normed).astype(o_ref.dtype)


def _retention(pn, cos, sin, consts, mavg, g, b, tb=512):
    bsz, s, _ = pn.shape
    const = lambda bi, j: (0, 0)
    return pl.pallas_call(
        functools.partial(_retention_kernel, n_chunks=tb // CHUNK),
        grid=(bsz, s // tb),
        in_specs=[
            pl.BlockSpec((1, tb, 4 * B_WIDTH), lambda bi, j: (bi, j, PN_RET // (4 * B_WIDTH))),
            pl.BlockSpec((tb, B_WIDTH // 2), lambda bi, j: (j, 0)),
            pl.BlockSpec((tb, B_WIDTH // 2), lambda bi, j: (j, 0)),
        ] + [pl.BlockSpec(a.shape, const) for a in consts] + [
            pl.BlockSpec(mavg.shape, const),
            pl.BlockSpec(g.shape, const),
            pl.BlockSpec(b.shape, const),
        ],
        out_specs=pl.BlockSpec((1, tb, B_WIDTH), lambda bi, j: (bi, j, 0)),
        out_shape=jax.ShapeDtypeStruct((bsz, s, B_WIDTH), BF16),
        scratch_shapes=[pltpu.VMEM((B_WIDTH, B_WIDTH), F32)],
        compiler_params=_params(("arbitrary", "arbitrary"), 32),
        name="retention",
    )(pn, cos, sin, *consts, mavg, g, b)


def _retention_consts():
    h_n, d, l_n = B_HEADS, HEAD_DIM, CHUNK
    log_gamma = jnp.log1p(-jnp.exp2(-5.0 - jnp.arange(h_n, dtype=F32)))
    idx = jnp.arange(l_n, dtype=F32)
    diff = idx[:, None] - idx[None, :]
    decay_in = jnp.where(diff >= 0, jnp.exp(log_gamma[:, None, None] * jnp.maximum(diff, 0.0)), 0.0)
    xi = jnp.exp(log_gamma[:, None] * (idx + 1.0))
    zeta = jnp.exp(log_gamma[:, None] * (l_n - 1.0 - idx))
    chunk_decay = jnp.exp(log_gamma * l_n)
    cols = np.arange(B_WIDTH)
    head_perm = (cols % (B_WIDTH // 2)) // (d // 2)
    head_std = cols // d
    dec = decay_in.reshape(h_n * l_n, l_n)
    zeta_t = zeta.T[:, head_perm]
    xi_t = xi.T[:, head_std]
    cd = chunk_decay[head_std][None, :]
    qm = jnp.asarray((head_perm[None, :] == np.arange(h_n)[:, None]).astype(np.float32))
    vm = jnp.asarray((head_std[None, :] == np.arange(h_n)[:, None]).astype(np.float32))
    bm = jnp.asarray((head_perm[:, None] == head_std[None, :]).astype(np.float32))
    return [dec, zeta_t, xi_t, cd, qm, vm, bm]


def _rotary_tables(s):
    half = HEAD_DIM // 2
    inv = jnp.power(ROPE_BASE, -jnp.arange(half, dtype=F32) / half)
    ang = jnp.arange(s).astype(F32)[:, None] * inv[None, :]
    return jnp.tile(jnp.cos(ang), (1, B_HEADS)), jnp.tile(jnp.sin(ang), (1, B_HEADS))


def _compress_kernel(xk_ref, xv_ref, posk_ref, posv_ref, w1k_ref, w1v_ref, w2k_ref, w2vt_ref, kc_ref, vct_ref,
                     *, nseg):
    half = CMP_BLOCK // 2
    acc = [jnp.zeros((nseg, C_KV_WIDTH), F32) for _ in range(4)]
    for l in range(half):
        xk = xk_ref[0, pl.ds(l, nseg, stride=CMP_STRIDE), :]
        xv = xv_ref[0, pl.ds(l, nseg, stride=CMP_STRIDE), :]
        acc[0] += _dot((xk + posk_ref[l:l + 1, :]).astype(BF16), w1k_ref[l])
        acc[1] += _dot((xk + posk_ref[half + l:half + l + 1, :]).astype(BF16), w1k_ref[half + l])
        acc[2] += _dot((xv + posv_ref[l:l + 1, :]).astype(BF16), w1v_ref[l])
        acc[3] += _dot((xv + posv_ref[half + l:half + l + 1, :]).astype(BF16), w1v_ref[half + l])
    hk = jax.nn.gelu(acc[0] + pltpu.roll(acc[1], nseg - 1, 0))
    hv = jax.nn.gelu(acc[2] + pltpu.roll(acc[3], nseg - 1, 0))
    kc_ref[0] = _dot(hk.astype(BF16), w2k_ref[...])
    vct_ref[0] = _dot_nt(w2vt_ref[...], hv.astype(BF16))


def _compress(kvc, posk, posv, w1k, w1v, w2k, w2v):
    bsz, s, width = kvc.shape
    nseg = s // CMP_STRIDE
    c2 = lambda bi: (0, 0)
    c3 = lambda bi: (0, 0, 0)
    return pl.pallas_call(
        functools.partial(_compress_kernel, nseg=nseg),
        grid=(bsz,),
        in_specs=[
            pl.BlockSpec((1, s, C_KV_WIDTH), lambda bi: (bi, 0, 0)),
            pl.BlockSpec((1, s, C_KV_WIDTH), lambda bi: (bi, 0, 1)),
            pl.BlockSpec(posk.shape, c2),
            pl.BlockSpec(posv.shape, c2),
            pl.BlockSpec(w1k.shape, c3),
            pl.BlockSpec(w1v.shape, c3),
            pl.BlockSpec(w2k.shape, c2),
            pl.BlockSpec(w2v.shape, c2),
        ],
        out_specs=[
            pl.BlockSpec((1, nseg, C_KV_WIDTH), lambda bi: (bi, 0, 0)),
            pl.BlockSpec((1, C_KV_WIDTH, nseg), lambda bi: (bi, 0, 0)),
        ],
        out_shape=[
            jax.ShapeDtypeStruct((bsz, nseg, C_KV_WIDTH), F32),
            jax.ShapeDtypeStruct((bsz, C_KV_WIDTH, nseg), F32),
        ],
        compiler_params=_params(("arbitrary",), 32),
        name="nsa_compress",
    )(kvc, kvc, posk, posv, w1k, w1v, w2k, w2v)


def _blockdiag2(w):
    z = jnp.zeros_like(w)
    return jnp.concatenate([jnp.concatenate([w, z], axis=-1), jnp.concatenate([z, w], axis=-1)], axis=-2)


def _overlap_t(nseg, ns):
    nc = nseg - 1
    c0 = np.arange(nc)[None, :] * CMP_STRIDE
    s0 = np.arange(ns)[:, None] * SEL_BLOCK
    ov = np.clip(np.minimum(c0 + CMP_BLOCK, s0 + SEL_BLOCK) - np.maximum(c0, s0), 0, None) / CMP_BLOCK
    out = np.zeros((ns, nseg), np.float32)
    out[:, :nc] = ov
    return jnp.asarray(out, dtype=BF16)


def _nsa_kernel(qt_ref, gt_ref, ks_ref, vst_ref, kw_ref, vwt_ref, kc_ref, vct_ref, ovt_ref, e_ref, o_ref,
                qb_ref, m_ref, alpha_ref, acc_ref, res_ref, sbuf_ref, pbuf_ref,
                cbuf_ref, pcmp_ref, *, nseg, ns, n_sel, n_chunks, nb):
    i = pl.program_id(1)
    s0 = i * CHUNK
    width = C_HEADS * CHUNK
    blk_w = 2 * CHUNK
    n_blk = C_HEADS // 2
    blk_per_g = n_blk // C_KV_HEADS
    seqs = range(nb)

    @pl.when((pl.program_id(0) == 0) & (i == 0))
    def _():
        brow = lax.broadcasted_iota(jnp.int32, (C_KV_WIDTH, width), 0)
        for bb in seqs:
            qb_ref[bb, 0:C_KV_WIDTH, :] = jnp.zeros((C_KV_WIDTH, width), BF16)
            qb_ref[bb, 2 * C_KV_WIDTH:, :] = jnp.where(brow == 1, NEG, 0.0).astype(BF16)

    zero = jnp.zeros((HEAD_DIM, CHUNK), BF16)
    for bb in seqs:
        for h in range(C_HEADS):
            qh = qt_ref[bb, h * HEAD_DIM:(h + 1) * HEAD_DIM, :]
            blk = jnp.concatenate([qh, zero] if h < C_REP else [zero, qh], axis=0)
            qb_ref[bb, C_KV_WIDTH:2 * C_KV_WIDTH, h * CHUNK:(h + 1) * CHUNK] = blk
    ones_tile = jnp.where(lax.broadcasted_iota(jnp.int32, (2 * SUBLANES, CHUNK), 0) == 0, 1.0, 0.0).astype(BF16)

    krow = lax.broadcasted_iota(jnp.int32, (CHUNK, blk_w), 0)
    t_loc = lax.broadcasted_iota(jnp.int32, (CHUNK, blk_w), 1) % CHUNK
    tri_diag = jnp.where(krow <= t_loc, 0.0, NEG)
    tri_old = jnp.where(krow > t_loc, 0.0, NEG)

    def cmp_scores():
        for bb in seqs:
            cbuf_ref[bb] = _dot(kc_ref[bb].astype(BF16), qb_ref[bb, C_KV_WIDTH:2 * C_KV_WIDTH, :])

    def cmp_softmax():
        crow = lax.broadcasted_iota(jnp.int32, (nseg, blk_w), 0)
        c_t = s0 + lax.broadcasted_iota(jnp.int32, (nseg, blk_w), 1) % CHUNK
        cbias = jnp.where(crow * CMP_STRIDE + (CMP_BLOCK - 1) <= c_t, 0.0, NEG)
        imps = []
        for bb in seqs:
            psum = [None] * C_KV_HEADS
            for b in range(n_blk):
                g = b // blk_per_g
                cols = slice(b * blk_w, (b + 1) * blk_w)
                sc = cbuf_ref[bb, :, cols] + cbias
                mx = jnp.max(sc, axis=0, keepdims=True)
                e = jnp.exp2(sc - mx)
                den = jnp.sum(e, axis=0, keepdims=True)
                p = e * jnp.where(mx > 0.5 * NEG, 1.0 / den, 0.0)
                pcmp_ref[bb, :, cols] = p.astype(BF16)
                both = p[:, :CHUNK] + p[:, CHUNK:]
                psum[g] = both if psum[g] is None else psum[g] + both
            imps.append([_split_dot_left(ovt_ref[...], ps) for ps in psum])
        return imps

    def cmp_output():
        for bb in seqs:
            vct = vct_ref[bb].astype(BF16)
            for b in range(n_blk):
                g = b // blk_per_g
                cols = slice(b * blk_w, (b + 1) * blk_w)
                res_ref[bb, :, cols] = _dot(vct[g * HEAD_DIM:(g + 1) * HEAD_DIM, :], pcmp_ref[bb, :, cols])

    def select_blocks(all_imps):
        j = lax.broadcasted_iota(jnp.int32, (ns, CHUNK), 0)
        cur = (s0 + lax.broadcasted_iota(jnp.int32, (ns, CHUNK), 1)) // SEL_BLOCK
        forced = (j == 0) | (j == cur) | (j == cur - 1)
        future = j > cur
        slab_rows = lax.broadcasted_iota(jnp.int32, (SUBLANES, CHUNK), 0)
        for bb, g in [(bb, g) for bb in seqs for g in range(C_KV_HEADS)]:
            imps = all_imps[bb]
            imp = jnp.where(forced, BIG, jnp.where(future, NEG, imps[g]))
            slabs = [imp[v * SUBLANES:(v + 1) * SUBLANES, :] for v in range(ns // SUBLANES)]
            ranks = [jnp.zeros((SUBLANES, CHUNK), F32) for _ in slabs]
            for i2 in range(ns):
                r_i = imp[i2:i2 + 1, :]
                for v, slab in enumerate(slabs):
                    if (v + 1) * SUBLANES - 1 <= i2:
                        ranks[v] = ranks[v] + jnp.where(r_i > slab, 1.0, 0.0)
                    elif v * SUBLANES > i2:
                        ranks[v] = ranks[v] + jnp.where(r_i >= slab, 1.0, 0.0)
                    else:
                        tie = jnp.where(slab_rows > i2 - v * SUBLANES, 1.0, 0.0)
                        ranks[v] = ranks[v] + jnp.where(r_i > slab, 1.0, 0.0) + jnp.where(r_i == slab, tie, 0.0)
            rank = jnp.concatenate(ranks, axis=0)
            sel_bias = jnp.where(rank < n_sel, 0.0, NEG).astype(BF16)
            for r in range(C_REP):
                h = g * C_REP + r
                qb_ref[bb, 0:ns, h * CHUNK:(h + 1) * CHUNK] = sel_bias

    def scores(br, k_ref, c, tile, slot):
        off = pl.multiple_of(c * CHUNK, CHUNK)
        e_tile = e_ref[tile]
        for bb in seqs:
            kch = k_ref[bb, pl.ds(off, CHUNK), :]
            if br == 0:
                sbuf_ref[bb, 0, slot] = _dot(jnp.concatenate([kch, e_tile], axis=1),
                                             qb_ref[bb, C_KV_WIDTH:3 * C_KV_WIDTH, :])
            else:
                sbuf_ref[bb, 1, slot] = _dot(jnp.concatenate([e_tile, kch], axis=1),
                                             qb_ref[bb, 0:2 * C_KV_WIDTH, :])

    def softmax(br, slot, tri):
        for bb in seqs:
            for b in range(n_blk):
                cols = slice(b * blk_w, (b + 1) * blk_w)
                s = sbuf_ref[bb, br, slot, :, cols]
                if tri is not None:
                    s = s + tri
                m_old = m_ref[bb, br, :, cols]
                m_new = jnp.maximum(m_old, jnp.max(s, axis=0, keepdims=True))
                alpha_ref[bb, br, :, cols] = jnp.exp2(m_old - m_new)
                pbuf_ref[bb, br, :, cols] = jnp.exp2(s - m_new).astype(BF16)
                m_ref[bb, br, :, cols] = m_new

    def accumulate(br, vt_ref, c, first=False):
        off = pl.multiple_of(c * CHUNK, CHUNK)
        for bb in seqs:
            vt = vt_ref[bb, :, pl.ds(off, CHUNK)]
            vone = [jnp.concatenate([vt[g * HEAD_DIM:(g + 1) * HEAD_DIM, :], ones_tile], axis=0)
                    for g in range(C_KV_HEADS)]
            for b in range(n_blk):
                cols = slice(b * blk_w, (b + 1) * blk_w)
                pv = _dot(vone[b // blk_per_g], pbuf_ref[bb, br, :, cols])
                acc_ref[bb, br, :, cols] = (pv if first else
                                            alpha_ref[bb, br, :, cols] * acc_ref[bb, br, :, cols] + pv)

    n_back = WINDOW // CHUNK
    chunk =[jnp.maximum(i - n_back + w, 0) for w in range(n_back + 1)]
    tile = [jnp.where(i - n_back + w < 0, n_chunks + 1, n_chunks) for w in range(n_back + 1)]

    def win_stage(w):
        if w >= 1:
            accumulate(0, vwt_ref, chunk[w - 1], first=(w == 1))
        softmax(0, w % 2, tri_old if w == 0 else (tri_diag if w == n_back else None))
        if w < n_back:
            scores(0, kw_ref, chunk[w + 1], tile[w + 1], (w + 1) % 2)

    cmp_scores()
    m_ref[:, 0] = jnp.full((nb, 1, width), NEG, F32)
    scores(0, kw_ref, chunk[0], tile[0], 0)
    imps = cmp_softmax()
    win_stage(0)
    win_stage(1)
    select_blocks(imps)

    m_ref[:, 1] = jnp.full((nb, 1, width), NEG, F32)
    alpha_ref[:, 1] = jnp.ones((nb, 1, width), F32)
    for bb in seqs:
        acc_ref[bb, 1] = jnp.zeros(acc_ref.shape[2:], F32)
        pbuf_ref[bb, 1] = jnp.zeros(pbuf_ref.shape[2:], BF16)
    scores(1, ks_ref, 0, 0, 0)
    win_stage(2)

    def sel_stage(k, slot, slot_next):
        accumulate(1, vst_ref, jnp.maximum(k - 1, 0))
        softmax(1, slot, None)
        scores(1, ks_ref, k + 1, k + 1, slot_next)

    def sel_pair(kk, carry):
        sel_stage(2 * kk, 0, 1)
        sel_stage(2 * kk + 1, 1, 0)
        return carry

    lax.fori_loop(0, i // 2, sel_pair, 0)

    @pl.when(i % 2 == 1)
    def _():
        sel_stage(i - 1, 0, 1)

    accumulate(1, vst_ref, jnp.maximum(i - 1, 0))
    win_stage(3)
    softmax(1, i % 2, tri_diag)
    win_stage(4)
    accumulate(1, vst_ref, i)
    accumulate(0, vwt_ref, chunk[n_back])
    cmp_output()

    for bb in seqs:
        gates = jax.nn.sigmoid(gt_ref[bb])
        den = [acc_ref[bb, br, HEAD_DIM:HEAD_DIM + 1, :] for br in (0, 1)]
        inv = [jnp.where(x > 0.0, 1.0 / x, 0.0) for x in den]
        for pair in range(C_HEADS // 2):
            pieces = []
            for h in (2 * pair, 2 * pair + 1):
                cs = slice(h * CHUNK, (h + 1) * CHUNK)
                pieces.append(gates[3 * h:3 * h + 1, :] * res_ref[bb, :, cs]
                              + (gates[3 * h + 1:3 * h + 2, :] * inv[1][:, cs]) * acc_ref[bb, 1, 0:HEAD_DIM, cs]
                              + (gates[3 * h + 2:3 * h + 3, :] * inv[0][:, cs]) * acc_ref[bb, 0, 0:HEAD_DIM, cs])
            o_ref[bb, :, pair * 2 * HEAD_DIM:(pair + 1) * 2 * HEAD_DIM] = (
                jnp.concatenate(pieces, axis=0).T.astype(o_ref.dtype))


def _route_tiles(n_chunks, ns):
    e = np.zeros((n_chunks + 2, CHUNK, C_KV_WIDTH), np.float32)
    r = np.arange(CHUNK)
    for c in range(n_chunks):
        e[c, r, 2 * c + r // SEL_BLOCK] = 1.0
    e[n_chunks, :, 0] = 1.0
    e[n_chunks + 1, :, 1] = 1.0
    return jnp.asarray(e, dtype=BF16)


def _nsa(ptq, gt, ksw, kc, vct, ovt, route):
    bsz, s, _ = ksw.shape
    nseg = s // CMP_STRIDE
    ns = s // SEL_BLOCK
    n_chunks = s // CHUNK
    width = C_HEADS * CHUNK
    nb = 4 if bsz % 4 == 0 else (2 if bsz % 2 == 0 else 1)
    return pl.pallas_call(
        functools.partial(_nsa_kernel, nseg=nseg, ns=ns, n_sel=min(N_SEL, ns), n_chunks=n_chunks, nb=nb),
        grid=(bsz // nb, n_chunks),
        in_specs=[
            pl.BlockSpec((nb, C_WIDTH, CHUNK), lambda b, i: (b, PT_Q // C_WIDTH, i)),
            pl.BlockSpec((nb, gt.shape[1], CHUNK), lambda b, i: (b, 0, i)),
            pl.BlockSpec((nb, s, C_KV_WIDTH), lambda b, i: (b, 0, 0)),
            pl.BlockSpec((nb, C_KV_WIDTH, s), lambda b, i: (b, PT_VS // C_KV_WIDTH, 0)),
            pl.BlockSpec((nb, s, C_KV_WIDTH), lambda b, i: (b, 0, 1)),
            pl.BlockSpec((nb, C_KV_WIDTH, s), lambda b, i: (b, PT_VW // C_KV_WIDTH, 0)),
            pl.BlockSpec((nb, nseg, C_KV_WIDTH), lambda b, i: (b, 0, 0)),
            pl.BlockSpec((nb, C_KV_WIDTH, nseg), lambda b, i: (b, 0, 0)),
            pl.BlockSpec(ovt.shape, lambda b, i: (0, 0)),
            pl.BlockSpec(route.shape, lambda b, i: (0, 0, 0)),
        ],
        out_specs=pl.BlockSpec((nb, CHUNK, C_WIDTH), lambda b, i: (b, i, 0)),
        out_shape=jax.ShapeDtypeStruct((bsz, s, C_WIDTH), BF16),
        scratch_shapes=[
            pltpu.VMEM((nb, 3 * C_KV_WIDTH, width), BF16),
            pltpu.VMEM((nb, 2, 1, width), F32),
            pltpu.VMEM((nb, 2, 1, width), F32),
            pltpu.VMEM((nb, 2, HEAD_DIM + 2 * SUBLANES, width), F32),
            pltpu.VMEM((nb, HEAD_DIM, width), F32),
            pltpu.VMEM((nb, 2, 2, CHUNK, width), F32),
            pltpu.VMEM((nb, 2, CHUNK, width), BF16),
            pltpu.VMEM((nb, nseg, width), F32),
            pltpu.VMEM((nb, nseg, width), BF16),
        ],
        compiler_params=_params(("arbitrary", "arbitrary"), 48),
        name="nsa_attention",
    )(ptq, gt, ksw, ptq, ksw, ptq, kc, vct, ovt, route)


def _tail_kernel(ya_ref, yb_ref, yc_ref, x_ref, mod_ref, wo_ref, g1_ref, b1_ref,
                 wup_ref, cw_ref, cb_ref, wdn_ref, g2_ref, b2_ref, o_ref,
                 work_ref, carry_ref, x1_ref, h_ref, act_ref, *, d, dff, cwid, tm, alpha):
    @pl.when(pl.program_id(1) == 0)
    def _():
        carry_ref[...] = jnp.zeros_like(carry_ref)

    y1 = _dot(ya_ref[0], wo_ref[0:A_WIDTH, :])
    y1 += _dot(yb_ref[0], wo_ref[A_WIDTH:A_WIDTH + B_WIDTH, :])
    y1 += _dot(yc_ref[0], wo_ref[A_WIDTH + B_WIDTH:, :])
    x1 = _row_ln(alpha * x_ref[0] + mod_ref[0, :, 2 * d:3 * d] * y1, g1_ref[...], b1_ref[...])
    x1_ref[...] = x1

    sh = mod_ref[0, :, 3 * d:4 * d]
    sc = mod_ref[0, :, 4 * d:5 * d]
    gate = mod_ref[0, :, 5 * d:6 * d]
    h_ref[...] = (x1 * (1.0 + sc) + sh).astype(BF16)
    pad = SUBLANES
    n_chunks = dff // cwid

    def up(ci):
        slot = ci % 2
        for part in range(2):
            cs = slice(part * dff + ci * cwid, part * dff + (ci + 1) * cwid)
            work_ref[slot, part, 0:pad, :] = carry_ref[:, cs]
            work_ref[slot, part, pad:pad + tm, :] = _dot(h_ref[...], wup_ref[:, cs])

    def conv_act(ci):
        slot = ci % 2
        conv = []
        for part in range(2):
            cs = slice(part * dff + ci * cwid, part * dff + (ci + 1) * cwid)
            buf = work_ref[slot, part]
            carry_ref[:, cs] = buf[tm:tm + pad, :]
            a = buf[pad:, :]
            a1 = pltpu.roll(buf, 1, 0)[pad:, :]
            a2 = pltpu.roll(buf, 2, 0)[pad:, :]
            conv.append(cw_ref[0:1, cs] * a2 + cw_ref[1:2, cs] * a1 + cw_ref[2:3, cs] * a + cb_ref[:, cs])
        half = 0.5 * conv[0]
        silu = half + half * jnp.tanh(half)
        act_ref[:, ci * cwid:(ci + 1) * cwid] = (silu * conv[1]).astype(BF16)

    split = (n_chunks // 2) * cwid
    up(0)
    y = None
    for ci in range(n_chunks):
        if ci + 1 < n_chunks:
            up(ci + 1)
        conv_act(ci)
        if (ci + 1) * cwid == split:
            y = _dot(act_ref[:, :split], wdn_ref[:split, :])
    y = y + _dot(act_ref[:, split:], wdn_ref[split:, :])
    o_ref[0] = _row_ln(alpha * x1_ref[...] + gate * y, g2_ref[...], b2_ref[...])


def _tail(ya, yb, yc, x, mod, layer, wo, g1, b1, wup, cw, cb, wdn, g2, b2, alpha, tm=512, cwid=256):
    bsz, s, d = x.shape
    dff = wdn.shape[1]
    const = lambda bi, j: (0, 0)
    tile = lambda w: pl.BlockSpec((1, tm, w), lambda bi, j: (bi, j, 0))
    weight = lambda w: pl.BlockSpec((None,) + w.shape[1:], lambda bi, j: (layer, 0, 0),
                                    pipeline_mode=pl.Buffered(1))
    small = lambda a: pl.BlockSpec(a.shape, const)
    return pl.pallas_call(
        functools.partial(_tail_kernel, d=d, dff=dff, cwid=cwid, tm=tm, alpha=alpha),
        grid=(bsz, s // tm),
        in_specs=[
            tile(A_WIDTH), tile(B_WIDTH), tile(C_WIDTH), tile(d),
            pl.BlockSpec((1, 1, 6 * d), lambda bi, j: (bi, 0, 0)),
            weight(wo), small(g1), small(b1),
            weight(wup), small(cw), small(cb), weight(wdn), small(g2), small(b2),
        ],
        out_specs=tile(d),
        out_shape=jax.ShapeDtypeStruct((bsz, s, d), F32),
        scratch_shapes=[
            pltpu.VMEM((2, 2, tm + SUBLANES, cwid), F32),
            pltpu.VMEM((SUBLANES, 2 * dff), F32),
            pltpu.VMEM((tm, d), F32),
            pltpu.VMEM((tm, d), BF16),
            pltpu.VMEM((tm, dff), BF16),
        ],
        compiler_params=_params(("arbitrary", "arbitrary"), 52),
        name="layer_tail",
    )(ya, yb, yc, x, mod, wo, g1, b1, wup, cw, cb, wdn, g2, b2)


def _inproj_weights(w):
    cols = np.arange(B_WIDTH)
    half = cols // (B_WIDTH // 2)
    perm = ((cols % (B_WIDTH // 2)) // (HEAD_DIM // 2)) * HEAD_DIM + half * (HEAD_DIM // 2) + cols % (HEAD_DIM // 2)
    za = w[:, 0:512]
    qb, kb, vb, gb = (w[:, 512 + i * B_WIDTH:512 + (i + 1) * B_WIDTH] for i in range(4))
    qc = w[:, 1536:2048]
    kcm, vcm, ksl, vsl, kwn, vwn = (w[:, 2048 + i * C_KV_WIDTH:2048 + (i + 1) * C_KV_WIDTH] for i in range(6))
    gc = w[:, 2816:2840]
    wn = jnp.concatenate([qb[:, perm], kb[:, perm], vb, gb, za, ksl, kwn, kcm, vcm], axis=1)
    pad = jnp.zeros((w.shape[0], PT_ROWS - PT_G - gc.shape[1]), w.dtype)
    wt = jnp.concatenate([qc * (HEAD_DIM ** -0.5 * LOG2E), vsl, vwn, gc, pad], axis=1).T
    return wn.astype(BF16), wt.astype(BF16)


def kernel(x, c, w_ada, b_ada, w_in, a_ln_g, a_ln_b, a_ws, a_bs, b_gn_g, b_gn_b, c_pos_k, c_w1_k, c_w2_k,
           c_pos_v, c_w1_v, c_w2_v, w_out, ln1_g, ln1_b, w_up, conv_w, conv_b, w_down, ln2_g, ln2_b):
    depth = w_in.shape[0]
    bsz, s, d = x.shape
    alpha = (2 * depth) ** 0.25
    nseg = s // CMP_STRIDE
    ns = s // SEL_BLOCK

    lanes = np.arange(A_WIDTH)
    mavg = jnp.asarray((lanes[:, None] // HEAD_DIM == lanes[None, :] // HEAD_DIM) / HEAD_DIM, dtype=BF16)
    gm = jnp.asarray((lanes[None, :] // HEAD_DIM == np.arange(A_GROUPS)[:, None]).astype(np.float32))
    ret_consts = _retention_consts()
    cos, sin = _rotary_tables(s)
    ovt = _overlap_t(nseg, ns)
    route = _route_tiles(s // CHUNK, ns)

    wo_bf, wup_bf, wdn_bf = w_out.astype(BF16), w_up.astype(BF16), w_down.astype(BF16)
    mods = _ada_mod(c, w_ada, b_ada)
    for l in range(depth):
        mod = mods[l][:, None, :]
        wn, wt = _inproj_weights(w_in[l])
        pn, ksw, kvc, ptq, gt = _inproj(x, mod, wn, wt)

        wcat = jnp.transpose(a_ws[l], (1, 0, 2)).reshape(CHUNK, A_GROUPS * CHUNK)
        bias = jnp.repeat(a_bs[l].T, HEAD_DIM, axis=1)
        ya = _mixer_a(pn, wcat, bias, a_ln_g[l].reshape(1, A_WIDTH), a_ln_b[l].reshape(1, A_WIDTH), mavg, gm)

        yb = _retention(pn, cos, sin, ret_consts, mavg,
                        b_gn_g[l].reshape(1, B_WIDTH), b_gn_b[l].reshape(1, B_WIDTH))

        w1k = _blockdiag2(c_w1_k[l].reshape(CMP_BLOCK, HEAD_DIM, HEAD_DIM)).astype(BF16)
        w1v = _blockdiag2(c_w1_v[l].reshape(CMP_BLOCK, HEAD_DIM, HEAD_DIM)).astype(BF16)
        kc, vct = _compress(kvc, jnp.tile(c_pos_k[l], (1, C_KV_HEADS)), jnp.tile(c_pos_v[l], (1, C_KV_HEADS)),
                            w1k, w1v, _blockdiag2(c_w2_k[l]).astype(BF16), _blockdiag2(c_w2_v[l]).T.astype(BF16))
        yc = _nsa(ptq, gt, ksw, kc, vct, ovt, route)

        x = _tail(ya, yb, yc, x, mod, l, wo_bf, ln1_g[l].reshape(1, d), ln1_b[l].reshape(1, d),
                  wup_bf, conv_w[l], conv_b[l].reshape(1, -1), wdn_bf,
                  ln2_g[l].reshape(1, d), ln2_b[l].reshape(1, d), alpha)
    return x
```

```python
import functools

import numpy as np
import jax
import jax.numpy as jnp
from jax import lax
from jax.experimental import pallas as pl
from jax.experimental.pallas import tpu as pltpu

F32 = jnp.float32
BF16 = jnp.bfloat16

HEAD_DIM = 64
A_GROUPS = 4
A_WIDTH = A_GROUPS * HEAD_DIM
CHUNK = 128
B_HEADS = 4
B_WIDTH = B_HEADS * HEAD_DIM
ROPE_BASE = 10000.0
C_HEADS = 8
C_KV_HEADS = 2
C_REP = C_HEADS // C_KV_HEADS
C_WIDTH = C_HEADS * HEAD_DIM
C_KV_WIDTH = C_KV_HEADS * HEAD_DIM
CMP_BLOCK = 32
CMP_STRIDE = 16
SEL_BLOCK = 64
N_SEL = 8
WINDOW = 512
CONV_WIDTH = 3
LN_EPS = 1e-5
NEG = -1e30
BIG = 1e30
LOG2E = 1.4426950408889634

LANES = 128
SUBLANES = 8
VMEM_BYTES = 64 * 1024 * 1024

PN_RET, PN_ZA, PN_KS, PN_KW, PN_KVC = 0, 1024, 1536, 1664, 1792
PN_COLS = 1536
PT_Q, PT_VS, PT_VW, PT_G = 0, 512, 640, 768
PT_ROWS = 800


def _dot(a, b):
    return jnp.dot(a, b, preferred_element_type=F32)


def _dot_nt(a, b):
    return lax.dot_general(a, b, (((1,), (1,)), ((), ())), preferred_element_type=F32)


def _dot_tn(a, b):
    return lax.dot_general(a, b, (((0,), (0,)), ((), ())), preferred_element_type=F32)


def _split_dot(x, m):
    hi = x.astype(BF16)
    lo = (x - hi.astype(F32)).astype(BF16)
    return _dot(hi, m) + _dot(lo, m)


def _split_dot_left(m, x):
    hi = x.astype(BF16)
    lo = (x - hi.astype(F32)).astype(BF16)
    return _dot(m, hi) + _dot(m, lo)


def _group_ln(x, mavg, g, b):
    mu = _split_dot(x, mavg)
    d = x - mu
    var = _split_dot(d * d, mavg)
    return d * lax.rsqrt(var + LN_EPS) * g + b


def _row_ln(x, g, b):
    mu = jnp.mean(x, axis=-1, keepdims=True)
    d = x - mu
    var = jnp.mean(d * d, axis=-1, keepdims=True)
    return d * lax.rsqrt(var + LN_EPS) * g + b


def _params(sem, vmem_mb):
    return pltpu.CompilerParams(dimension_semantics=sem, vmem_limit_bytes=vmem_mb * 1024 * 1024)


def _ada_kernel(c_ref, w_ref, b_ref, o_ref):
    cond = jax.nn.silu(c_ref[...]).astype(BF16)
    o_ref[0] = _dot(cond, w_ref[0].astype(BF16)) + b_ref[0]


def _ada_mod(c, w_ada, b_ada):
    depth, d, n = w_ada.shape
    bsz = c.shape[0]
    tn = 1536
    return pl.pallas_call(
        _ada_kernel,
        grid=(depth, n // tn),
        in_specs=[
            pl.BlockSpec((bsz, d), lambda l, j: (0, 0)),
            pl.BlockSpec((1, d, tn), lambda l, j: (l, 0, j)),
            pl.BlockSpec((1, 1, tn), lambda l, j: (l, 0, j)),
        ],
        out_specs=pl.BlockSpec((1, bsz, tn), lambda l, j: (l, 0, j)),
        out_shape=jax.ShapeDtypeStruct((depth, bsz, n), F32),
        compiler_params=_params(("arbitrary", "arbitrary"), 40),
        name="ada_mod",
    )(c, w_ada, b_ada.reshape(depth, 1, n))


def _inproj_kernel(x_ref, mod_ref, wn_ref, wt_ref, cos_ref, sin_ref, dec_ref, zeta_ref, xi_ref, cd_ref, qm_ref,
                   vm_ref, bm_ref, mavg_ref, gng_ref, gnb_ref, wcat_ref, abias_ref, alg_ref, alb_ref, gm_ref,
                   ya_ref, yb_ref, ksw_ref, kvc_ref, ptq_ref, gt_ref, pn_ref, state_ref, *, d, n_chunks):
    @pl.when(pl.program_id(1) == 0)
    def _():
        state_ref[...] = jnp.zeros_like(state_ref)

    sh = mod_ref[0, :, 0:d]
    sc = mod_ref[0, :, d:2 * d]
    h = (x_ref[0] * (1.0 + sc) + sh).astype(BF16)
    pn_ref[...] = _dot(h, wn_ref[:, :PN_COLS])
    src = lambda rows, lo, hi: pn_ref[rows, lo:hi]
    rest = _dot(h, wn_ref[:, PN_COLS:])
    ksw_ref[0] = rest[:, :PN_KVC - PN_COLS].astype(BF16)
    kvc_ref[0] = rest[:, PN_KVC - PN_COLS:]
    _retention_chunks(src, cos_ref, sin_ref, dec_ref, zeta_ref, xi_ref, cd_ref, qm_ref, vm_ref, bm_ref,
                      mavg_ref, gng_ref, gnb_ref, yb_ref, state_ref, n_chunks)
    pt = _dot_nt(wt_ref[...], h)
    ptq_ref[0] = pt[:PT_G, :].astype(BF16)
    gt_ref[0] = pt[PT_G:, :]
    _gmlp_chunks(src, wcat_ref, abias_ref, alg_ref, alb_ref, mavg_ref, gm_ref, ya_ref, n_chunks)


def _inproj(x, mod, wn, wt, cos, sin, ret_consts, mavg, gn_g, gn_b, wcat, abias, al_g, al_b, gm, tm=512):
    bsz, s, d = x.shape
    const = lambda b, j: (0, 0)
    small = lambda a: pl.BlockSpec(a.shape, const)
    rows = lambda w: pl.BlockSpec((1, tm, w), lambda b, j: (b, j, 0))
    return pl.pallas_call(
        functools.partial(_inproj_kernel, d=d, n_chunks=tm // CHUNK),
        grid=(bsz, s // tm),
        in_specs=[
            rows(d),
            pl.BlockSpec((1, 1, 6 * d), lambda b, j: (b, 0, 0)),
            small(wn), small(wt),
            pl.BlockSpec((tm, B_WIDTH // 2), lambda b, j: (j, 0)),
            pl.BlockSpec((tm, B_WIDTH // 2), lambda b, j: (j, 0)),
        ] + [small(a) for a in ret_consts] + [
            small(mavg), small(gn_g), small(gn_b), small(wcat), small(abias), small(al_g), small(al_b), small(gm),
        ],
        out_specs=[
            rows(A_WIDTH), rows(B_WIDTH), rows(2 * C_KV_WIDTH), rows(2 * C_KV_WIDTH),
            pl.BlockSpec((1, PT_G, tm), lambda b, j: (b, 0, j)),
            pl.BlockSpec((1, PT_ROWS - PT_G, tm), lambda b, j: (b, 0, j)),
        ],
        out_shape=[
            jax.ShapeDtypeStruct((bsz, s, A_WIDTH), BF16),
            jax.ShapeDtypeStruct((bsz, s, B_WIDTH), BF16),
            jax.ShapeDtypeStruct((bsz, s, 2 * C_KV_WIDTH), BF16),
            jax.ShapeDtypeStruct((bsz, s, 2 * C_KV_WIDTH), F32),
            jax.ShapeDtypeStruct((bsz, PT_G, s), BF16),
            jax.ShapeDtypeStruct((bsz, PT_ROWS - PT_G, s), F32),
        ],
        scratch_shapes=[
            pltpu.VMEM((tm, PN_COLS), F32),
            pltpu.VMEM((B_WIDTH, B_WIDTH), F32),
        ],
        compiler_params=_params(("arbitrary", "arbitrary"), 48),
        name="inproj_mixers",
    )(x, mod, wn, wt, cos, sin, *ret_consts, mavg, gn_g, gn_b, wcat, abias, al_g, al_b, gm)


def _gmlp_chunks(src, w_ref, bias_ref, g_ref, b_ref, mavg_ref, gm_ref, o_ref, n_chunks):
    row = lax.broadcasted_iota(jnp.int32, (CHUNK, A_GROUPS * CHUNK), 0)
    col = lax.broadcasted_iota(jnp.int32, (CHUNK, A_GROUPS * CHUNK), 1)
    wc = jnp.where((col % CHUNK) <= row, w_ref[...], 0.0).astype(BF16)
    chunks = range(n_chunks)
    rows = [slice(c * CHUNK, (c + 1) * CHUNK) for c in chunks]
    z = [jax.nn.gelu(src(r, PN_ZA, PN_ZA + 2 * A_WIDTH)) for r in rows]
    v = [zc[:, A_WIDTH:] for zc in z]
    mavg = mavg_ref[...]
    mu = [_split_dot(vc, mavg) for vc in v]
    dev = [vc - m for vc, m in zip(v, mu)]
    var = [_split_dot(d * d, mavg) for d in dev]
    vn = [d * lax.rsqrt(s2 + LN_EPS) * g_ref[...] + b_ref[...] for d, s2 in zip(dev, var)]
    vstack = [jnp.concatenate([x * gm_ref[g:g + 1, :] for g in range(A_GROUPS)], axis=0).astype(BF16) for x in vn]
    vs = [_dot(wc, x) + bias_ref[...] for x in vstack]
    for c in chunks:
        o_ref[0, rows[c], :] = (z[c][:, :A_WIDTH] * vs[c]).astype(o_ref.dtype)


def _retention_chunks(src, cos_ref, sin_ref, dec_ref, zeta_ref, xi_ref, cd_ref, qm_ref, vm_ref, bm_ref,
                      mavg_ref, g_ref, b_ref, o_ref, state_ref, n_chunks):
    half = B_WIDTH // 2
    chunks = range(n_chunks)
    rows = [slice(c * CHUNK, (c + 1) * CHUNK) for c in chunks]

    def rot(t, r):
        t1 = t[:, :half]
        t2 = t[:, half:]
        cos = cos_ref[r, :]
        sin = sin_ref[r, :]
        return jnp.concatenate([t1 * cos - t2 * sin, t1 * sin + t2 * cos], axis=1)

    qr = [rot(src(r, 0, B_WIDTH), r) for r in rows]
    kr = [rot(src(r, B_WIDTH, 2 * B_WIDTH), r) * (HEAD_DIM ** -0.5) for r in rows]
    v = [src(r, 2 * B_WIDTH, 3 * B_WIDTH) for r in rows]
    qs = [jnp.concatenate([x * qm_ref[h:h + 1, :] for h in range(B_HEADS)], axis=0).astype(BF16) for x in qr]
    s = [_dot_nt(a, b.astype(BF16)) * dec_ref[...] for a, b in zip(qs, kr)]
    kv = [_dot_tn((a * zeta_ref[...]).astype(BF16), b.astype(BF16)) * bm_ref[...]
          for a, b in zip(kr, v)]
    scat = [jnp.concatenate([x[h * CHUNK:(h + 1) * CHUNK, :] for h in range(B_HEADS)], axis=1).astype(BF16)
            for x in s]
    vstack = [jnp.concatenate([x * vm_ref[h:h + 1, :] for h in range(B_HEADS)], axis=0).astype(BF16) for x in v]
    o_inner = [_dot(a, b) for a, b in zip(scat, vstack)]
    state = state_ref[...]
    before = []
    for c in chunks:
        before.append(state)
        state = state * cd_ref[...] + kv[c]
    state_ref[...] = state
    o = [oi + _dot(a.astype(BF16), st.astype(BF16)) * xi_ref[...] for oi, a, st in zip(o_inner, qr, before)]
    mavg = mavg_ref[...]
    mu = [_split_dot(x, mavg) for x in o]
    dev = [x - m for x, m in zip(o, mu)]
    var = [_split_dot(d * d, mavg) for d in dev]
    for c in chunks:
        normed = dev[c] * lax.rsqrt(var[c] + LN_EPS) * g_ref[...] + b_ref[...]
        gate = src(rows[c], 3 * B_WIDTH, 4 * B_WIDTH)
        o_ref[0, rows[c], :] = (jax.nn.silu(gate) * normed).astype(o_ref.dtype)


def _retention_consts():
    h_n, d, l_n = B_HEADS, HEAD_DIM, CHUNK
    log_gamma = jnp.log1p(-jnp.exp2(-5.0 - jnp.arange(h_n, dtype=F32)))
    idx = jnp.arange(l_n, dtype=F32)
    diff = idx[:, None] - idx[None, :]
    decay_in = jnp.where(diff >= 0, jnp.exp(log_gamma[:, None, None] * jnp.maximum(diff, 0.0)), 0.0)
    xi = jnp.exp(log_gamma[:, None] * (idx + 1.0))
    zeta = jnp.exp(log_gamma[:, None] * (l_n - 1.0 - idx))
    chunk_decay = jnp.exp(log_gamma * l_n)
    cols = np.arange(B_WIDTH)
    head_perm = (cols % (B_WIDTH // 2)) // (d // 2)
    head_std = cols // d
    dec = decay_in.reshape(h_n * l_n, l_n)
    zeta_t = zeta.T[:, head_perm]
    xi_t = xi.T[:, head_std]
    cd = chunk_decay[head_std][None, :]
    qm = jnp.asarray((head_perm[None, :] == np.arange(h_n)[:, None]).astype(np.float32))
    vm = jnp.asarray((head_std[None, :] == np.arange(h_n)[:, None]).astype(np.float32))
    bm = jnp.asarray((head_perm[:, None] == head_std[None, :]).astype(np.float32))
    return [dec, zeta_t, xi_t, cd, qm, vm, bm]


def _rotary_tables(s):
    half = HEAD_DIM // 2
    inv = jnp.power(ROPE_BASE, -jnp.arange(half, dtype=F32) / half)
    ang = jnp.arange(s).astype(F32)[:, None] * inv[None, :]
    return jnp.tile(jnp.cos(ang), (1, B_HEADS)), jnp.tile(jnp.sin(ang), (1, B_HEADS))


def _compress_kernel(xk_ref, xv_ref, posk_ref, posv_ref, w1k_ref, w1v_ref, w2k_ref, w2vt_ref, kc_ref, vct_ref,
                     *, nseg):
    half = CMP_BLOCK // 2
    acc = [jnp.zeros((nseg, C_KV_WIDTH), F32) for _ in range(4)]
    for l in range(half):
        xk = xk_ref[0, pl.ds(l, nseg, stride=CMP_STRIDE), :]
        xv = xv_ref[0, pl.ds(l, nseg, stride=CMP_STRIDE), :]
        acc[0] += _dot((xk + posk_ref[l:l + 1, :]).astype(BF16), w1k_ref[l])
        acc[1] += _dot((xk + posk_ref[half + l:half + l + 1, :]).astype(BF16), w1k_ref[half + l])
        acc[2] += _dot((xv + posv_ref[l:l + 1, :]).astype(BF16), w1v_ref[l])
        acc[3] += _dot((xv + posv_ref[half + l:half + l + 1, :]).astype(BF16), w1v_ref[half + l])
    hk = jax.nn.gelu(acc[0] + pltpu.roll(acc[1], nseg - 1, 0))
    hv = jax.nn.gelu(acc[2] + pltpu.roll(acc[3], nseg - 1, 0))
    kc_ref[0] = _dot(hk.astype(BF16), w2k_ref[...])
    vct_ref[0] = _dot_nt(w2vt_ref[...], hv.astype(BF16))


def _compress(kvc, posk, posv, w1k, w1v, w2k, w2v):
    bsz, s, width = kvc.shape
    nseg = s // CMP_STRIDE
    c2 = lambda bi: (0, 0)
    c3 = lambda bi: (0, 0, 0)
    return pl.pallas_call(
        functools.partial(_compress_kernel, nseg=nseg),
        grid=(bsz,),
        in_specs=[
            pl.BlockSpec((1, s, C_KV_WIDTH), lambda bi: (bi, 0, 0)),
            pl.BlockSpec((1, s, C_KV_WIDTH), lambda bi: (bi, 0, 1)),
            pl.BlockSpec(posk.shape, c2),
            pl.BlockSpec(posv.shape, c2),
            pl.BlockSpec(w1k.shape, c3),
            pl.BlockSpec(w1v.shape, c3),
            pl.BlockSpec(w2k.shape, c2),
            pl.BlockSpec(w2v.shape, c2),
        ],
        out_specs=[
            pl.BlockSpec((1, nseg, C_KV_WIDTH), lambda bi: (bi, 0, 0)),
            pl.BlockSpec((1, C_KV_WIDTH, nseg), lambda bi: (bi, 0, 0)),
        ],
        out_shape=[
            jax.ShapeDtypeStruct((bsz, nseg, C_KV_WIDTH), F32),
            jax.ShapeDtypeStruct((bsz, C_KV_WIDTH, nseg), F32),
        ],
        compiler_params=_params(("arbitrary",), 32),
        name="nsa_compress",
    )(kvc, kvc, posk, posv, w1k, w1v, w2k, w2v)


def _blockdiag2(w):
    z = jnp.zeros_like(w)
    return jnp.concatenate([jnp.concatenate([w, z], axis=-1), jnp.concatenate([z, w], axis=-1)], axis=-2)


def _overlap_t(nseg, ns):
    nc = nseg - 1
    c0 = np.arange(nc)[None, :] * CMP_STRIDE
    s0 = np.arange(ns)[:, None] * SEL_BLOCK
    ov = np.clip(np.minimum(c0 + CMP_BLOCK, s0 + SEL_BLOCK) - np.maximum(c0, s0), 0, None) / CMP_BLOCK
    out = np.zeros((ns, nseg), np.float32)
    out[:, :nc] = ov
    return jnp.asarray(out, dtype=BF16)


def _nsa_kernel(qt_ref, gt_ref, ks_ref, vst_ref, kw_ref, vwt_ref, kc_ref, vct_ref, ovt_ref, e_ref, o_ref,
                qb_ref, m_ref, alpha_ref, acc_ref, res_ref, sbuf_ref, pbuf_ref,
                cbuf_ref, pcmp_ref, *, nseg, ns, n_sel, n_chunks, nb):
    i = pl.program_id(1)
    s0 = i * CHUNK
    width = C_HEADS * CHUNK
    blk_w = 2 * CHUNK
    n_blk = C_HEADS // 2
    blk_per_g = n_blk // C_KV_HEADS
    seqs = range(nb)

    @pl.when((pl.program_id(0) == 0) & (i == 0))
    def _():
        brow = lax.broadcasted_iota(jnp.int32, (C_KV_WIDTH, width), 0)
        for bb in seqs:
            qb_ref[bb, 0:C_KV_WIDTH, :] = jnp.zeros((C_KV_WIDTH, width), BF16)
            qb_ref[bb, 2 * C_KV_WIDTH:, :] = jnp.where(brow == 1, NEG, 0.0).astype(BF16)

    zero = jnp.zeros((HEAD_DIM, CHUNK), BF16)
    for bb in seqs:
        for h in range(C_HEADS):
            qh = qt_ref[bb, h * HEAD_DIM:(h + 1) * HEAD_DIM, :]
            blk = jnp.concatenate([qh, zero] if h < C_REP else [zero, qh], axis=0)
            qb_ref[bb, C_KV_WIDTH:2 * C_KV_WIDTH, h * CHUNK:(h + 1) * CHUNK] = blk
    ones_tile = jnp.where(lax.broadcasted_iota(jnp.int32, (2 * SUBLANES, CHUNK), 0) == 0, 1.0, 0.0).astype(BF16)

    krow = lax.broadcasted_iota(jnp.int32, (CHUNK, blk_w), 0)
    t_loc = lax.broadcasted_iota(jnp.int32, (CHUNK, blk_w), 1) % CHUNK
    tri_diag = jnp.where(krow <= t_loc, 0.0, NEG)
    tri_old = jnp.where(krow > t_loc, 0.0, NEG)

    def cmp_scores():
        for bb in seqs:
            cbuf_ref[bb] = _dot(kc_ref[bb].astype(BF16), qb_ref[bb, C_KV_WIDTH:2 * C_KV_WIDTH, :])

    def cmp_softmax():
        crow = lax.broadcasted_iota(jnp.int32, (nseg, blk_w), 0)
        c_t = s0 + lax.broadcasted_iota(jnp.int32, (nseg, blk_w), 1) % CHUNK
        cbias = jnp.where(crow * CMP_STRIDE + (CMP_BLOCK - 1) <= c_t, 0.0, NEG)
        imps = []
        for bb in seqs:
            psum = [None] * C_KV_HEADS
            for b in range(n_blk):
                g = b // blk_per_g
                cols = slice(b * blk_w, (b + 1) * blk_w)
                sc = cbuf_ref[bb, :, cols] + cbias
                mx = jnp.max(sc, axis=0, keepdims=True)
                e = jnp.exp2(sc - mx)
                den = jnp.sum(e, axis=0, keepdims=True)
                p = e * jnp.where(mx > 0.5 * NEG, 1.0 / den, 0.0)
                pcmp_ref[bb, :, cols] = p.astype(BF16)
                both = p[:, :CHUNK] + p[:, CHUNK:]
                psum[g] = both if psum[g] is None else psum[g] + both
            imps.append([_split_dot_left(ovt_ref[...], ps) for ps in psum])
        return imps

    def cmp_output():
        for bb in seqs:
            vct = vct_ref[bb].astype(BF16)
            for b in range(n_blk):
                g = b // blk_per_g
                cols = slice(b * blk_w, (b + 1) * blk_w)
                res_ref[bb, :, cols] = _dot(vct[g * HEAD_DIM:(g + 1) * HEAD_DIM, :], pcmp_ref[bb, :, cols])

    def select_blocks(all_imps):
        j = lax.broadcasted_iota(jnp.int32, (ns, CHUNK), 0)
        cur = (s0 + lax.broadcasted_iota(jnp.int32, (ns, CHUNK), 1)) // SEL_BLOCK
        forced = (j == 0) | (j == cur) | (j == cur - 1)
        future = j > cur
        slab_rows = lax.broadcasted_iota(jnp.int32, (SUBLANES, CHUNK), 0)
        for bb, g in [(bb, g) for bb in seqs for g in range(C_KV_HEADS)]:
            imps = all_imps[bb]
            imp = jnp.where(forced, BIG, jnp.where(future, NEG, imps[g]))
            slabs = [imp[v * SUBLANES:(v + 1) * SUBLANES, :] for v in range(ns // SUBLANES)]
            ranks = [jnp.zeros((SUBLANES, CHUNK), F32) for _ in slabs]
            for i2 in range(ns):
                r_i = imp[i2:i2 + 1, :]
                for v, slab in enumerate(slabs):
                    if (v + 1) * SUBLANES - 1 <= i2:
                        ranks[v] = ranks[v] + jnp.where(r_i > slab, 1.0, 0.0)
                    elif v * SUBLANES > i2:
                        ranks[v] = ranks[v] + jnp.where(r_i >= slab, 1.0, 0.0)
                    else:
                        tie = jnp.where(slab_rows > i2 - v * SUBLANES, 1.0, 0.0)
                        ranks[v] = ranks[v] + jnp.where(r_i > slab, 1.0, 0.0) + jnp.where(r_i == slab, tie, 0.0)
            rank = jnp.concatenate(ranks, axis=0)
            sel_bias = jnp.where(rank < n_sel, 0.0, NEG).astype(BF16)
            for r in range(C_REP):
                h = g * C_REP + r
                qb_ref[bb, 0:ns, h * CHUNK:(h + 1) * CHUNK] = sel_bias

    def scores(br, k_ref, c, tile, slot):
        off = pl.multiple_of(c * CHUNK, CHUNK)
        e_tile = e_ref[tile]
        for bb in seqs:
            kch = k_ref[bb, pl.ds(off, CHUNK), :]
            if br == 0:
                sbuf_ref[bb, 0, slot] = _dot(jnp.concatenate([kch, e_tile], axis=1),
                                             qb_ref[bb, C_KV_WIDTH:3 * C_KV_WIDTH, :])
            else:
                sbuf_ref[bb, 1, slot] = _dot(jnp.concatenate([e_tile, kch], axis=1),
                                             qb_ref[bb, 0:2 * C_KV_WIDTH, :])

    def softmax(br, slot, tri):
        for bb in seqs:
            for b in range(n_blk):
                cols = slice(b * blk_w, (b + 1) * blk_w)
                s = sbuf_ref[bb, br, slot, :, cols]
                if tri is not None:
                    s = s + tri
                m_old = m_ref[bb, br, :, cols]
                m_new = jnp.maximum(m_old, jnp.max(s, axis=0, keepdims=True))
                alpha_ref[bb, br, :, cols] = jnp.exp2(m_old - m_new)
                pbuf_ref[bb, br, :, cols] = jnp.exp2(s - m_new).astype(BF16)
                m_ref[bb, br, :, cols] = m_new

    def accumulate(br, vt_ref, c, first=False):
        off = pl.multiple_of(c * CHUNK, CHUNK)
        for bb in seqs:
            vt = vt_ref[bb, :, pl.ds(off, CHUNK)]
            vone = [jnp.concatenate([vt[g * HEAD_DIM:(g + 1) * HEAD_DIM, :], ones_tile], axis=0)
                    for g in range(C_KV_HEADS)]
            for b in range(n_blk):
                cols = slice(b * blk_w, (b + 1) * blk_w)
                pv = _dot(vone[b // blk_per_g], pbuf_ref[bb, br, :, cols])
                acc_ref[bb, br, :, cols] = (pv if first else
                                            alpha_ref[bb, br, :, cols] * acc_ref[bb, br, :, cols] + pv)

    n_back = WINDOW // CHUNK
    chunk =[jnp.maximum(i - n_back + w, 0) for w in range(n_back + 1)]
    tile = [jnp.where(i - n_back + w < 0, n_chunks + 1, n_chunks) for w in range(n_back + 1)]

    def win_stage(w):
        if w >= 1:
            accumulate(0, vwt_ref, chunk[w - 1], first=(w == 1))
        softmax(0, w % 2, tri_old if w == 0 else (tri_diag if w == n_back else None))
        if w < n_back:
            scores(0, kw_ref, chunk[w + 1], tile[w + 1], (w + 1) % 2)

    cmp_scores()
    m_ref[:, 0] = jnp.full((nb, 1, width), NEG, F32)
    scores(0, kw_ref, chunk[0], tile[0], 0)
    imps = cmp_softmax()
    win_stage(0)
    win_stage(1)
    select_blocks(imps)

    m_ref[:, 1] = jnp.full((nb, 1, width), NEG, F32)
    alpha_ref[:, 1] = jnp.ones((nb, 1, width), F32)
    for bb in seqs:
        acc_ref[bb, 1] = jnp.zeros(acc_ref.shape[2:], F32)
        pbuf_ref[bb, 1] = jnp.zeros(pbuf_ref.shape[2:], BF16)
    scores(1, ks_ref, 0, 0, 0)
    win_stage(2)

    def sel_stage(k, slot, slot_next):
        accumulate(1, vst_ref, jnp.maximum(k - 1, 0))
        softmax(1, slot, None)
        scores(1, ks_ref, k + 1, k + 1, slot_next)

    def sel_pair(kk, carry):
        sel_stage(2 * kk, 0, 1)
        sel_stage(2 * kk + 1, 1, 0)
        return carry

    lax.fori_loop(0, i // 2, sel_pair, 0)

    @pl.when(i % 2 == 1)
    def _():
        sel_stage(i - 1, 0, 1)

    accumulate(1, vst_ref, jnp.maximum(i - 1, 0))
    win_stage(3)
    softmax(1, i % 2, tri_diag)
    win_stage(4)
    accumulate(1, vst_ref, i)
    accumulate(0, vwt_ref, chunk[n_back])
    cmp_output()

    for bb in seqs:
        gates = jax.nn.sigmoid(gt_ref[bb])
        den = [acc_ref[bb, br, HEAD_DIM:HEAD_DIM + 1, :] for br in (0, 1)]
        inv = [jnp.where(x > 0.0, 1.0 / x, 0.0) for x in den]
        for pair in range(C_HEADS // 2):
            pieces = []
            for h in (2 * pair, 2 * pair + 1):
                cs = slice(h * CHUNK, (h + 1) * CHUNK)
                pieces.append(gates[3 * h:3 * h + 1, :] * res_ref[bb, :, cs]
                              + (gates[3 * h + 1:3 * h + 2, :] * inv[1][:, cs]) * acc_ref[bb, 1, 0:HEAD_DIM, cs]
                              + (gates[3 * h + 2:3 * h + 3, :] * inv[0][:, cs]) * acc_ref[bb, 0, 0:HEAD_DIM, cs])
            o_ref[bb, :, pair * 2 * HEAD_DIM:(pair + 1) * 2 * HEAD_DIM] = (
                jnp.concatenate(pieces, axis=0).T.astype(o_ref.dtype))


def _route_tiles(n_chunks, ns):
    e = np.zeros((n_chunks + 2, CHUNK, C_KV_WIDTH), np.float32)
    r = np.arange(CHUNK)
    for c in range(n_chunks):
        e[c, r, 2 * c + r // SEL_BLOCK] = 1.0
    e[n_chunks, :, 0] = 1.0
    e[n_chunks + 1, :, 1] = 1.0
    return jnp.asarray(e, dtype=BF16)


def _nsa(ptq, gt, ksw, kc, vct, ovt, route):
    bsz, s, _ = ksw.shape
    nseg = s // CMP_STRIDE
    ns = s // SEL_BLOCK
    n_chunks = s // CHUNK
    width = C_HEADS * CHUNK
    nb = 4 if bsz % 4 == 0 else (2 if bsz % 2 == 0 else 1)
    return pl.pallas_call(
        functools.partial(_nsa_kernel, nseg=nseg, ns=ns, n_sel=min(N_SEL, ns), n_chunks=n_chunks, nb=nb),
        grid=(bsz // nb, n_chunks),
        in_specs=[
            pl.BlockSpec((nb, C_WIDTH, CHUNK), lambda b, i: (b, PT_Q // C_WIDTH, i)),
            pl.BlockSpec((nb, gt.shape[1], CHUNK), lambda b, i: (b, 0, i)),
            pl.BlockSpec((nb, s, C_KV_WIDTH), lambda b, i: (b, 0, 0)),
            pl.BlockSpec((nb, C_KV_WIDTH, s), lambda b, i: (b, PT_VS // C_KV_WIDTH, 0)),
            pl.BlockSpec((nb, s, C_KV_WIDTH), lambda b, i: (b, 0, 1)),
            pl.BlockSpec((nb, C_KV_WIDTH, s), lambda b, i: (b, PT_VW // C_KV_WIDTH, 0)),
            pl.BlockSpec((nb, nseg, C_KV_WIDTH), lambda b, i: (b, 0, 0)),
            pl.BlockSpec((nb, C_KV_WIDTH, nseg), lambda b, i: (b, 0, 0)),
            pl.BlockSpec(ovt.shape, lambda b, i: (0, 0)),
            pl.BlockSpec(route.shape, lambda b, i: (0, 0, 0)),
        ],
        out_specs=pl.BlockSpec((nb, CHUNK, C_WIDTH), lambda b, i: (b, i, 0)),
        out_shape=jax.ShapeDtypeStruct((bsz, s, C_WIDTH), BF16),
        scratch_shapes=[
            pltpu.VMEM((nb, 3 * C_KV_WIDTH, width), BF16),
            pltpu.VMEM((nb, 2, 1, width), F32),
            pltpu.VMEM((nb, 2, 1, width), F32),
            pltpu.VMEM((nb, 2, HEAD_DIM + 2 * SUBLANES, width), F32),
            pltpu.VMEM((nb, HEAD_DIM, width), F32),
            pltpu.VMEM((nb, 2, 2, CHUNK, width), F32),
            pltpu.VMEM((nb, 2, CHUNK, width), BF16),
            pltpu.VMEM((nb, nseg, width), F32),
            pltpu.VMEM((nb, nseg, width), BF16),
        ],
        compiler_params=_params(("arbitrary", "arbitrary"), 48),
        name="nsa_attention",
    )(ptq, gt, ksw, ptq, ksw, ptq, kc, vct, ovt, route)


def _tail_kernel(ya_ref, yb_ref, yc_ref, x_ref, mod_ref, wo_ref, g1_ref, b1_ref,
                 wup_ref, cw_ref, cb_ref, wdn_ref, g2_ref, b2_ref, o_ref,
                 work_ref, carry_ref, x1_ref, h_ref, act_ref, *, d, dff, cwid, tm, alpha):
    @pl.when(pl.program_id(1) == 0)
    def _():
        carry_ref[...] = jnp.zeros_like(carry_ref)

    y1 = _dot(ya_ref[0], wo_ref[0:A_WIDTH, :])
    y1 += _dot(yb_ref[0], wo_ref[A_WIDTH:A_WIDTH + B_WIDTH, :])
    y1 += _dot(yc_ref[0], wo_ref[A_WIDTH + B_WIDTH:, :])
    x1 = _row_ln(alpha * x_ref[0] + mod_ref[0, :, 2 * d:3 * d] * y1, g1_ref[...], b1_ref[...])
    x1_ref[...] = x1

    sh = mod_ref[0, :, 3 * d:4 * d]
    sc = mod_ref[0, :, 4 * d:5 * d]
    gate = mod_ref[0, :, 5 * d:6 * d]
    h_ref[...] = (x1 * (1.0 + sc) + sh).astype(BF16)
    pad = SUBLANES
    n_chunks = dff // cwid

    def up(ci):
        slot = ci % 2
        for part in range(2):
            cs = slice(part * dff + ci * cwid, part * dff + (ci + 1) * cwid)
            work_ref[slot, part, 0:pad, :] = carry_ref[:, cs]
            work_ref[slot, part, pad:pad + tm, :] = _dot(h_ref[...], wup_ref[:, cs])

    def conv_act(ci):
        slot = ci % 2
        conv = []
        for part in range(2):
            cs = slice(part * dff + ci * cwid, part * dff + (ci + 1) * cwid)
            buf = work_ref[slot, part]
            carry_ref[:, cs] = buf[tm:tm + pad, :]
            a = buf[pad:, :]
            a1 = pltpu.roll(buf, 1, 0)[pad:, :]
            a2 = pltpu.roll(buf, 2, 0)[pad:, :]
            conv.append(cw_ref[0:1, cs] * a2 + cw_ref[1:2, cs] * a1 + cw_ref[2:3, cs] * a + cb_ref[:, cs])
        half = 0.5 * conv[0]
        silu = half + half * jnp.tanh(half)
        act_ref[:, ci * cwid:(ci + 1) * cwid] = (silu * conv[1]).astype(BF16)

    split = (n_chunks // 2) * cwid
    up(0)
    y = None
    for ci in range(n_chunks):
        if ci + 1 < n_chunks:
            up(ci + 1)
        conv_act(ci)
        if (ci + 1) * cwid == split:
            y = _dot(act_ref[:, :split], wdn_ref[:split, :])
    y = y + _dot(act_ref[:, split:], wdn_ref[split:, :])
    o_ref[0] = _row_ln(alpha * x1_ref[...] + gate * y, g2_ref[...], b2_ref[...])


def _tail(ya, yb, yc, x, mod, layer, wo, g1, b1, wup, cw, cb, wdn, g2, b2, alpha, tm=512, cwid=256):
    bsz, s, d = x.shape
    dff = wdn.shape[1]
    const = lambda bi, j: (0, 0)
    tile = lambda w: pl.BlockSpec((1, tm, w), lambda bi, j: (bi, j, 0))
    weight = lambda w: pl.BlockSpec((None,) + w.shape[1:], lambda bi, j: (layer, 0, 0),
                                    pipeline_mode=pl.Buffered(1))
    small = lambda a: pl.BlockSpec(a.shape, const)
    return pl.pallas_call(
        functools.partial(_tail_kernel, d=d, dff=dff, cwid=cwid, tm=tm, alpha=alpha),
        grid=(bsz, s // tm),
        in_specs=[
            tile(A_WIDTH), tile(B_WIDTH), tile(C_WIDTH), tile(d),
            pl.BlockSpec((1, 1, 6 * d), lambda bi, j: (bi, 0, 0)),
            weight(wo), small(g1), small(b1),
            weight(wup), small(cw), small(cb), weight(wdn), small(g2), small(b2),
        ],
        out_specs=tile(d),
        out_shape=jax.ShapeDtypeStruct((bsz, s, d), F32),
        scratch_shapes=[
            pltpu.VMEM((2, 2, tm + SUBLANES, cwid), F32),
            pltpu.VMEM((SUBLANES, 2 * dff), F32),
            pltpu.VMEM((tm, d), F32),
            pltpu.VMEM((tm, d), BF16),
            pltpu.VMEM((tm, dff), BF16),
        ],
        compiler_params=_params(("arbitrary", "arbitrary"), 52),
        name="layer_tail",
    )(ya, yb, yc, x, mod, wo, g1, b1, wup, cw, cb, wdn, g2, b2)


def _inproj_weights(w):
    cols = np.arange(B_WIDTH)
    half = cols // (B_WIDTH // 2)
    perm = ((cols % (B_WIDTH // 2)) // (HEAD_DIM // 2)) * HEAD_DIM + half * (HEAD_DIM // 2) + cols % (HEAD_DIM // 2)
    za = w[:, 0:512]
    qb, kb, vb, gb = (w[:, 512 + i * B_WIDTH:512 + (i + 1) * B_WIDTH] for i in range(4))
    qc = w[:, 1536:2048]
    kcm, vcm, ksl, vsl, kwn, vwn = (w[:, 2048 + i * C_KV_WIDTH:2048 + (i + 1) * C_KV_WIDTH] for i in range(6))
    gc = w[:, 2816:2840]
    wn = jnp.concatenate([qb[:, perm], kb[:, perm], vb, gb, za, ksl, kwn, kcm, vcm], axis=1)
    pad = jnp.zeros((w.shape[0], PT_ROWS - PT_G - gc.shape[1]), w.dtype)
    wt = jnp.concatenate([qc * (HEAD_DIM ** -0.5 * LOG2E), vsl, vwn, gc, pad], axis=1).T
    return wn.astype(BF16), wt.astype(BF16)


def kernel(x, c, w_ada, b_ada, w_in, a_ln_g, a_ln_b, a_ws, a_bs, b_gn_g, b_gn_b, c_pos_k, c_w1_k, c_w2_k,
           c_pos_v, c_w1_v, c_w2_v, w_out, ln1_g, ln1_b, w_up, conv_w, conv_b, w_down, ln2_g, ln2_b):
    depth = w_in.shape[0]
    bsz, s, d = x.shape
    alpha = (2 * depth) ** 0.25
    nseg = s // CMP_STRIDE
    ns = s // SEL_BLOCK

    lanes = np.arange(A_WIDTH)
    mavg = jnp.asarray((lanes[:, None] // HEAD_DIM == lanes[None, :] // HEAD_DIM) / HEAD_DIM, dtype=BF16)
    gm = jnp.asarray((lanes[None, :] // HEAD_DIM == np.arange(A_GROUPS)[:, None]).astype(np.float32))
    ret_consts = _retention_consts()
    cos, sin = _rotary_tables(s)
    ovt = _overlap_t(nseg, ns)
    route = _route_tiles(s // CHUNK, ns)

    wo_bf, wup_bf, wdn_bf = w_out.astype(BF16), w_up.astype(BF16), w_down.astype(BF16)
    mods = _ada_mod(c, w_ada, b_ada)
    for l in range(depth):
        mod = mods[l][:, None, :]
        wn, wt = _inproj_weights(w_in[l])
        wcat = jnp.transpose(a_ws[l], (1, 0, 2)).reshape(CHUNK, A_GROUPS * CHUNK)
        bias = jnp.repeat(a_bs[l].T, HEAD_DIM, axis=1)
        ya, yb, ksw, kvc, ptq, gt = _inproj(
            x, mod, wn, wt, cos, sin, ret_consts, mavg, b_gn_g[l].reshape(1, B_WIDTH), b_gn_b[l].reshape(1, B_WIDTH),
            wcat, bias, a_ln_g[l].reshape(1, A_WIDTH), a_ln_b[l].reshape(1, A_WIDTH), gm)

        w1k = _blockdiag2(c_w1_k[l].reshape(CMP_BLOCK, HEAD_DIM, HEAD_DIM)).astype(BF16)
        w1v = _blockdiag2(c_w1_v[l].reshape(CMP_BLOCK, HEAD_DIM, HEAD_DIM)).astype(BF16)
        kc, vct = _compress(kvc, jnp.tile(c_pos_k[l], (1, C_KV_HEADS)), jnp.tile(c_pos_v[l], (1, C_KV_HEADS)),
                            w1k, w1v, _blockdiag2(c_w2_k[l]).astype(BF16), _blockdiag2(c_w2_v[l]).T.astype(BF16))
        yc = _nsa(ptq, gt, ksw, kc, vct, ovt, route)

        x = _tail(ya, yb, yc, x, mod, l, wo_bf, ln1_g[l].reshape(1, d), ln1_b[l].reshape(1, d),
                  wup_bf, conv_w[l], conv_b[l].reshape(1, -1), wdn_bf,
                  ln2_g[l].reshape(1, d), ln2_b[l].reshape(1, d), alpha)
    return x
```

```python
import functools

import numpy as np
import jax
import jax.numpy as jnp
from jax import lax
from jax.experimental import pallas as pl
from jax.experimental.pallas import tpu as pltpu

F32 = jnp.float32
BF16 = jnp.bfloat16

HEAD_DIM = 64
A_GROUPS = 4
A_WIDTH = A_GROUPS * HEAD_DIM
CHUNK = 128
B_HEADS = 4
B_WIDTH = B_HEADS * HEAD_DIM
ROPE_BASE = 10000.0
C_HEADS = 8
C_KV_HEADS = 2
C_REP = C_HEADS // C_KV_HEADS
C_WIDTH = C_HEADS * HEAD_DIM
C_KV_WIDTH = C_KV_HEADS * HEAD_DIM
CMP_BLOCK = 32
CMP_STRIDE = 16
SEL_BLOCK = 64
N_SEL = 8
WINDOW = 512
CONV_WIDTH = 3
LN_EPS = 1e-5
NEG = -1e30
BIG = 1e30
LOG2E = 1.4426950408889634

LANES = 128
SUBLANES = 8
VMEM_BYTES = 64 * 1024 * 1024

PN_RET, PN_ZA, PN_KS, PN_KW, PN_KVC = 0, 1024, 1536, 1664, 1792
PN_COLS = 1536
PT_Q, PT_VS, PT_VW, PT_G = 0, 512, 640, 768
PT_ROWS = 800


def _dot(a, b):
    return jnp.dot(a, b, preferred_element_type=F32)


def _dot_nt(a, b):
    return lax.dot_general(a, b, (((1,), (1,)), ((), ())), preferred_element_type=F32)


def _dot_tn(a, b):
    return lax.dot_general(a, b, (((0,), (0,)), ((), ())), preferred_element_type=F32)


def _split_dot(x, m):
    hi = x.astype(BF16)
    lo = (x - hi.astype(F32)).astype(BF16)
    return _dot(hi, m) + _dot(lo, m)


def _split_dot_left(m, x):
    hi = x.astype(BF16)
    lo = (x - hi.astype(F32)).astype(BF16)
    return _dot(m, hi) + _dot(m, lo)


def _group_ln(x, mavg, g, b):
    mu = _split_dot(x, mavg)
    d = x - mu
    var = _split_dot(d * d, mavg)
    return d * lax.rsqrt(var + LN_EPS) * g + b


def _row_ln(x, g, b):
    mu = jnp.mean(x, axis=-1, keepdims=True)
    d = x - mu
    var = jnp.mean(d * d, axis=-1, keepdims=True)
    return d * lax.rsqrt(var + LN_EPS) * g + b


def _params(sem, vmem_mb):
    return pltpu.CompilerParams(dimension_semantics=sem, vmem_limit_bytes=vmem_mb * 1024 * 1024)


def _ada_kernel(c_ref, w_ref, b_ref, o_ref):
    cond = jax.nn.silu(c_ref[...]).astype(BF16)
    o_ref[0] = _dot(cond, w_ref[0].astype(BF16)) + b_ref[0]


def _ada_mod(c, w_ada, b_ada):
    depth, d, n = w_ada.shape
    bsz = c.shape[0]
    tn = 1536
    return pl.pallas_call(
        _ada_kernel,
        grid=(depth, n // tn),
        in_specs=[
            pl.BlockSpec((bsz, d), lambda l, j: (0, 0)),
            pl.BlockSpec((1, d, tn), lambda l, j: (l, 0, j)),
            pl.BlockSpec((1, 1, tn), lambda l, j: (l, 0, j)),
        ],
        out_specs=pl.BlockSpec((1, bsz, tn), lambda l, j: (l, 0, j)),
        out_shape=jax.ShapeDtypeStruct((depth, bsz, n), F32),
        compiler_params=_params(("arbitrary", "arbitrary"), 40),
        name="ada_mod",
    )(c, w_ada, b_ada.reshape(depth, 1, n))


def _inproj_kernel(x_ref, mod_ref, wn_ref, wt_ref, cos_ref, sin_ref, dec_ref, zeta_ref, xi_ref, cd_ref, qm_ref,
                   vm_ref, bm_ref, mavg_ref, gng_ref, gnb_ref, wcat_ref, abias_ref, alg_ref, alb_ref, gm_ref,
                   wo32_ref, wup32_ref, wdn32_ref,
                   ya_ref, yb_ref, ksw_ref, kvc_ref, ptq_ref, gt_ref, wo16_ref, wup16_ref, wdn16_ref,
                   pn_ref, state_ref, *, d, n_chunks, wdn_steps):
    @pl.when(pl.program_id(1) == 0)
    def _():
        state_ref[...] = jnp.zeros_like(state_ref)

    wo16_ref[...] = wo32_ref[...].astype(BF16)
    wup16_ref[...] = wup32_ref[...].astype(BF16)

    @pl.when(pl.program_id(0) * pl.num_programs(1) + pl.program_id(1) < wdn_steps)
    def _():
        wdn16_ref[...] = wdn32_ref[...].astype(BF16)

    sh = mod_ref[0, :, 0:d]
    sc = mod_ref[0, :, d:2 * d]
    h = (x_ref[0] * (1.0 + sc) + sh).astype(BF16)
    pn_ref[...] = _dot(h, wn_ref[:, :PN_COLS])
    src = lambda rows, lo, hi: pn_ref[rows, lo:hi]
    rest = _dot(h, wn_ref[:, PN_COLS:])
    ksw_ref[0] = rest[:, :PN_KVC - PN_COLS].astype(BF16)
    kvc_ref[0] = rest[:, PN_KVC - PN_COLS:]
    _retention_chunks(src, cos_ref, sin_ref, dec_ref, zeta_ref, xi_ref, cd_ref, qm_ref, vm_ref, bm_ref,
                      mavg_ref, gng_ref, gnb_ref, yb_ref, state_ref, n_chunks)
    pt = _dot_nt(wt_ref[...], h)
    ptq_ref[0] = pt[:PT_G, :].astype(BF16)
    gt_ref[0] = pt[PT_G:, :]
    _gmlp_chunks(src, wcat_ref, abias_ref, alg_ref, alb_ref, mavg_ref, gm_ref, ya_ref, n_chunks)


def _inproj(x, mod, wn, wt, cos, sin, ret_consts, mavg, gn_g, gn_b, wcat, abias, al_g, al_b, gm,
            layer, w_out, w_up, w_down, tm=512):
    bsz, s, d = x.shape
    nt = s // tm
    steps = bsz * nt
    dff = w_down.shape[1]
    bf16_rows = 2 * SUBLANES
    slab = d // steps
    assert d % steps == 0 and slab % bf16_rows == 0
    wdn_steps = max(k for k in range(1, steps + 1) if dff % k == 0 and (dff // k) % bf16_rows == 0)
    wdn_slab = dff // wdn_steps
    const = lambda b, j: (0, 0)
    small = lambda a: pl.BlockSpec(a.shape, const)
    rows = lambda w: pl.BlockSpec((1, tm, w), lambda b, j: (b, j, 0))
    step = lambda b, j: b * nt + j
    return pl.pallas_call(
        functools.partial(_inproj_kernel, d=d, n_chunks=tm // CHUNK, wdn_steps=wdn_steps),
        grid=(bsz, nt),
        in_specs=[
            rows(d),
            pl.BlockSpec((1, 1, 6 * d), lambda b, j: (b, 0, 0)),
            small(wn), small(wt),
            pl.BlockSpec((tm, B_WIDTH // 2), lambda b, j: (j, 0)),
            pl.BlockSpec((tm, B_WIDTH // 2), lambda b, j: (j, 0)),
        ] + [small(a) for a in ret_consts] + [
            small(mavg), small(gn_g), small(gn_b), small(wcat), small(abias), small(al_g), small(al_b), small(gm),
            pl.BlockSpec((None, slab, d), lambda b, j: (layer, step(b, j), 0)),
            pl.BlockSpec((None, slab, w_up.shape[2]), lambda b, j: (layer, step(b, j), 0)),
            pl.BlockSpec((None, wdn_slab, d), lambda b, j: (layer, jnp.minimum(step(b, j), wdn_steps - 1), 0)),
        ],
        out_specs=[
            rows(A_WIDTH), rows(B_WIDTH), rows(2 * C_KV_WIDTH), rows(2 * C_KV_WIDTH),
            pl.BlockSpec((1, PT_G, tm), lambda b, j: (b, 0, j)),
            pl.BlockSpec((1, PT_ROWS - PT_G, tm), lambda b, j: (b, 0, j)),
            pl.BlockSpec((slab, d), lambda b, j: (step(b, j), 0)),
            pl.BlockSpec((slab, w_up.shape[2]), lambda b, j: (step(b, j), 0)),
            pl.BlockSpec((wdn_slab, d), lambda b, j: (jnp.minimum(step(b, j), wdn_steps - 1), 0)),
        ],
        out_shape=[
            jax.ShapeDtypeStruct((bsz, s, A_WIDTH), BF16),
            jax.ShapeDtypeStruct((bsz, s, B_WIDTH), BF16),
            jax.ShapeDtypeStruct((bsz, s, 2 * C_KV_WIDTH), BF16),
            jax.ShapeDtypeStruct((bsz, s, 2 * C_KV_WIDTH), F32),
            jax.ShapeDtypeStruct((bsz, PT_G, s), BF16),
            jax.ShapeDtypeStruct((bsz, PT_ROWS - PT_G, s), F32),
            jax.ShapeDtypeStruct(w_out.shape[1:], BF16),
            jax.ShapeDtypeStruct(w_up.shape[1:], BF16),
            jax.ShapeDtypeStruct(w_down.shape[1:], BF16),
        ],
        scratch_shapes=[
            pltpu.VMEM((tm, PN_COLS), F32),
            pltpu.VMEM((B_WIDTH, B_WIDTH), F32),
        ],
        compiler_params=_params(("arbitrary", "arbitrary"), 48),
        name="inproj_mixers",
    )(x, mod, wn, wt, cos, sin, *ret_consts, mavg, gn_g, gn_b, wcat, abias, al_g, al_b, gm, w_out, w_up, w_down)


def _gmlp_chunks(src, w_ref, bias_ref, g_ref, b_ref, mavg_ref, gm_ref, o_ref, n_chunks):
    row = lax.broadcasted_iota(jnp.int32, (CHUNK, A_GROUPS * CHUNK), 0)
    col = lax.broadcasted_iota(jnp.int32, (CHUNK, A_GROUPS * CHUNK), 1)
    wc = jnp.where((col % CHUNK) <= row, w_ref[...], 0.0).astype(BF16)
    chunks = range(n_chunks)
    rows = [slice(c * CHUNK, (c + 1) * CHUNK) for c in chunks]
    z = [jax.nn.gelu(src(r, PN_ZA, PN_ZA + 2 * A_WIDTH)) for r in rows]
    v = [zc[:, A_WIDTH:] for zc in z]
    mavg = mavg_ref[...]
    mu = [_split_dot(vc, mavg) for vc in v]
    dev = [vc - m for vc, m in zip(v, mu)]
    var = [_split_dot(d * d, mavg) for d in dev]
    vn = [d * lax.rsqrt(s2 + LN_EPS) * g_ref[...] + b_ref[...] for d, s2 in zip(dev, var)]
    vstack = [jnp.concatenate([x * gm_ref[g:g + 1, :] for g in range(A_GROUPS)], axis=0).astype(BF16) for x in vn]
    vs = [_dot(wc, x) + bias_ref[...] for x in vstack]
    for c in chunks:
        o_ref[0, rows[c], :] = (z[c][:, :A_WIDTH] * vs[c]).astype(o_ref.dtype)


def _retention_chunks(src, cos_ref, sin_ref, dec_ref, zeta_ref, xi_ref, cd_ref, qm_ref, vm_ref, bm_ref,
                      mavg_ref, g_ref, b_ref, o_ref, state_ref, n_chunks):
    half = B_WIDTH // 2
    chunks = range(n_chunks)
    rows = [slice(c * CHUNK, (c + 1) * CHUNK) for c in chunks]

    def rot(t, r):
        t1 = t[:, :half]
        t2 = t[:, half:]
        cos = cos_ref[r, :]
        sin = sin_ref[r, :]
        return jnp.concatenate([t1 * cos - t2 * sin, t1 * sin + t2 * cos], axis=1)

    qr = [rot(src(r, 0, B_WIDTH), r) for r in rows]
    kr = [rot(src(r, B_WIDTH, 2 * B_WIDTH), r) * (HEAD_DIM ** -0.5) for r in rows]
    v = [src(r, 2 * B_WIDTH, 3 * B_WIDTH) for r in rows]
    qs = [jnp.concatenate([x * qm_ref[h:h + 1, :] for h in range(B_HEADS)], axis=0).astype(BF16) for x in qr]
    s = [_dot_nt(a, b.astype(BF16)) * dec_ref[...] for a, b in zip(qs, kr)]
    kv = [_dot_tn((a * zeta_ref[...]).astype(BF16), b.astype(BF16)) * bm_ref[...]
          for a, b in zip(kr, v)]
    scat = [jnp.concatenate([x[h * CHUNK:(h + 1) * CHUNK, :] for h in range(B_HEADS)], axis=1).astype(BF16)
            for x in s]
    vstack = [jnp.concatenate([x * vm_ref[h:h + 1, :] for h in range(B_HEADS)], axis=0).astype(BF16) for x in v]
    o_inner = [_dot(a, b) for a, b in zip(scat, vstack)]
    state = state_ref[...]
    before = []
    for c in chunks:
        before.append(state)
        state = state * cd_ref[...] + kv[c]
    state_ref[...] = state
    o = [oi + _dot(a.astype(BF16), st.astype(BF16)) * xi_ref[...] for oi, a, st in zip(o_inner, qr, before)]
    mavg = mavg_ref[...]
    mu = [_split_dot(x, mavg) for x in o]
    dev = [x - m for x, m in zip(o, mu)]
    var = [_split_dot(d * d, mavg) for d in dev]
    for c in chunks:
        normed = dev[c] * lax.rsqrt(var[c] + LN_EPS) * g_ref[...] + b_ref[...]
        gate = src(rows[c], 3 * B_WIDTH, 4 * B_WIDTH)
        o_ref[0, rows[c], :] = (jax.nn.silu(gate) * normed).astype(o_ref.dtype)


def _retention_consts():
    h_n, d, l_n = B_HEADS, HEAD_DIM, CHUNK
    log_gamma = jnp.log1p(-jnp.exp2(-5.0 - jnp.arange(h_n, dtype=F32)))
    idx = jnp.arange(l_n, dtype=F32)
    diff = idx[:, None] - idx[None, :]
    decay_in = jnp.where(diff >= 0, jnp.exp(log_gamma[:, None, None] * jnp.maximum(diff, 0.0)), 0.0)
    xi = jnp.exp(log_gamma[:, None] * (idx + 1.0))
    zeta = jnp.exp(log_gamma[:, None] * (l_n - 1.0 - idx))
    chunk_decay = jnp.exp(log_gamma * l_n)
    cols = np.arange(B_WIDTH)
    head_perm = (cols % (B_WIDTH // 2)) // (d // 2)
    head_std = cols // d
    dec = decay_in.reshape(h_n * l_n, l_n)
    zeta_t = zeta.T[:, head_perm]
    xi_t = xi.T[:, head_std]
    cd = chunk_decay[head_std][None, :]
    qm = jnp.asarray((head_perm[None, :] == np.arange(h_n)[:, None]).astype(np.float32))
    vm = jnp.asarray((head_std[None, :] == np.arange(h_n)[:, None]).astype(np.float32))
    bm = jnp.asarray((head_perm[:, None] == head_std[None, :]).astype(np.float32))
    return [dec, zeta_t, xi_t, cd, qm, vm, bm]


def _rotary_tables(s):
    half = HEAD_DIM // 2
    inv = jnp.power(ROPE_BASE, -jnp.arange(half, dtype=F32) / half)
    ang = jnp.arange(s).astype(F32)[:, None] * inv[None, :]
    return jnp.tile(jnp.cos(ang), (1, B_HEADS)), jnp.tile(jnp.sin(ang), (1, B_HEADS))


def _compress_kernel(xk_ref, xv_ref, posk_ref, posv_ref, w1k_ref, w1v_ref, w2k_ref, w2vt_ref, kc_ref, vct_ref,
                     *, nseg):
    half = CMP_BLOCK // 2
    acc = [jnp.zeros((nseg, C_KV_WIDTH), F32) for _ in range(4)]
    for l in range(half):
        xk = xk_ref[0, pl.ds(l, nseg, stride=CMP_STRIDE), :]
        xv = xv_ref[0, pl.ds(l, nseg, stride=CMP_STRIDE), :]
        acc[0] += _dot((xk + posk_ref[l:l + 1, :]).astype(BF16), w1k_ref[l])
        acc[1] += _dot((xk + posk_ref[half + l:half + l + 1, :]).astype(BF16), w1k_ref[half + l])
        acc[2] += _dot((xv + posv_ref[l:l + 1, :]).astype(BF16), w1v_ref[l])
        acc[3] += _dot((xv + posv_ref[half + l:half + l + 1, :]).astype(BF16), w1v_ref[half + l])
    hk = jax.nn.gelu(acc[0] + pltpu.roll(acc[1], nseg - 1, 0))
    hv = jax.nn.gelu(acc[2] + pltpu.roll(acc[3], nseg - 1, 0))
    kc_ref[0] = _dot(hk.astype(BF16), w2k_ref[...])
    vct_ref[0] = _dot_nt(w2vt_ref[...], hv.astype(BF16))


def _compress(kvc, posk, posv, w1k, w1v, w2k, w2v):
    bsz, s, width = kvc.shape
    nseg = s // CMP_STRIDE
    c2 = lambda bi: (0, 0)
    c3 = lambda bi: (0, 0, 0)
    return pl.pallas_call(
        functools.partial(_compress_kernel, nseg=nseg),
        grid=(bsz,),
        in_specs=[
            pl.BlockSpec((1, s, C_KV_WIDTH), lambda bi: (bi, 0, 0)),
            pl.BlockSpec((1, s, C_KV_WIDTH), lambda bi: (bi, 0, 1)),
            pl.BlockSpec(posk.shape, c2),
            pl.BlockSpec(posv.shape, c2),
            pl.BlockSpec(w1k.shape, c3),
            pl.BlockSpec(w1v.shape, c3),
            pl.BlockSpec(w2k.shape, c2),
            pl.BlockSpec(w2v.shape, c2),
        ],
        out_specs=[
            pl.BlockSpec((1, nseg, C_KV_WIDTH), lambda bi: (bi, 0, 0)),
            pl.BlockSpec((1, C_KV_WIDTH, nseg), lambda bi: (bi, 0, 0)),
        ],
        out_shape=[
            jax.ShapeDtypeStruct((bsz, nseg, C_KV_WIDTH), F32),
            jax.ShapeDtypeStruct((bsz, C_KV_WIDTH, nseg), F32),
        ],
        compiler_params=_params(("arbitrary",), 32),
        name="nsa_compress",
    )(kvc, kvc, posk, posv, w1k, w1v, w2k, w2v)


def _blockdiag2(w):
    z = jnp.zeros_like(w)
    return jnp.concatenate([jnp.concatenate([w, z], axis=-1), jnp.concatenate([z, w], axis=-1)], axis=-2)


def _overlap_t(nseg, ns):
    nc = nseg - 1
    c0 = np.arange(nc)[None, :] * CMP_STRIDE
    s0 = np.arange(ns)[:, None] * SEL_BLOCK
    ov = np.clip(np.minimum(c0 + CMP_BLOCK, s0 + SEL_BLOCK) - np.maximum(c0, s0), 0, None) / CMP_BLOCK
    out = np.zeros((ns, nseg), np.float32)
    out[:, :nc] = ov
    return jnp.asarray(out, dtype=BF16)


def _nsa_kernel(qt_ref, gt_ref, ks_ref, vst_ref, kw_ref, vwt_ref, kc_ref, vct_ref, ovt_ref, e_ref, o_ref,
                qb_ref, m_ref, alpha_ref, acc_ref, res_ref, sbuf_ref, pbuf_ref,
                cbuf_ref, pcmp_ref, *, nseg, ns, n_sel, n_chunks, nb):
    i = pl.program_id(1)
    s0 = i * CHUNK
    width = C_HEADS * CHUNK
    blk_w = 2 * CHUNK
    n_blk = C_HEADS // 2
    blk_per_g = n_blk // C_KV_HEADS
    seqs = range(nb)

    @pl.when((pl.program_id(0) == 0) & (i == 0))
    def _():
        brow = lax.broadcasted_iota(jnp.int32, (C_KV_WIDTH, width), 0)
        for bb in seqs:
            qb_ref[bb, 0:C_KV_WIDTH, :] = jnp.zeros((C_KV_WIDTH, width), BF16)
            qb_ref[bb, 2 * C_KV_WIDTH:, :] = jnp.where(brow == 1, NEG, 0.0).astype(BF16)

    zero = jnp.zeros((HEAD_DIM, CHUNK), BF16)
    for bb in seqs:
        for h in range(C_HEADS):
            qh = qt_ref[bb, h * HEAD_DIM:(h + 1) * HEAD_DIM, :]
            blk = jnp.concatenate([qh, zero] if h < C_REP else [zero, qh], axis=0)
            qb_ref[bb, C_KV_WIDTH:2 * C_KV_WIDTH, h * CHUNK:(h + 1) * CHUNK] = blk
    ones_tile = jnp.where(lax.broadcasted_iota(jnp.int32, (2 * SUBLANES, CHUNK), 0) == 0, 1.0, 0.0).astype(BF16)

    krow = lax.broadcasted_iota(jnp.int32, (CHUNK, blk_w), 0)
    t_loc = lax.broadcasted_iota(jnp.int32, (CHUNK, blk_w), 1) % CHUNK
    tri_diag = jnp.where(krow <= t_loc, 0.0, NEG)
    tri_old = jnp.where(krow > t_loc, 0.0, NEG)

    def cmp_scores():
        for bb in seqs:
            cbuf_ref[bb] = _dot(kc_ref[bb].astype(BF16), qb_ref[bb, C_KV_WIDTH:2 * C_KV_WIDTH, :])

    def cmp_softmax():
        crow = lax.broadcasted_iota(jnp.int32, (nseg, blk_w), 0)
        c_t = s0 + lax.broadcasted_iota(jnp.int32, (nseg, blk_w), 1) % CHUNK
        cbias = jnp.where(crow * CMP_STRIDE + (CMP_BLOCK - 1) <= c_t, 0.0, NEG)
        imps = []
        for bb in seqs:
            psum = [None] * C_KV_HEADS
            for b in range(n_blk):
                g = b // blk_per_g
                cols = slice(b * blk_w, (b + 1) * blk_w)
                sc = cbuf_ref[bb, :, cols] + cbias
                mx = jnp.max(sc, axis=0, keepdims=True)
                e = jnp.exp2(sc - mx)
                den = jnp.sum(e, axis=0, keepdims=True)
                p = e * jnp.where(mx > 0.5 * NEG, 1.0 / den, 0.0)
                pcmp_ref[bb, :, cols] = p.astype(BF16)
                both = p[:, :CHUNK] + p[:, CHUNK:]
                psum[g] = both if psum[g] is None else psum[g] + both
            imps.append([_split_dot_left(ovt_ref[...], ps) for ps in psum])
        return imps

    def cmp_output():
        for bb in seqs:
            vct = vct_ref[bb].astype(BF16)
            for b in range(n_blk):
                g = b // blk_per_g
                cols = slice(b * blk_w, (b + 1) * blk_w)
                res_ref[bb, :, cols] = _dot(vct[g * HEAD_DIM:(g + 1) * HEAD_DIM, :], pcmp_ref[bb, :, cols])

    def select_blocks(all_imps):
        j = lax.broadcasted_iota(jnp.int32, (ns, CHUNK), 0)
        cur = (s0 + lax.broadcasted_iota(jnp.int32, (ns, CHUNK), 1)) // SEL_BLOCK
        forced = (j == 0) | (j == cur) | (j == cur - 1)
        future = j > cur
        slab_rows = lax.broadcasted_iota(jnp.int32, (SUBLANES, CHUNK), 0)
        for bb, g in [(bb, g) for bb in seqs for g in range(C_KV_HEADS)]:
            imps = all_imps[bb]
            imp = jnp.where(forced, BIG, jnp.where(future, NEG, imps[g]))
            slabs = [imp[v * SUBLANES:(v + 1) * SUBLANES, :] for v in range(ns // SUBLANES)]
            ranks = [jnp.zeros((SUBLANES, CHUNK), F32) for _ in slabs]
            for i2 in range(ns):
                r_i = imp[i2:i2 + 1, :]
                for v, slab in enumerate(slabs):
                    if (v + 1) * SUBLANES - 1 <= i2:
                        ranks[v] = ranks[v] + jnp.where(r_i > slab, 1.0, 0.0)
                    elif v * SUBLANES > i2:
                        ranks[v] = ranks[v] + jnp.where(r_i >= slab, 1.0, 0.0)
                    else:
                        tie = jnp.where(slab_rows > i2 - v * SUBLANES, 1.0, 0.0)
                        ranks[v] = ranks[v] + jnp.where(r_i > slab, 1.0, 0.0) + jnp.where(r_i == slab, tie, 0.0)
            rank = jnp.concatenate(ranks, axis=0)
            sel_bias = jnp.where(rank < n_sel, 0.0, NEG).astype(BF16)
            for r in range(C_REP):
                h = g * C_REP + r
                qb_ref[bb, 0:ns, h * CHUNK:(h + 1) * CHUNK] = sel_bias

    def scores(br, k_ref, c, tile, slot):
        off = pl.multiple_of(c * CHUNK, CHUNK)
        e_tile = e_ref[tile]
        for bb in seqs:
            kch = k_ref[bb, pl.ds(off, CHUNK), :]
            if br == 0:
                sbuf_ref[bb, 0, slot] = _dot(jnp.concatenate([kch, e_tile], axis=1),
                                             qb_ref[bb, C_KV_WIDTH:3 * C_KV_WIDTH, :])
            else:
                sbuf_ref[bb, 1, slot] = _dot(jnp.concatenate([e_tile, kch], axis=1),
                                             qb_ref[bb, 0:2 * C_KV_WIDTH, :])

    def softmax(br, slot, tri):
        for bb in seqs:
            for b in range(n_blk):
                cols = slice(b * blk_w, (b + 1) * blk_w)
                s = sbuf_ref[bb, br, slot, :, cols]
                if tri is not None:
                    s = s + tri
                m_old = m_ref[bb, br, :, cols]
                m_new = jnp.maximum(m_old, jnp.max(s, axis=0, keepdims=True))
                alpha_ref[bb, br, :, cols] = jnp.exp2(m_old - m_new)
                pbuf_ref[bb, br, :, cols] = jnp.exp2(s - m_new).astype(BF16)
                m_ref[bb, br, :, cols] = m_new

    def accumulate(br, vt_ref, c, first=False):
        off = pl.multiple_of(c * CHUNK, CHUNK)
        for bb in seqs:
            vt = vt_ref[bb, :, pl.ds(off, CHUNK)]
            vone = [jnp.concatenate([vt[g * HEAD_DIM:(g + 1) * HEAD_DIM, :], ones_tile], axis=0)
                    for g in range(C_KV_HEADS)]
            for b in range(n_blk):
                cols = slice(b * blk_w, (b + 1) * blk_w)
                pv = _dot(vone[b // blk_per_g], pbuf_ref[bb, br, :, cols])
                acc_ref[bb, br, :, cols] = (pv if first else
                                            alpha_ref[bb, br, :, cols] * acc_ref[bb, br, :, cols] + pv)

    n_back = WINDOW // CHUNK
    chunk =[jnp.maximum(i - n_back + w, 0) for w in range(n_back + 1)]
    tile = [jnp.where(i - n_back + w < 0, n_chunks + 1, n_chunks) for w in range(n_back + 1)]

    def win_stage(w):
        if w >= 1:
            accumulate(0, vwt_ref, chunk[w - 1], first=(w == 1))
        softmax(0, w % 2, tri_old if w == 0 else (tri_diag if w == n_back else None))
        if w < n_back:
            scores(0, kw_ref, chunk[w + 1], tile[w + 1], (w + 1) % 2)

    cmp_scores()
    m_ref[:, 0] = jnp.full((nb, 1, width), NEG, F32)
    scores(0, kw_ref, chunk[0], tile[0], 0)
    imps = cmp_softmax()
    win_stage(0)
    win_stage(1)
    select_blocks(imps)

    m_ref[:, 1] = jnp.full((nb, 1, width), NEG, F32)
    alpha_ref[:, 1] = jnp.ones((nb, 1, width), F32)
    for bb in seqs:
        acc_ref[bb, 1] = jnp.zeros(acc_ref.shape[2:], F32)
        pbuf_ref[bb, 1] = jnp.zeros(pbuf_ref.shape[2:], BF16)
    scores(1, ks_ref, 0, 0, 0)
    win_stage(2)

    def sel_stage(k, slot, slot_next):
        accumulate(1, vst_ref, jnp.maximum(k - 1, 0))
        softmax(1, slot, None)
        scores(1, ks_ref, k + 1, k + 1, slot_next)

    def sel_pair(kk, carry):
        sel_stage(2 * kk, 0, 1)
        sel_stage(2 * kk + 1, 1, 0)
        return carry

    lax.fori_loop(0, i // 2, sel_pair, 0)

    @pl.when(i % 2 == 1)
    def _():
        sel_stage(i - 1, 0, 1)

    accumulate(1, vst_ref, jnp.maximum(i - 1, 0))
    win_stage(3)
    softmax(1, i % 2, tri_diag)
    win_stage(4)
    accumulate(1, vst_ref, i)
    accumulate(0, vwt_ref, chunk[n_back])
    cmp_output()

    for bb in seqs:
        gates = jax.nn.sigmoid(gt_ref[bb])
        den = [acc_ref[bb, br, HEAD_DIM:HEAD_DIM + 1, :] for br in (0, 1)]
        inv = [jnp.where(x > 0.0, 1.0 / x, 0.0) for x in den]
        for pair in range(C_HEADS // 2):
            pieces = []
            for h in (2 * pair, 2 * pair + 1):
                cs = slice(h * CHUNK, (h + 1) * CHUNK)
                pieces.append(gates[3 * h:3 * h + 1, :] * res_ref[bb, :, cs]
                              + (gates[3 * h + 1:3 * h + 2, :] * inv[1][:, cs]) * acc_ref[bb, 1, 0:HEAD_DIM, cs]
                              + (gates[3 * h + 2:3 * h + 3, :] * inv[0][:, cs]) * acc_ref[bb, 0, 0:HEAD_DIM, cs])
            o_ref[bb, :, pair * 2 * HEAD_DIM:(pair + 1) * 2 * HEAD_DIM] = (
                jnp.concatenate(pieces, axis=0).T.astype(o_ref.dtype))


def _route_tiles(n_chunks, ns):
    e = np.zeros((n_chunks + 2, CHUNK, C_KV_WIDTH), np.float32)
    r = np.arange(CHUNK)
    for c in range(n_chunks):
        e[c, r, 2 * c + r // SEL_BLOCK] = 1.0
    e[n_chunks, :, 0] = 1.0
    e[n_chunks + 1, :, 1] = 1.0
    return jnp.asarray(e, dtype=BF16)


def _nsa(ptq, gt, ksw, kc, vct, ovt, route):
    bsz, s, _ = ksw.shape
    nseg = s // CMP_STRIDE
    ns = s // SEL_BLOCK
    n_chunks = s // CHUNK
    width = C_HEADS * CHUNK
    nb = 4 if bsz % 4 == 0 else (2 if bsz % 2 == 0 else 1)
    return pl.pallas_call(
        functools.partial(_nsa_kernel, nseg=nseg, ns=ns, n_sel=min(N_SEL, ns), n_chunks=n_chunks, nb=nb),
        grid=(bsz // nb, n_chunks),
        in_specs=[
            pl.BlockSpec((nb, C_WIDTH, CHUNK), lambda b, i: (b, PT_Q // C_WIDTH, i)),
            pl.BlockSpec((nb, gt.shape[1], CHUNK), lambda b, i: (b, 0, i)),
            pl.BlockSpec((nb, s, C_KV_WIDTH), lambda b, i: (b, 0, 0)),
            pl.BlockSpec((nb, C_KV_WIDTH, s), lambda b, i: (b, PT_VS // C_KV_WIDTH, 0)),
            pl.BlockSpec((nb, s, C_KV_WIDTH), lambda b, i: (b, 0, 1)),
            pl.BlockSpec((nb, C_KV_WIDTH, s), lambda b, i: (b, PT_VW // C_KV_WIDTH, 0)),
            pl.BlockSpec((nb, nseg, C_KV_WIDTH), lambda b, i: (b, 0, 0)),
            pl.BlockSpec((nb, C_KV_WIDTH, nseg), lambda b, i: (b, 0, 0)),
            pl.BlockSpec(ovt.shape, lambda b, i: (0, 0)),
            pl.BlockSpec(route.shape, lambda b, i: (0, 0, 0)),
        ],
        out_specs=pl.BlockSpec((nb, CHUNK, C_WIDTH), lambda b, i: (b, i, 0)),
        out_shape=jax.ShapeDtypeStruct((bsz, s, C_WIDTH), BF16),
        scratch_shapes=[
            pltpu.VMEM((nb, 3 * C_KV_WIDTH, width), BF16),
            pltpu.VMEM((nb, 2, 1, width), F32),
            pltpu.VMEM((nb, 2, 1, width), F32),
            pltpu.VMEM((nb, 2, HEAD_DIM + 2 * SUBLANES, width), F32),
            pltpu.VMEM((nb, HEAD_DIM, width), F32),
            pltpu.VMEM((nb, 2, 2, CHUNK, width), F32),
            pltpu.VMEM((nb, 2, CHUNK, width), BF16),
            pltpu.VMEM((nb, nseg, width), F32),
            pltpu.VMEM((nb, nseg, width), BF16),
        ],
        compiler_params=_params(("arbitrary", "arbitrary"), 48),
        name="nsa_attention",
    )(ptq, gt, ksw, ptq, ksw, ptq, kc, vct, ovt, route)


def _tail_kernel(ya_ref, yb_ref, yc_ref, x_ref, mod_ref, wo_ref, g1_ref, b1_ref,
                 wup_ref, cw_ref, cb_ref, wdn_ref, g2_ref, b2_ref, o_ref,
                 work_ref, carry_ref, x1_ref, h_ref, act_ref, *, d, dff, cwid, tm, alpha):
    @pl.when(pl.program_id(1) == 0)
    def _():
        carry_ref[...] = jnp.zeros_like(carry_ref)

    y1 = _dot(ya_ref[0], wo_ref[0:A_WIDTH, :])
    y1 += _dot(yb_ref[0], wo_ref[A_WIDTH:A_WIDTH + B_WIDTH, :])
    y1 += _dot(yc_ref[0], wo_ref[A_WIDTH + B_WIDTH:, :])
    x1 = _row_ln(alpha * x_ref[0] + mod_ref[0, :, 2 * d:3 * d] * y1, g1_ref[...], b1_ref[...])
    x1_ref[...] = x1

    sh = mod_ref[0, :, 3 * d:4 * d]
    sc = mod_ref[0, :, 4 * d:5 * d]
    gate = mod_ref[0, :, 5 * d:6 * d]
    h_ref[...] = (x1 * (1.0 + sc) + sh).astype(BF16)
    pad = SUBLANES
    n_chunks = dff // cwid

    def up(ci):
        slot = ci % 2
        for part in range(2):
            cs = slice(part * dff + ci * cwid, part * dff + (ci + 1) * cwid)
            work_ref[slot, part, 0:pad, :] = carry_ref[:, cs]
            work_ref[slot, part, pad:pad + tm, :] = _dot(h_ref[...], wup_ref[:, cs])

    def conv_act(ci):
        slot = ci % 2
        conv = []
        for part in range(2):
            cs = slice(part * dff + ci * cwid, part * dff + (ci + 1) * cwid)
            buf = work_ref[slot, part]
            carry_ref[:, cs] = buf[tm:tm + pad, :]
            a = buf[pad:, :]
            a1 = pltpu.roll(buf, 1, 0)[pad:, :]
            a2 = pltpu.roll(buf, 2, 0)[pad:, :]
            conv.append(cw_ref[0:1, cs] * a2 + cw_ref[1:2, cs] * a1 + cw_ref[2:3, cs] * a + cb_ref[:, cs])
        half = 0.5 * conv[0]
        silu = half + half * jnp.tanh(half)
        act_ref[:, ci * cwid:(ci + 1) * cwid] = (silu * conv[1]).astype(BF16)

    split = (n_chunks // 2) * cwid
    up(0)
    y = None
    for ci in range(n_chunks):
        if ci + 1 < n_chunks:
            up(ci + 1)
        conv_act(ci)
        if (ci + 1) * cwid == split:
            y = _dot(act_ref[:, :split], wdn_ref[:split, :])
    y = y + _dot(act_ref[:, split:], wdn_ref[split:, :])
    o_ref[0] = _row_ln(alpha * x1_ref[...] + gate * y, g2_ref[...], b2_ref[...])


def _tail(ya, yb, yc, x, mod, wo, g1, b1, wup, cw, cb, wdn, g2, b2, alpha, tm=512, cwid=256):
    bsz, s, d = x.shape
    dff = wdn.shape[0]
    const = lambda bi, j: (0, 0)
    tile = lambda w: pl.BlockSpec((1, tm, w), lambda bi, j: (bi, j, 0))
    weight = lambda w: pl.BlockSpec(w.shape, const, pipeline_mode=pl.Buffered(1))
    small = lambda a: pl.BlockSpec(a.shape, const)
    return pl.pallas_call(
        functools.partial(_tail_kernel, d=d, dff=dff, cwid=cwid, tm=tm, alpha=alpha),
        grid=(bsz, s // tm),
        in_specs=[
            tile(A_WIDTH), tile(B_WIDTH), tile(C_WIDTH), tile(d),
            pl.BlockSpec((1, 1, 6 * d), lambda bi, j: (bi, 0, 0)),
            weight(wo), small(g1), small(b1),
            weight(wup), small(cw), small(cb), weight(wdn), small(g2), small(b2),
        ],
        out_specs=tile(d),
        out_shape=jax.ShapeDtypeStruct((bsz, s, d), F32),
        scratch_shapes=[
            pltpu.VMEM((2, 2, tm + SUBLANES, cwid), F32),
            pltpu.VMEM((SUBLANES, 2 * dff), F32),
            pltpu.VMEM((tm, d), F32),
            pltpu.VMEM((tm, d), BF16),
            pltpu.VMEM((tm, dff), BF16),
        ],
        compiler_params=_params(("arbitrary", "arbitrary"), 52),
        name="layer_tail",
    )(ya, yb, yc, x, mod, wo, g1, b1, wup, cw, cb, wdn, g2, b2)


def _inproj_weights(w):
    cols = np.arange(B_WIDTH)
    half = cols // (B_WIDTH // 2)
    perm = ((cols % (B_WIDTH // 2)) // (HEAD_DIM // 2)) * HEAD_DIM + half * (HEAD_DIM // 2) + cols % (HEAD_DIM // 2)
    za = w[:, 0:512]
    qb, kb, vb, gb = (w[:, 512 + i * B_WIDTH:512 + (i + 1) * B_WIDTH] for i in range(4))
    qc = w[:, 1536:2048]
    kcm, vcm, ksl, vsl, kwn, vwn = (w[:, 2048 + i * C_KV_WIDTH:2048 + (i + 1) * C_KV_WIDTH] for i in range(6))
    gc = w[:, 2816:2840]
    wn = jnp.concatenate([qb[:, perm], kb[:, perm], vb, gb, za, ksl, kwn, kcm, vcm], axis=1)
    pad = jnp.zeros((w.shape[0], PT_ROWS - PT_G - gc.shape[1]), w.dtype)
    wt = jnp.concatenate([qc * (HEAD_DIM ** -0.5 * LOG2E), vsl, vwn, gc, pad], axis=1).T
    return wn.astype(BF16), wt.astype(BF16)


def kernel(x, c, w_ada, b_ada, w_in, a_ln_g, a_ln_b, a_ws, a_bs, b_gn_g, b_gn_b, c_pos_k, c_w1_k, c_w2_k,
           c_pos_v, c_w1_v, c_w2_v, w_out, ln1_g, ln1_b, w_up, conv_w, conv_b, w_down, ln2_g, ln2_b):
    depth = w_in.shape[0]
    bsz, s, d = x.shape
    alpha = (2 * depth) ** 0.25
    nseg = s // CMP_STRIDE
    ns = s // SEL_BLOCK

    lanes = np.arange(A_WIDTH)
    mavg = jnp.asarray((lanes[:, None] // HEAD_DIM == lanes[None, :] // HEAD_DIM) / HEAD_DIM, dtype=BF16)
    gm = jnp.asarray((lanes[None, :] // HEAD_DIM == np.arange(A_GROUPS)[:, None]).astype(np.float32))
    ret_consts = _retention_consts()
    cos, sin = _rotary_tables(s)
    ovt = _overlap_t(nseg, ns)
    route = _route_tiles(s // CHUNK, ns)

    mods = _ada_mod(c, w_ada, b_ada)
    for l in range(depth):
        mod = mods[l][:, None, :]
        wn, wt = _inproj_weights(w_in[l])
        wcat = jnp.transpose(a_ws[l], (1, 0, 2)).reshape(CHUNK, A_GROUPS * CHUNK)
        bias = jnp.repeat(a_bs[l].T, HEAD_DIM, axis=1)
        ya, yb, ksw, kvc, ptq, gt, wo_bf, wup_bf, wdn_bf = _inproj(
            x, mod, wn, wt, cos, sin, ret_consts, mavg, b_gn_g[l].reshape(1, B_WIDTH), b_gn_b[l].reshape(1, B_WIDTH),
            wcat, bias, a_ln_g[l].reshape(1, A_WIDTH), a_ln_b[l].reshape(1, A_WIDTH), gm, l, w_out, w_up, w_down)

        w1k = _blockdiag2(c_w1_k[l].reshape(CMP_BLOCK, HEAD_DIM, HEAD_DIM)).astype(BF16)
        w1v = _blockdiag2(c_w1_v[l].reshape(CMP_BLOCK, HEAD_DIM, HEAD_DIM)).astype(BF16)
        kc, vct = _compress(kvc, jnp.tile(c_pos_k[l], (1, C_KV_HEADS)), jnp.tile(c_pos_v[l], (1, C_KV_HEADS)),
                            w1k, w1v, _blockdiag2(c_w2_k[l]).astype(BF16), _blockdiag2(c_w2_v[l]).T.astype(BF16))
        yc = _nsa(ptq, gt, ksw, kc, vct, ovt, route)

        x = _tail(ya, yb, yc, x, mod, wo_bf, ln1_g[l].reshape(1, d), ln1_b[l].reshape(1, d),
                  wup_bf, conv_w[l], conv_b[l].reshape(1, -1), wdn_bf,
                  ln2_g[l].reshape(1, d), ln2_b[l].reshape(1, d), alpha)
    return x
```

```python
import functools

import numpy as np
import jax
import jax.numpy as jnp
from jax import lax
from jax.experimental import pallas as pl
from jax.experimental.pallas import tpu as pltpu

F32 = jnp.float32
BF16 = jnp.bfloat16

HEAD_DIM = 64
A_GROUPS = 4
A_WIDTH = A_GROUPS * HEAD_DIM
CHUNK = 128
B_HEADS = 4
B_WIDTH = B_HEADS * HEAD_DIM
ROPE_BASE = 10000.0
C_HEADS = 8
C_KV_HEADS = 2
C_REP = C_HEADS // C_KV_HEADS
C_WIDTH = C_HEADS * HEAD_DIM
C_KV_WIDTH = C_KV_HEADS * HEAD_DIM
CMP_BLOCK = 32
CMP_STRIDE = 16
SEL_BLOCK = 64
N_SEL = 8
WINDOW = 512
CONV_WIDTH = 3
LN_EPS = 1e-5
NEG = -1e30
BIG = 1e30
LOG2E = 1.4426950408889634

SUBLANES = 8

TOKEN_TILE = 512
FFN_CHUNK = 256
ADA_TILE = 1536
NSA_SEQS = (4, 2, 1)

PN_RET, PN_ZA, PN_KS, PN_KW, PN_KVC = 0, 1024, 1536, 1664, 1792
PN_COLS = 1536
PT_Q, PT_VS, PT_VW, PT_G = 0, 512, 640, 768
PT_ROWS = 800


def _dot(a, b):
    return jnp.dot(a, b, preferred_element_type=F32)


def _dot_nt(a, b):
    return lax.dot_general(a, b, (((1,), (1,)), ((), ())), preferred_element_type=F32)


def _dot_tn(a, b):
    return lax.dot_general(a, b, (((0,), (0,)), ((), ())), preferred_element_type=F32)


def _split_dot(x, m):
    hi = x.astype(BF16)
    lo = (x - hi.astype(F32)).astype(BF16)
    return _dot(hi, m) + _dot(lo, m)


def _split_dot_left(m, x):
    hi = x.astype(BF16)
    lo = (x - hi.astype(F32)).astype(BF16)
    return _dot(m, hi) + _dot(m, lo)


def _row_ln(x, g, b):
    mu = jnp.mean(x, axis=-1, keepdims=True)
    d = x - mu
    var = jnp.mean(d * d, axis=-1, keepdims=True)
    return d * lax.rsqrt(var + LN_EPS) * g + b


def _params(sem, vmem_mb):
    return pltpu.CompilerParams(dimension_semantics=sem, vmem_limit_bytes=vmem_mb * 1024 * 1024)


def _ada_kernel(c_ref, w_ref, b_ref, o_ref):
    cond = jax.nn.silu(c_ref[...]).astype(BF16)
    o_ref[0] = _dot(cond, w_ref[0].astype(BF16)) + b_ref[0]


def _ada_mod(c, w_ada, b_ada):
    depth, d, n = w_ada.shape
    bsz = c.shape[0]
    tn = ADA_TILE
    return pl.pallas_call(
        _ada_kernel,
        grid=(depth, n // tn),
        in_specs=[
            pl.BlockSpec((bsz, d), lambda l, j: (0, 0)),
            pl.BlockSpec((1, d, tn), lambda l, j: (l, 0, j)),
            pl.BlockSpec((1, 1, tn), lambda l, j: (l, 0, j)),
        ],
        out_specs=pl.BlockSpec((1, bsz, tn), lambda l, j: (l, 0, j)),
        out_shape=jax.ShapeDtypeStruct((depth, bsz, n), F32),
        compiler_params=_params(("arbitrary", "arbitrary"), 40),
        name="ada_mod",
    )(c, w_ada, b_ada.reshape(depth, 1, n))


def _inproj_kernel(x_ref, mod_ref, wn_ref, wt_ref, cos_ref, sin_ref, dec_ref, zeta_ref, xi_ref, cd_ref, qm_ref,
                   vm_ref, bm_ref, mavg_ref, gng_ref, gnb_ref, wcat_ref, abias_ref, alg_ref, alb_ref, gm_ref,
                   wo32_ref, wup32_ref, wdn32_ref,
                   ya_ref, yb_ref, ksw_ref, kvc_ref, ptq_ref, gt_ref, wo16_ref, wup16_ref, wdn16_ref,
                   pn_ref, state_ref, *, d, n_chunks, wdn_steps):
    @pl.when(pl.program_id(1) == 0)
    def _():
        state_ref[...] = jnp.zeros_like(state_ref)

    wo16_ref[...] = wo32_ref[...].astype(BF16)
    wup16_ref[...] = wup32_ref[...].astype(BF16)

    @pl.when(pl.program_id(0) * pl.num_programs(1) + pl.program_id(1) < wdn_steps)
    def _():
        wdn16_ref[...] = wdn32_ref[...].astype(BF16)

    sh = mod_ref[0, :, 0:d]
    sc = mod_ref[0, :, d:2 * d]
    h = (x_ref[0] * (1.0 + sc) + sh).astype(BF16)
    pn_ref[...] = _dot(h, wn_ref[:, :PN_COLS])
    src = lambda rows, lo, hi: pn_ref[rows, lo:hi]
    rest = _dot(h, wn_ref[:, PN_COLS:])
    ksw_ref[0] = rest[:, :PN_KVC - PN_COLS].astype(BF16)
    kvc_ref[0] = rest[:, PN_KVC - PN_COLS:]
    _retention_chunks(src, cos_ref, sin_ref, dec_ref, zeta_ref, xi_ref, cd_ref, qm_ref, vm_ref, bm_ref,
                      mavg_ref, gng_ref, gnb_ref, yb_ref, state_ref, n_chunks)
    pt = _dot_nt(wt_ref[...], h)
    ptq_ref[0] = pt[:PT_G, :].astype(BF16)
    gt_ref[0] = pt[PT_G:, :]
    _gmlp_chunks(src, wcat_ref, abias_ref, alg_ref, alb_ref, mavg_ref, gm_ref, ya_ref, n_chunks)


def _inproj(x, mod, wn, wt, cos, sin, ret_consts, mavg, gn_g, gn_b, wcat, abias, al_g, al_b, gm,
            layer, w_out, w_up, w_down, tm=TOKEN_TILE):
    bsz, s, d = x.shape
    nt = s // tm
    steps = bsz * nt
    dff = w_down.shape[1]
    bf16_rows = 2 * SUBLANES
    slab = d // steps
    assert d % steps == 0 and slab % bf16_rows == 0
    wdn_steps = max(k for k in range(1, steps + 1) if dff % k == 0 and (dff // k) % bf16_rows == 0)
    wdn_slab = dff // wdn_steps
    const = lambda b, j: (0, 0)
    small = lambda a: pl.BlockSpec(a.shape, const)
    rows = lambda w: pl.BlockSpec((1, tm, w), lambda b, j: (b, j, 0))
    step = lambda b, j: b * nt + j
    return pl.pallas_call(
        functools.partial(_inproj_kernel, d=d, n_chunks=tm // CHUNK, wdn_steps=wdn_steps),
        grid=(bsz, nt),
        in_specs=[
            rows(d),
            pl.BlockSpec((1, 1, 6 * d), lambda b, j: (b, 0, 0)),
            small(wn), small(wt),
            pl.BlockSpec((tm, B_WIDTH // 2), lambda b, j: (j, 0)),
            pl.BlockSpec((tm, B_WIDTH // 2), lambda b, j: (j, 0)),
        ] + [small(a) for a in ret_consts] + [
            small(mavg), small(gn_g), small(gn_b), small(wcat), small(abias), small(al_g), small(al_b), small(gm),
            pl.BlockSpec((None, slab, d), lambda b, j: (layer, step(b, j), 0)),
            pl.BlockSpec((None, slab, w_up.shape[2]), lambda b, j: (layer, step(b, j), 0)),
            pl.BlockSpec((None, wdn_slab, d), lambda b, j: (layer, jnp.minimum(step(b, j), wdn_steps - 1), 0)),
        ],
        out_specs=[
            rows(A_WIDTH), rows(B_WIDTH), rows(2 * C_KV_WIDTH), rows(2 * C_KV_WIDTH),
            pl.BlockSpec((1, PT_G, tm), lambda b, j: (b, 0, j)),
            pl.BlockSpec((1, PT_ROWS - PT_G, tm), lambda b, j: (b, 0, j)),
            pl.BlockSpec((slab, d), lambda b, j: (step(b, j), 0)),
            pl.BlockSpec((slab, w_up.shape[2]), lambda b, j: (step(b, j), 0)),
            pl.BlockSpec((wdn_slab, d), lambda b, j: (jnp.minimum(step(b, j), wdn_steps - 1), 0)),
        ],
        out_shape=[
            jax.ShapeDtypeStruct((bsz, s, A_WIDTH), BF16),
            jax.ShapeDtypeStruct((bsz, s, B_WIDTH), BF16),
            jax.ShapeDtypeStruct((bsz, s, 2 * C_KV_WIDTH), BF16),
            jax.ShapeDtypeStruct((bsz, s, 2 * C_KV_WIDTH), F32),
            jax.ShapeDtypeStruct((bsz, PT_G, s), BF16),
            jax.ShapeDtypeStruct((bsz, PT_ROWS - PT_G, s), F32),
            jax.ShapeDtypeStruct(w_out.shape[1:], BF16),
            jax.ShapeDtypeStruct(w_up.shape[1:], BF16),
            jax.ShapeDtypeStruct(w_down.shape[1:], BF16),
        ],
        scratch_shapes=[
            pltpu.VMEM((tm, PN_COLS), F32),
            pltpu.VMEM((B_WIDTH, B_WIDTH), F32),
        ],
        compiler_params=_params(("arbitrary", "arbitrary"), 48),
        name="inproj_mixers",
    )(x, mod, wn, wt, cos, sin, *ret_consts, mavg, gn_g, gn_b, wcat, abias, al_g, al_b, gm, w_out, w_up, w_down)


def _gmlp_chunks(src, w_ref, bias_ref, g_ref, b_ref, mavg_ref, gm_ref, o_ref, n_chunks):
    row = lax.broadcasted_iota(jnp.int32, (CHUNK, A_GROUPS * CHUNK), 0)
    col = lax.broadcasted_iota(jnp.int32, (CHUNK, A_GROUPS * CHUNK), 1)
    wc = jnp.where((col % CHUNK) <= row, w_ref[...], 0.0).astype(BF16)
    chunks = range(n_chunks)
    rows = [slice(c * CHUNK, (c + 1) * CHUNK) for c in chunks]
    z = [jax.nn.gelu(src(r, PN_ZA, PN_ZA + 2 * A_WIDTH)) for r in rows]
    v = [zc[:, A_WIDTH:] for zc in z]
    mavg = mavg_ref[...]
    mu = [_split_dot(vc, mavg) for vc in v]
    dev = [vc - m for vc, m in zip(v, mu)]
    var = [_split_dot(d * d, mavg) for d in dev]
    vn = [d * lax.rsqrt(s2 + LN_EPS) * g_ref[...] + b_ref[...] for d, s2 in zip(dev, var)]
    vstack = [jnp.concatenate([x * gm_ref[g:g + 1, :] for g in range(A_GROUPS)], axis=0).astype(BF16) for x in vn]
    vs = [_dot(wc, x) + bias_ref[...] for x in vstack]
    for c in chunks:
        o_ref[0, rows[c], :] = (z[c][:, :A_WIDTH] * vs[c]).astype(o_ref.dtype)


def _retention_chunks(src, cos_ref, sin_ref, dec_ref, zeta_ref, xi_ref, cd_ref, qm_ref, vm_ref, bm_ref,
                      mavg_ref, g_ref, b_ref, o_ref, state_ref, n_chunks):
    half = B_WIDTH // 2
    chunks = range(n_chunks)
    rows = [slice(c * CHUNK, (c + 1) * CHUNK) for c in chunks]

    def rot(t, r):
        t1 = t[:, :half]
        t2 = t[:, half:]
        cos = cos_ref[r, :]
        sin = sin_ref[r, :]
        return jnp.concatenate([t1 * cos - t2 * sin, t1 * sin + t2 * cos], axis=1)

    qr = [rot(src(r, 0, B_WIDTH), r) for r in rows]
    kr = [rot(src(r, B_WIDTH, 2 * B_WIDTH), r) * (HEAD_DIM ** -0.5) for r in rows]
    v = [src(r, 2 * B_WIDTH, 3 * B_WIDTH) for r in rows]
    qs = [jnp.concatenate([x * qm_ref[h:h + 1, :] for h in range(B_HEADS)], axis=0).astype(BF16) for x in qr]
    s = [_dot_nt(a, b.astype(BF16)) * dec_ref[...] for a, b in zip(qs, kr)]
    kv = [_dot_tn((a * zeta_ref[...]).astype(BF16), b.astype(BF16)) * bm_ref[...]
          for a, b in zip(kr, v)]
    scat = [jnp.concatenate([x[h * CHUNK:(h + 1) * CHUNK, :] for h in range(B_HEADS)], axis=1).astype(BF16)
            for x in s]
    vstack = [jnp.concatenate([x * vm_ref[h:h + 1, :] for h in range(B_HEADS)], axis=0).astype(BF16) for x in v]
    o_inner = [_dot(a, b) for a, b in zip(scat, vstack)]
    state = state_ref[...]
    before = []
    for c in chunks:
        before.append(state)
        state = state * cd_ref[...] + kv[c]
    state_ref[...] = state
    o = [oi + _dot(a.astype(BF16), st.astype(BF16)) * xi_ref[...] for oi, a, st in zip(o_inner, qr, before)]
    mavg = mavg_ref[...]
    mu = [_split_dot(x, mavg) for x in o]
    dev = [x - m for x, m in zip(o, mu)]
    var = [_split_dot(d * d, mavg) for d in dev]
    for c in chunks:
        normed = dev[c] * lax.rsqrt(var[c] + LN_EPS) * g_ref[...] + b_ref[...]
        gate = src(rows[c], 3 * B_WIDTH, 4 * B_WIDTH)
        o_ref[0, rows[c], :] = (jax.nn.silu(gate) * normed).astype(o_ref.dtype)


def _retention_consts():
    h_n, d, l_n = B_HEADS, HEAD_DIM, CHUNK
    log_gamma = jnp.log1p(-jnp.exp2(-5.0 - jnp.arange(h_n, dtype=F32)))
    idx = jnp.arange(l_n, dtype=F32)
    diff = idx[:, None] - idx[None, :]
    decay_in = jnp.where(diff >= 0, jnp.exp(log_gamma[:, None, None] * jnp.maximum(diff, 0.0)), 0.0)
    xi = jnp.exp(log_gamma[:, None] * (idx + 1.0))
    zeta = jnp.exp(log_gamma[:, None] * (l_n - 1.0 - idx))
    chunk_decay = jnp.exp(log_gamma * l_n)
    cols = np.arange(B_WIDTH)
    head_perm = (cols % (B_WIDTH // 2)) // (d // 2)
    head_std = cols // d
    dec = decay_in.reshape(h_n * l_n, l_n)
    zeta_t = zeta.T[:, head_perm]
    xi_t = xi.T[:, head_std]
    cd = chunk_decay[head_std][None, :]
    qm = jnp.asarray((head_perm[None, :] == np.arange(h_n)[:, None]).astype(np.float32))
    vm = jnp.asarray((head_std[None, :] == np.arange(h_n)[:, None]).astype(np.float32))
    bm = jnp.asarray((head_perm[:, None] == head_std[None, :]).astype(np.float32))
    return [dec, zeta_t, xi_t, cd, qm, vm, bm]


def _rotary_tables(s):
    half = HEAD_DIM // 2
    inv = jnp.power(ROPE_BASE, -jnp.arange(half, dtype=F32) / half)
    ang = jnp.arange(s).astype(F32)[:, None] * inv[None, :]
    return jnp.tile(jnp.cos(ang), (1, B_HEADS)), jnp.tile(jnp.sin(ang), (1, B_HEADS))


def _compress_kernel(xk_ref, xv_ref, posk_ref, posv_ref, w1k_ref, w1v_ref, w2k_ref, w2vt_ref, kc_ref, vct_ref,
                     *, nseg):
    half = CMP_BLOCK // 2
    acc = [jnp.zeros((nseg, C_KV_WIDTH), F32) for _ in range(4)]
    for l in range(half):
        xk = xk_ref[0, pl.ds(l, nseg, stride=CMP_STRIDE), :]
        xv = xv_ref[0, pl.ds(l, nseg, stride=CMP_STRIDE), :]
        acc[0] += _dot((xk + posk_ref[l:l + 1, :]).astype(BF16), w1k_ref[l])
        acc[1] += _dot((xk + posk_ref[half + l:half + l + 1, :]).astype(BF16), w1k_ref[half + l])
        acc[2] += _dot((xv + posv_ref[l:l + 1, :]).astype(BF16), w1v_ref[l])
        acc[3] += _dot((xv + posv_ref[half + l:half + l + 1, :]).astype(BF16), w1v_ref[half + l])
    hk = jax.nn.gelu(acc[0] + pltpu.roll(acc[1], nseg - 1, 0))
    hv = jax.nn.gelu(acc[2] + pltpu.roll(acc[3], nseg - 1, 0))
    kc_ref[0] = _dot(hk.astype(BF16), w2k_ref[...])
    vct_ref[0] = _dot_nt(w2vt_ref[...], hv.astype(BF16))


def _compress(kvc, posk, posv, w1k, w1v, w2k, w2v):
    bsz, s, width = kvc.shape
    nseg = s // CMP_STRIDE
    c2 = lambda bi: (0, 0)
    c3 = lambda bi: (0, 0, 0)
    return pl.pallas_call(
        functools.partial(_compress_kernel, nseg=nseg),
        grid=(bsz,),
        in_specs=[
            pl.BlockSpec((1, s, C_KV_WIDTH), lambda bi: (bi, 0, 0)),
            pl.BlockSpec((1, s, C_KV_WIDTH), lambda bi: (bi, 0, 1)),
            pl.BlockSpec(posk.shape, c2),
            pl.BlockSpec(posv.shape, c2),
            pl.BlockSpec(w1k.shape, c3),
            pl.BlockSpec(w1v.shape, c3),
            pl.BlockSpec(w2k.shape, c2),
            pl.BlockSpec(w2v.shape, c2),
        ],
        out_specs=[
            pl.BlockSpec((1, nseg, C_KV_WIDTH), lambda bi: (bi, 0, 0)),
            pl.BlockSpec((1, C_KV_WIDTH, nseg), lambda bi: (bi, 0, 0)),
        ],
        out_shape=[
            jax.ShapeDtypeStruct((bsz, nseg, C_KV_WIDTH), F32),
            jax.ShapeDtypeStruct((bsz, C_KV_WIDTH, nseg), F32),
        ],
        compiler_params=_params(("arbitrary",), 32),
        name="nsa_compress",
    )(kvc, kvc, posk, posv, w1k, w1v, w2k, w2v)


def _blockdiag2(w):
    z = jnp.zeros_like(w)
    return jnp.concatenate([jnp.concatenate([w, z], axis=-1), jnp.concatenate([z, w], axis=-1)], axis=-2)


def _overlap_t(nseg, ns):
    nc = nseg - 1
    c0 = np.arange(nc)[None, :] * CMP_STRIDE
    s0 = np.arange(ns)[:, None] * SEL_BLOCK
    ov = np.clip(np.minimum(c0 + CMP_BLOCK, s0 + SEL_BLOCK) - np.maximum(c0, s0), 0, None) / CMP_BLOCK
    out = np.zeros((ns, nseg), np.float32)
    out[:, :nc] = ov
    return jnp.asarray(out, dtype=BF16)


def _nsa_kernel(qt_ref, gt_ref, ks_ref, vst_ref, kw_ref, vwt_ref, kc_ref, vct_ref, ovt_ref, e_ref, o_ref,
                qb_ref, m_ref, alpha_ref, acc_ref, res_ref, sbuf_ref, pbuf_ref,
                cbuf_ref, pcmp_ref, *, nseg, ns, n_sel, n_chunks, nb):
    i = pl.program_id(1)
    s0 = i * CHUNK
    width = C_HEADS * CHUNK
    blk_w = 2 * CHUNK
    n_blk = C_HEADS // 2
    blk_per_g = n_blk // C_KV_HEADS
    seqs = range(nb)

    @pl.when((pl.program_id(0) == 0) & (i == 0))
    def _():
        brow = lax.broadcasted_iota(jnp.int32, (C_KV_WIDTH, width), 0)
        for bb in seqs:
            qb_ref[bb, 0:C_KV_WIDTH, :] = jnp.zeros((C_KV_WIDTH, width), BF16)
            qb_ref[bb, 2 * C_KV_WIDTH:, :] = jnp.where(brow == 1, NEG, 0.0).astype(BF16)

    zero = jnp.zeros((HEAD_DIM, CHUNK), BF16)
    for bb in seqs:
        for h in range(C_HEADS):
            qh = qt_ref[bb, h * HEAD_DIM:(h + 1) * HEAD_DIM, :]
            blk = jnp.concatenate([qh, zero] if h < C_REP else [zero, qh], axis=0)
            qb_ref[bb, C_KV_WIDTH:2 * C_KV_WIDTH, h * CHUNK:(h + 1) * CHUNK] = blk
    ones_tile = jnp.where(lax.broadcasted_iota(jnp.int32, (2 * SUBLANES, CHUNK), 0) == 0, 1.0, 0.0).astype(BF16)

    krow = lax.broadcasted_iota(jnp.int32, (CHUNK, blk_w), 0)
    t_loc = lax.broadcasted_iota(jnp.int32, (CHUNK, blk_w), 1) % CHUNK
    tri_diag = jnp.where(krow <= t_loc, 0.0, NEG)
    tri_old = jnp.where(krow > t_loc, 0.0, NEG)

    def cmp_scores():
        for bb in seqs:
            cbuf_ref[bb] = _dot(kc_ref[bb].astype(BF16), qb_ref[bb, C_KV_WIDTH:2 * C_KV_WIDTH, :])

    def cmp_softmax():
        crow = lax.broadcasted_iota(jnp.int32, (nseg, blk_w), 0)
        c_t = s0 + lax.broadcasted_iota(jnp.int32, (nseg, blk_w), 1) % CHUNK
        cbias = jnp.where(crow * CMP_STRIDE + (CMP_BLOCK - 1) <= c_t, 0.0, NEG)
        imps = []
        for bb in seqs:
            psum = [None] * C_KV_HEADS
            for b in range(n_blk):
                g = b // blk_per_g
                cols = slice(b * blk_w, (b + 1) * blk_w)
                sc = cbuf_ref[bb, :, cols] + cbias
                mx = jnp.max(sc, axis=0, keepdims=True)
                e = jnp.exp2(sc - mx)
                den = jnp.sum(e, axis=0, keepdims=True)
                p = e * jnp.where(mx > 0.5 * NEG, 1.0 / den, 0.0)
                pcmp_ref[bb, :, cols] = p.astype(BF16)
                both = p[:, :CHUNK] + p[:, CHUNK:]
                psum[g] = both if psum[g] is None else psum[g] + both
            imps.append([_split_dot_left(ovt_ref[...], ps) for ps in psum])
        return imps

    def cmp_output():
        for bb in seqs:
            vct = vct_ref[bb].astype(BF16)
            for b in range(n_blk):
                g = b // blk_per_g
                cols = slice(b * blk_w, (b + 1) * blk_w)
                res_ref[bb, :, cols] = _dot(vct[g * HEAD_DIM:(g + 1) * HEAD_DIM, :], pcmp_ref[bb, :, cols])

    def select_blocks(all_imps):
        j = lax.broadcasted_iota(jnp.int32, (ns, CHUNK), 0)
        cur = (s0 + lax.broadcasted_iota(jnp.int32, (ns, CHUNK), 1)) // SEL_BLOCK
        forced = (j == 0) | (j == cur) | (j == cur - 1)
        future = j > cur
        slab_rows = lax.broadcasted_iota(jnp.int32, (SUBLANES, CHUNK), 0)
        for bb, g in [(bb, g) for bb in seqs for g in range(C_KV_HEADS)]:
            imps = all_imps[bb]
            imp = jnp.where(forced, BIG, jnp.where(future, NEG, imps[g]))
            slabs = [imp[v * SUBLANES:(v + 1) * SUBLANES, :] for v in range(ns // SUBLANES)]
            ranks = [jnp.zeros((SUBLANES, CHUNK), F32) for _ in slabs]
            for i2 in range(ns):
                r_i = imp[i2:i2 + 1, :]
                for v, slab in enumerate(slabs):
                    if (v + 1) * SUBLANES - 1 <= i2:
                        ranks[v] = ranks[v] + jnp.where(r_i > slab, 1.0, 0.0)
                    elif v * SUBLANES > i2:
                        ranks[v] = ranks[v] + jnp.where(r_i >= slab, 1.0, 0.0)
                    else:
                        tie = jnp.where(slab_rows > i2 - v * SUBLANES, 1.0, 0.0)
                        ranks[v] = ranks[v] + jnp.where(r_i > slab, 1.0, 0.0) + jnp.where(r_i == slab, tie, 0.0)
            rank = jnp.concatenate(ranks, axis=0)
            sel_bias = jnp.where(rank < n_sel, 0.0, NEG).astype(BF16)
            for r in range(C_REP):
                h = g * C_REP + r
                qb_ref[bb, 0:ns, h * CHUNK:(h + 1) * CHUNK] = sel_bias

    def scores(br, k_ref, c, tile, slot):
        off = pl.multiple_of(c * CHUNK, CHUNK)
        e_tile = e_ref[tile]
        for bb in seqs:
            kch = k_ref[bb, pl.ds(off, CHUNK), :]
            if br == 0:
                sbuf_ref[bb, 0, slot] = _dot(jnp.concatenate([kch, e_tile], axis=1),
                                             qb_ref[bb, C_KV_WIDTH:3 * C_KV_WIDTH, :])
            else:
                sbuf_ref[bb, 1, slot] = _dot(jnp.concatenate([e_tile, kch], axis=1),
                                             qb_ref[bb, 0:2 * C_KV_WIDTH, :])

    def softmax(br, slot, tri):
        for bb in seqs:
            for b in range(n_blk):
                cols = slice(b * blk_w, (b + 1) * blk_w)
                s = sbuf_ref[bb, br, slot, :, cols]
                if tri is not None:
                    s = s + tri
                m_old = m_ref[bb, br, :, cols]
                m_new = jnp.maximum(m_old, jnp.max(s, axis=0, keepdims=True))
                alpha_ref[bb, br, :, cols] = jnp.exp2(m_old - m_new)
                pbuf_ref[bb, br, :, cols] = jnp.exp2(s - m_new).astype(BF16)
                m_ref[bb, br, :, cols] = m_new

    def accumulate(br, vt_ref, c, first=False):
        off = pl.multiple_of(c * CHUNK, CHUNK)
        for bb in seqs:
            vt = vt_ref[bb, :, pl.ds(off, CHUNK)]
            vone = [jnp.concatenate([vt[g * HEAD_DIM:(g + 1) * HEAD_DIM, :], ones_tile], axis=0)
                    for g in range(C_KV_HEADS)]
            for b in range(n_blk):
                cols = slice(b * blk_w, (b + 1) * blk_w)
                pv = _dot(vone[b // blk_per_g], pbuf_ref[bb, br, :, cols])
                acc_ref[bb, br, :, cols] = (pv if first else
                                            alpha_ref[bb, br, :, cols] * acc_ref[bb, br, :, cols] + pv)

    n_back = WINDOW // CHUNK
    chunk =[jnp.maximum(i - n_back + w, 0) for w in range(n_back + 1)]
    tile = [jnp.where(i - n_back + w < 0, n_chunks + 1, n_chunks) for w in range(n_back + 1)]

    def win_stage(w):
        if w >= 1:
            accumulate(0, vwt_ref, chunk[w - 1], first=(w == 1))
        softmax(0, w % 2, tri_old if w == 0 else (tri_diag if w == n_back else None))
        if w < n_back:
            scores(0, kw_ref, chunk[w + 1], tile[w + 1], (w + 1) % 2)

    cmp_scores()
    m_ref[:, 0] = jnp.full((nb, 1, width), NEG, F32)
    scores(0, kw_ref, chunk[0], tile[0], 0)
    imps = cmp_softmax()
    win_stage(0)
    win_stage(1)
    select_blocks(imps)

    m_ref[:, 1] = jnp.full((nb, 1, width), NEG, F32)
    alpha_ref[:, 1] = jnp.ones((nb, 1, width), F32)
    for bb in seqs:
        acc_ref[bb, 1] = jnp.zeros(acc_ref.shape[2:], F32)
        pbuf_ref[bb, 1] = jnp.zeros(pbuf_ref.shape[2:], BF16)
    scores(1, ks_ref, 0, 0, 0)
    win_stage(2)

    def sel_stage(k, slot, slot_next):
        accumulate(1, vst_ref, jnp.maximum(k - 1, 0))
        softmax(1, slot, None)
        scores(1, ks_ref, k + 1, k + 1, slot_next)

    def sel_pair(kk, carry):
        sel_stage(2 * kk, 0, 1)
        sel_stage(2 * kk + 1, 1, 0)
        return carry

    lax.fori_loop(0, i // 2, sel_pair, 0)

    @pl.when(i % 2 == 1)
    def _():
        sel_stage(i - 1, 0, 1)

    accumulate(1, vst_ref, jnp.maximum(i - 1, 0))
    win_stage(3)
    softmax(1, i % 2, tri_diag)
    win_stage(4)
    accumulate(1, vst_ref, i)
    accumulate(0, vwt_ref, chunk[n_back])
    cmp_output()

    for bb in seqs:
        gates = jax.nn.sigmoid(gt_ref[bb])
        den = [acc_ref[bb, br, HEAD_DIM:HEAD_DIM + 1, :] for br in (0, 1)]
        inv = [jnp.where(x > 0.0, 1.0 / x, 0.0) for x in den]
        for pair in range(C_HEADS // 2):
            pieces = []
            for h in (2 * pair, 2 * pair + 1):
                cs = slice(h * CHUNK, (h + 1) * CHUNK)
                pieces.append(gates[3 * h:3 * h + 1, :] * res_ref[bb, :, cs]
                              + (gates[3 * h + 1:3 * h + 2, :] * inv[1][:, cs]) * acc_ref[bb, 1, 0:HEAD_DIM, cs]
                              + (gates[3 * h + 2:3 * h + 3, :] * inv[0][:, cs]) * acc_ref[bb, 0, 0:HEAD_DIM, cs])
            o_ref[bb, :, pair * 2 * HEAD_DIM:(pair + 1) * 2 * HEAD_DIM] = (
                jnp.concatenate(pieces, axis=0).T.astype(o_ref.dtype))


def _route_tiles(n_chunks, ns):
    e = np.zeros((n_chunks + 2, CHUNK, C_KV_WIDTH), np.float32)
    r = np.arange(CHUNK)
    for c in range(n_chunks):
        e[c, r, 2 * c + r // SEL_BLOCK] = 1.0
    e[n_chunks, :, 0] = 1.0
    e[n_chunks + 1, :, 1] = 1.0
    return jnp.asarray(e, dtype=BF16)


def _nsa(ptq, gt, ksw, kc, vct, ovt, route):
    bsz, s, _ = ksw.shape
    nseg = s // CMP_STRIDE
    ns = s // SEL_BLOCK
    n_chunks = s // CHUNK
    width = C_HEADS * CHUNK
    nb = next(n for n in NSA_SEQS if bsz % n == 0)
    return pl.pallas_call(
        functools.partial(_nsa_kernel, nseg=nseg, ns=ns, n_sel=min(N_SEL, ns), n_chunks=n_chunks, nb=nb),
        grid=(bsz // nb, n_chunks),
        in_specs=[
            pl.BlockSpec((nb, C_WIDTH, CHUNK), lambda b, i: (b, PT_Q // C_WIDTH, i)),
            pl.BlockSpec((nb, gt.shape[1], CHUNK), lambda b, i: (b, 0, i)),
            pl.BlockSpec((nb, s, C_KV_WIDTH), lambda b, i: (b, 0, 0)),
            pl.BlockSpec((nb, C_KV_WIDTH, s), lambda b, i: (b, PT_VS // C_KV_WIDTH, 0)),
            pl.BlockSpec((nb, s, C_KV_WIDTH), lambda b, i: (b, 0, 1)),
            pl.BlockSpec((nb, C_KV_WIDTH, s), lambda b, i: (b, PT_VW // C_KV_WIDTH, 0)),
            pl.BlockSpec((nb, nseg, C_KV_WIDTH), lambda b, i: (b, 0, 0)),
            pl.BlockSpec((nb, C_KV_WIDTH, nseg), lambda b, i: (b, 0, 0)),
            pl.BlockSpec(ovt.shape, lambda b, i: (0, 0)),
            pl.BlockSpec(route.shape, lambda b, i: (0, 0, 0)),
        ],
        out_specs=pl.BlockSpec((nb, CHUNK, C_WIDTH), lambda b, i: (b, i, 0)),
        out_shape=jax.ShapeDtypeStruct((bsz, s, C_WIDTH), BF16),
        scratch_shapes=[
            pltpu.VMEM((nb, 3 * C_KV_WIDTH, width), BF16),
            pltpu.VMEM((nb, 2, 1, width), F32),
            pltpu.VMEM((nb, 2, 1, width), F32),
            pltpu.VMEM((nb, 2, HEAD_DIM + 2 * SUBLANES, width), F32),
            pltpu.VMEM((nb, HEAD_DIM, width), F32),
            pltpu.VMEM((nb, 2, 2, CHUNK, width), F32),
            pltpu.VMEM((nb, 2, CHUNK, width), BF16),
            pltpu.VMEM((nb, nseg, width), F32),
            pltpu.VMEM((nb, nseg, width), BF16),
        ],
        compiler_params=_params(("arbitrary", "arbitrary"), 48),
        name="nsa_attention",
    )(ptq, gt, ksw, ptq, ksw, ptq, kc, vct, ovt, route)


def _tail_kernel(ya_ref, yb_ref, yc_ref, x_ref, mod_ref, wo_ref, g1_ref, b1_ref,
                 wup_ref, cw_ref, cb_ref, wdn_ref, g2_ref, b2_ref, o_ref,
                 work_ref, carry_ref, x1_ref, h_ref, act_ref, *, d, dff, cwid, tm, alpha):
    @pl.when(pl.program_id(1) == 0)
    def _():
        carry_ref[...] = jnp.zeros_like(carry_ref)

    y1 = _dot(ya_ref[0], wo_ref[0:A_WIDTH, :])
    y1 += _dot(yb_ref[0], wo_ref[A_WIDTH:A_WIDTH + B_WIDTH, :])
    y1 += _dot(yc_ref[0], wo_ref[A_WIDTH + B_WIDTH:, :])
    x1 = _row_ln(alpha * x_ref[0] + mod_ref[0, :, 2 * d:3 * d] * y1, g1_ref[...], b1_ref[...])
    x1_ref[...] = x1

    sh = mod_ref[0, :, 3 * d:4 * d]
    sc = mod_ref[0, :, 4 * d:5 * d]
    gate = mod_ref[0, :, 5 * d:6 * d]
    h_ref[...] = (x1 * (1.0 + sc) + sh).astype(BF16)
    pad = SUBLANES
    n_chunks = dff // cwid

    def up(ci):
        slot = ci % 2
        for part in range(2):
            cs = slice(part * dff + ci * cwid, part * dff + (ci + 1) * cwid)
            work_ref[slot, part, 0:pad, :] = carry_ref[:, cs]
            work_ref[slot, part, pad:pad + tm, :] = _dot(h_ref[...], wup_ref[:, cs])

    def conv_act(ci):
        slot = ci % 2
        conv = []
        for part in range(2):
            cs = slice(part * dff + ci * cwid, part * dff + (ci + 1) * cwid)
            buf = work_ref[slot, part]
            carry_ref[:, cs] = buf[tm:tm + pad, :]
            a = buf[pad:, :]
            a1 = pltpu.roll(buf, 1, 0)[pad:, :]
            a2 = pltpu.roll(buf, 2, 0)[pad:, :]
            conv.append(cw_ref[0:1, cs] * a2 + cw_ref[1:2, cs] * a1 + cw_ref[2:3, cs] * a + cb_ref[:, cs])
        half = 0.5 * conv[0]
        silu = half + half * jnp.tanh(half)
        act_ref[:, ci * cwid:(ci + 1) * cwid] = (silu * conv[1]).astype(BF16)

    split = (n_chunks // 2) * cwid
    up(0)
    y = None
    for ci in range(n_chunks):
        if ci + 1 < n_chunks:
            up(ci + 1)
        conv_act(ci)
        if (ci + 1) * cwid == split:
            y = _dot(act_ref[:, :split], wdn_ref[:split, :])
    y = y + _dot(act_ref[:, split:], wdn_ref[split:, :])
    o_ref[0] = _row_ln(alpha * x1_ref[...] + gate * y, g2_ref[...], b2_ref[...])


def _tail(ya, yb, yc, x, mod, wo, g1, b1, wup, cw, cb, wdn, g2, b2, alpha, tm=TOKEN_TILE, cwid=FFN_CHUNK):
    bsz, s, d = x.shape
    dff = wdn.shape[0]
    const = lambda bi, j: (0, 0)
    tile = lambda w: pl.BlockSpec((1, tm, w), lambda bi, j: (bi, j, 0))
    weight = lambda w: pl.BlockSpec(w.shape, const, pipeline_mode=pl.Buffered(1))
    small = lambda a: pl.BlockSpec(a.shape, const)
    return pl.pallas_call(
        functools.partial(_tail_kernel, d=d, dff=dff, cwid=cwid, tm=tm, alpha=alpha),
        grid=(bsz, s // tm),
        in_specs=[
            tile(A_WIDTH), tile(B_WIDTH), tile(C_WIDTH), tile(d),
            pl.BlockSpec((1, 1, 6 * d), lambda bi, j: (bi, 0, 0)),
            weight(wo), small(g1), small(b1),
            weight(wup), small(cw), small(cb), weight(wdn), small(g2), small(b2),
        ],
        out_specs=tile(d),
        out_shape=jax.ShapeDtypeStruct((bsz, s, d), F32),
        scratch_shapes=[
            pltpu.VMEM((2, 2, tm + SUBLANES, cwid), F32),
            pltpu.VMEM((SUBLANES, 2 * dff), F32),
            pltpu.VMEM((tm, d), F32),
            pltpu.VMEM((tm, d), BF16),
            pltpu.VMEM((tm, dff), BF16),
        ],
        compiler_params=_params(("arbitrary", "arbitrary"), 52),
        name="layer_tail",
    )(ya, yb, yc, x, mod, wo, g1, b1, wup, cw, cb, wdn, g2, b2)


def _inproj_weights(w):
    cols = np.arange(B_WIDTH)
    half = cols // (B_WIDTH // 2)
    perm = ((cols % (B_WIDTH // 2)) // (HEAD_DIM // 2)) * HEAD_DIM + half * (HEAD_DIM // 2) + cols % (HEAD_DIM // 2)
    scale = np.ones((1, w.shape[1]), np.float32)
    scale[:, 1536:2048] = HEAD_DIM ** -0.5 * LOG2E
    w = (w * scale).astype(BF16)
    za = w[:, 0:512]
    qb, kb, vb, gb = (w[:, 512 + i * B_WIDTH:512 + (i + 1) * B_WIDTH] for i in range(4))
    qc = w[:, 1536:2048]
    kcm, vcm, ksl, vsl, kwn, vwn = (w[:, 2048 + i * C_KV_WIDTH:2048 + (i + 1) * C_KV_WIDTH] for i in range(6))
    gc = w[:, 2816:2840]
    wn = jnp.concatenate([qb[:, perm], kb[:, perm], vb, gb, za, ksl, kwn, kcm, vcm], axis=1)
    pad = jnp.zeros((w.shape[0], PT_ROWS - PT_G - gc.shape[1]), w.dtype)
    wt = jnp.concatenate([qc, vsl, vwn, gc, pad], axis=1).T
    return wn, wt


def kernel(x, c, w_ada, b_ada, w_in, a_ln_g, a_ln_b, a_ws, a_bs, b_gn_g, b_gn_b, c_pos_k, c_w1_k, c_w2_k,
           c_pos_v, c_w1_v, c_w2_v, w_out, ln1_g, ln1_b, w_up, conv_w, conv_b, w_down, ln2_g, ln2_b):
    depth = w_in.shape[0]
    bsz, s, d = x.shape
    alpha = (2 * depth) ** 0.25
    nseg = s // CMP_STRIDE
    ns = s // SEL_BLOCK

    lanes = np.arange(A_WIDTH)
    mavg = jnp.asarray((lanes[:, None] // HEAD_DIM == lanes[None, :] // HEAD_DIM) / HEAD_DIM, dtype=BF16)
    gm = jnp.asarray((lanes[None, :] // HEAD_DIM == np.arange(A_GROUPS)[:, None]).astype(np.float32))
    ret_consts = _retention_consts()
    cos, sin = _rotary_tables(s)
    ovt = _overlap_t(nseg, ns)
    route = _route_tiles(s // CHUNK, ns)

    mods = _ada_mod(c, w_ada, b_ada)
    for l in range(depth):
        mod = mods[l][:, None, :]
        wn, wt = _inproj_weights(w_in[l])
        wcat = jnp.transpose(a_ws[l], (1, 0, 2)).reshape(CHUNK, A_GROUPS * CHUNK)
        bias = jnp.repeat(a_bs[l].T, HEAD_DIM, axis=1)
        ya, yb, ksw, kvc, ptq, gt, wo_bf, wup_bf, wdn_bf = _inproj(
            x, mod, wn, wt, cos, sin, ret_consts, mavg, b_gn_g[l].reshape(1, B_WIDTH), b_gn_b[l].reshape(1, B_WIDTH),
            wcat, bias, a_ln_g[l].reshape(1, A_WIDTH), a_ln_b[l].reshape(1, A_WIDTH), gm, l, w_out, w_up, w_down)

        w1k = _blockdiag2(c_w1_k[l].reshape(CMP_BLOCK, HEAD_DIM, HEAD_DIM)).astype(BF16)
        w1v = _blockdiag2(c_w1_v[l].reshape(CMP_BLOCK, HEAD_DIM, HEAD_DIM)).astype(BF16)
        kc, vct = _compress(kvc, jnp.tile(c_pos_k[l], (1, C_KV_HEADS)), jnp.tile(c_pos_v[l], (1, C_KV_HEADS)),
                            w1k, w1v, _blockdiag2(c_w2_k[l]).astype(BF16), _blockdiag2(c_w2_v[l]).T.astype(BF16))
        yc = _nsa(ptq, gt, ksw, kc, vct, ovt, route)

        x = _tail(ya, yb, yc, x, mod, wo_bf, ln1_g[l].reshape(1, d), ln1_b[l].reshape(1, d),
                  wup_bf, conv_w[l], conv_b[l].reshape(1, -1), wdn_bf,
                  ln2_g[l].reshape(1, d), ln2_b[l].reshape(1, d), alpha)
    return x
```

```python
import functools

import numpy as np
import jax
import jax.numpy as jnp
from jax import lax
from jax.experimental import pallas as pl
from jax.experimental.pallas import tpu as pltpu

F32 = jnp.float32
BF16 = jnp.bfloat16

HEAD_DIM = 64
A_GROUPS = 4
A_WIDTH = A_GROUPS * HEAD_DIM
CHUNK = 128
B_HEADS = 4
B_WIDTH = B_HEADS * HEAD_DIM
ROPE_BASE = 10000.0
C_HEADS = 8
C_KV_HEADS = 2
C_REP = C_HEADS // C_KV_HEADS
C_WIDTH = C_HEADS * HEAD_DIM
C_KV_WIDTH = C_KV_HEADS * HEAD_DIM
CMP_BLOCK = 32
CMP_STRIDE = 16
SEL_BLOCK = 64
N_SEL = 8
WINDOW = 512
CONV_WIDTH = 3
LN_EPS = 1e-5
NEG = -1e30
BIG = 1e30
LOG2E = 1.4426950408889634

SUBLANES = 8

TOKEN_TILE = 512
FFN_CHUNK = 256
ADA_TILE = 1536
NSA_SEQS = (4, 2, 1)

PN_RET, PN_ZA, PN_KS, PN_KW, PN_KVC = 0, 1024, 1536, 1664, 1792
PN_COLS = 1536
PT_Q, PT_VS, PT_VW, PT_G = 0, 512, 640, 768
PT_ROWS = 800


def _dot(a, b):
    return jnp.dot(a, b, preferred_element_type=F32)


def _dot_nt(a, b):
    return lax.dot_general(a, b, (((1,), (1,)), ((), ())), preferred_element_type=F32)


def _dot_tn(a, b):
    return lax.dot_general(a, b, (((0,), (0,)), ((), ())), preferred_element_type=F32)


def _split_dot(x, m):
    hi = x.astype(BF16)
    lo = (x - hi.astype(F32)).astype(BF16)
    return _dot(hi, m) + _dot(lo, m)


def _split_dot_left(m, x):
    hi = x.astype(BF16)
    lo = (x - hi.astype(F32)).astype(BF16)
    return _dot(m, hi) + _dot(m, lo)


def _row_ln(x, g, b):
    mu = jnp.mean(x, axis=-1, keepdims=True)
    d = x - mu
    var = jnp.mean(d * d, axis=-1, keepdims=True)
    return d * lax.rsqrt(var + LN_EPS) * g + b


def _params(sem, vmem_mb):
    return pltpu.CompilerParams(dimension_semantics=sem, vmem_limit_bytes=vmem_mb * 1024 * 1024)


def _ada_kernel(c_ref, w_ref, b_ref, o_ref):
    cond = jax.nn.silu(c_ref[...]).astype(BF16)
    o_ref[0] = _dot(cond, w_ref[0].astype(BF16)) + b_ref[0]


def _ada_mod(c, w_ada, b_ada):
    depth, d, n = w_ada.shape
    bsz = c.shape[0]
    tn = ADA_TILE
    return pl.pallas_call(
        _ada_kernel,
        grid=(depth, n // tn),
        in_specs=[
            pl.BlockSpec((bsz, d), lambda l, j: (0, 0)),
            pl.BlockSpec((1, d, tn), lambda l, j: (l, 0, j)),
            pl.BlockSpec((1, 1, tn), lambda l, j: (l, 0, j)),
        ],
        out_specs=pl.BlockSpec((1, bsz, tn), lambda l, j: (l, 0, j)),
        out_shape=jax.ShapeDtypeStruct((depth, bsz, n), F32),
        compiler_params=_params(("arbitrary", "arbitrary"), 40),
        name="ada_mod",
    )(c, w_ada, b_ada.reshape(depth, 1, n))


def _inproj_kernel(x_ref, mod_ref, wn_ref, wt_ref, cos_ref, sin_ref, dec_ref, zeta_ref, xi_ref, cd_ref, qm_ref,
                   vm_ref, bm_ref, mavg_ref, gng_ref, gnb_ref, wcat_ref, abias_ref, alg_ref, alb_ref, gm_ref,
                   wo32_ref, wup32_ref, wdn32_ref,
                   ya_ref, yb_ref, ksw_ref, kvc_ref, ptq_ref, gt_ref, wo16_ref, wup16_ref, wdn16_ref,
                   pn_ref, state_ref, *, d, n_chunks, wdn_steps):
    @pl.when(pl.program_id(1) == 0)
    def _():
        state_ref[...] = jnp.zeros_like(state_ref)

    wo16_ref[...] = wo32_ref[...].astype(BF16)
    wup16_ref[...] = wup32_ref[...].astype(BF16)

    @pl.when(pl.program_id(0) * pl.num_programs(1) + pl.program_id(1) < wdn_steps)
    def _():
        wdn16_ref[...] = wdn32_ref[...].astype(BF16)

    sh = mod_ref[0, :, 0:d]
    sc = mod_ref[0, :, d:2 * d]
    h = (x_ref[0] * (1.0 + sc) + sh).astype(BF16)
    pn_ref[...] = _dot(h, wn_ref[:, :PN_COLS])
    src = lambda rows, lo, hi: pn_ref[rows, lo:hi]
    rest = _dot(h, wn_ref[:, PN_COLS:])
    ksw_ref[0] = rest[:, :PN_KVC - PN_COLS].astype(BF16)
    kvc_ref[0] = rest[:, PN_KVC - PN_COLS:]
    _retention_chunks(src, cos_ref, sin_ref, dec_ref, zeta_ref, xi_ref, cd_ref, qm_ref, vm_ref, bm_ref,
                      mavg_ref, gng_ref, gnb_ref, yb_ref, state_ref, n_chunks)
    pt = _dot_nt(wt_ref[...], h)
    ptq_ref[0] = pt[:PT_G, :].astype(BF16)
    gt_ref[0] = pt[PT_G:, :]
    _gmlp_chunks(src, wcat_ref, abias_ref, alg_ref, alb_ref, mavg_ref, gm_ref, ya_ref, n_chunks)


def _inproj(x, mod, wn, wt, cos, sin, ret_consts, mavg, gn_g, gn_b, wcat, abias, al_g, al_b, gm,
            layer, w_out, w_up, w_down, tm=2 * TOKEN_TILE):
    bsz, s, d = x.shape
    nt = s // tm
    steps = bsz * nt
    dff = w_down.shape[1]
    bf16_rows = 2 * SUBLANES
    slab = d // steps
    assert d % steps == 0 and slab % bf16_rows == 0
    wdn_steps = max(k for k in range(1, steps + 1) if dff % k == 0 and (dff // k) % bf16_rows == 0)
    wdn_slab = dff // wdn_steps
    const = lambda b, j: (0, 0)
    small = lambda a: pl.BlockSpec(a.shape, const)
    rows = lambda w: pl.BlockSpec((1, tm, w), lambda b, j: (b, j, 0))
    step = lambda b, j: b * nt + j
    return pl.pallas_call(
        functools.partial(_inproj_kernel, d=d, n_chunks=tm // CHUNK, wdn_steps=wdn_steps),
        grid=(bsz, nt),
        in_specs=[
            rows(d),
            pl.BlockSpec((1, 1, 6 * d), lambda b, j: (b, 0, 0)),
            small(wn), small(wt),
            pl.BlockSpec((tm, B_WIDTH // 2), lambda b, j: (j, 0)),
            pl.BlockSpec((tm, B_WIDTH // 2), lambda b, j: (j, 0)),
        ] + [small(a) for a in ret_consts] + [
            small(mavg), small(gn_g), small(gn_b), small(wcat), small(abias), small(al_g), small(al_b), small(gm),
            pl.BlockSpec((None, slab, d), lambda b, j: (layer, step(b, j), 0)),
            pl.BlockSpec((None, slab, w_up.shape[2]), lambda b, j: (layer, step(b, j), 0)),
            pl.BlockSpec((None, wdn_slab, d), lambda b, j: (layer, jnp.minimum(step(b, j), wdn_steps - 1), 0)),
        ],
        out_specs=[
            rows(A_WIDTH), rows(B_WIDTH), rows(2 * C_KV_WIDTH), rows(2 * C_KV_WIDTH),
            pl.BlockSpec((1, PT_G, tm), lambda b, j: (b, 0, j)),
            pl.BlockSpec((1, PT_ROWS - PT_G, tm), lambda b, j: (b, 0, j)),
            pl.BlockSpec((slab, d), lambda b, j: (step(b, j), 0)),
            pl.BlockSpec((slab, w_up.shape[2]), lambda b, j: (step(b, j), 0)),
            pl.BlockSpec((wdn_slab, d), lambda b, j: (jnp.minimum(step(b, j), wdn_steps - 1), 0)),
        ],
        out_shape=[
            jax.ShapeDtypeStruct((bsz, s, A_WIDTH), BF16),
            jax.ShapeDtypeStruct((bsz, s, B_WIDTH), BF16),
            jax.ShapeDtypeStruct((bsz, s, 2 * C_KV_WIDTH), BF16),
            jax.ShapeDtypeStruct((bsz, s, 2 * C_KV_WIDTH), F32),
            jax.ShapeDtypeStruct((bsz, PT_G, s), BF16),
            jax.ShapeDtypeStruct((bsz, PT_ROWS - PT_G, s), F32),
            jax.ShapeDtypeStruct(w_out.shape[1:], BF16),
            jax.ShapeDtypeStruct(w_up.shape[1:], BF16),
            jax.ShapeDtypeStruct(w_down.shape[1:], BF16),
        ],
        scratch_shapes=[
            pltpu.VMEM((tm, PN_COLS), F32),
            pltpu.VMEM((B_WIDTH, B_WIDTH), F32),
        ],
        compiler_params=_params(("arbitrary", "arbitrary"), 48),
        name="inproj_mixers",
    )(x, mod, wn, wt, cos, sin, *ret_consts, mavg, gn_g, gn_b, wcat, abias, al_g, al_b, gm, w_out, w_up, w_down)


def _gmlp_chunks(src, w_ref, bias_ref, g_ref, b_ref, mavg_ref, gm_ref, o_ref, n_chunks):
    row = lax.broadcasted_iota(jnp.int32, (CHUNK, A_GROUPS * CHUNK), 0)
    col = lax.broadcasted_iota(jnp.int32, (CHUNK, A_GROUPS * CHUNK), 1)
    wc = jnp.where((col % CHUNK) <= row, w_ref[...], 0.0).astype(BF16)
    chunks = range(n_chunks)
    rows = [slice(c * CHUNK, (c + 1) * CHUNK) for c in chunks]
    z = [jax.nn.gelu(src(r, PN_ZA, PN_ZA + 2 * A_WIDTH)) for r in rows]
    v = [zc[:, A_WIDTH:] for zc in z]
    mavg = mavg_ref[...]
    mu = [_split_dot(vc, mavg) for vc in v]
    dev = [vc - m for vc, m in zip(v, mu)]
    var = [_split_dot(d * d, mavg) for d in dev]
    vn = [d * lax.rsqrt(s2 + LN_EPS) * g_ref[...] + b_ref[...] for d, s2 in zip(dev, var)]
    vstack = [jnp.concatenate([x * gm_ref[g:g + 1, :] for g in range(A_GROUPS)], axis=0).astype(BF16) for x in vn]
    vs = [_dot(wc, x) + bias_ref[...] for x in vstack]
    for c in chunks:
        o_ref[0, rows[c], :] = (z[c][:, :A_WIDTH] * vs[c]).astype(o_ref.dtype)


def _retention_chunks(src, cos_ref, sin_ref, dec_ref, zeta_ref, xi_ref, cd_ref, qm_ref, vm_ref, bm_ref,
                      mavg_ref, g_ref, b_ref, o_ref, state_ref, n_chunks):
    half = B_WIDTH // 2
    chunks = range(n_chunks)
    rows = [slice(c * CHUNK, (c + 1) * CHUNK) for c in chunks]

    def rot(t, r):
        t1 = t[:, :half]
        t2 = t[:, half:]
        cos = cos_ref[r, :]
        sin = sin_ref[r, :]
        return jnp.concatenate([t1 * cos - t2 * sin, t1 * sin + t2 * cos], axis=1)

    qr = [rot(src(r, 0, B_WIDTH), r) for r in rows]
    kr = [rot(src(r, B_WIDTH, 2 * B_WIDTH), r) * (HEAD_DIM ** -0.5) for r in rows]
    v = [src(r, 2 * B_WIDTH, 3 * B_WIDTH) for r in rows]
    qs = [jnp.concatenate([x * qm_ref[h:h + 1, :] for h in range(B_HEADS)], axis=0).astype(BF16) for x in qr]
    s = [_dot_nt(a, b.astype(BF16)) * dec_ref[...] for a, b in zip(qs, kr)]
    kv = [_dot_tn((a * zeta_ref[...]).astype(BF16), b.astype(BF16)) * bm_ref[...]
          for a, b in zip(kr, v)]
    scat = [jnp.concatenate([x[h * CHUNK:(h + 1) * CHUNK, :] for h in range(B_HEADS)], axis=1).astype(BF16)
            for x in s]
    vstack = [jnp.concatenate([x * vm_ref[h:h + 1, :] for h in range(B_HEADS)], axis=0).astype(BF16) for x in v]
    o_inner = [_dot(a, b) for a, b in zip(scat, vstack)]
    state = state_ref[...]
    before = []
    for c in chunks:
        before.append(state)
        state = state * cd_ref[...] + kv[c]
    state_ref[...] = state
    o = [oi + _dot(a.astype(BF16), st.astype(BF16)) * xi_ref[...] for oi, a, st in zip(o_inner, qr, before)]
    mavg = mavg_ref[...]
    mu = [_split_dot(x, mavg) for x in o]
    dev = [x - m for x, m in zip(o, mu)]
    var = [_split_dot(d * d, mavg) for d in dev]
    for c in chunks:
        normed = dev[c] * lax.rsqrt(var[c] + LN_EPS) * g_ref[...] + b_ref[...]
        gate = src(rows[c], 3 * B_WIDTH, 4 * B_WIDTH)
        o_ref[0, rows[c], :] = (jax.nn.silu(gate) * normed).astype(o_ref.dtype)


def _retention_consts():
    h_n, d, l_n = B_HEADS, HEAD_DIM, CHUNK
    log_gamma = jnp.log1p(-jnp.exp2(-5.0 - jnp.arange(h_n, dtype=F32)))
    idx = jnp.arange(l_n, dtype=F32)
    diff = idx[:, None] - idx[None, :]
    decay_in = jnp.where(diff >= 0, jnp.exp(log_gamma[:, None, None] * jnp.maximum(diff, 0.0)), 0.0)
    xi = jnp.exp(log_gamma[:, None] * (idx + 1.0))
    zeta = jnp.exp(log_gamma[:, None] * (l_n - 1.0 - idx))
    chunk_decay = jnp.exp(log_gamma * l_n)
    cols = np.arange(B_WIDTH)
    head_perm = (cols % (B_WIDTH // 2)) // (d // 2)
    head_std = cols // d
    dec = decay_in.reshape(h_n * l_n, l_n)
    zeta_t = zeta.T[:, head_perm]
    xi_t = xi.T[:, head_std]
    cd = chunk_decay[head_std][None, :]
    qm = jnp.asarray((head_perm[None, :] == np.arange(h_n)[:, None]).astype(np.float32))
    vm = jnp.asarray((head_std[None, :] == np.arange(h_n)[:, None]).astype(np.float32))
    bm = jnp.asarray((head_perm[:, None] == head_std[None, :]).astype(np.float32))
    return [dec, zeta_t, xi_t, cd, qm, vm, bm]


def _rotary_tables(s):
    half = HEAD_DIM // 2
    inv = jnp.power(ROPE_BASE, -jnp.arange(half, dtype=F32) / half)
    ang = jnp.arange(s).astype(F32)[:, None] * inv[None, :]
    return jnp.tile(jnp.cos(ang), (1, B_HEADS)), jnp.tile(jnp.sin(ang), (1, B_HEADS))


def _compress_kernel(xk_ref, xv_ref, posk_ref, posv_ref, w1k_ref, w1v_ref, w2k_ref, w2vt_ref, kc_ref, vct_ref,
                     *, nseg):
    half = CMP_BLOCK // 2
    acc = [jnp.zeros((nseg, C_KV_WIDTH), F32) for _ in range(4)]
    for l in range(half):
        xk = xk_ref[0, pl.ds(l, nseg, stride=CMP_STRIDE), :]
        xv = xv_ref[0, pl.ds(l, nseg, stride=CMP_STRIDE), :]
        acc[0] += _dot((xk + posk_ref[l:l + 1, :]).astype(BF16), w1k_ref[l])
        acc[1] += _dot((xk + posk_ref[half + l:half + l + 1, :]).astype(BF16), w1k_ref[half + l])
        acc[2] += _dot((xv + posv_ref[l:l + 1, :]).astype(BF16), w1v_ref[l])
        acc[3] += _dot((xv + posv_ref[half + l:half + l + 1, :]).astype(BF16), w1v_ref[half + l])
    hk = jax.nn.gelu(acc[0] + pltpu.roll(acc[1], nseg - 1, 0))
    hv = jax.nn.gelu(acc[2] + pltpu.roll(acc[3], nseg - 1, 0))
    kc_ref[0] = _dot(hk.astype(BF16), w2k_ref[...])
    vct_ref[0] = _dot_nt(w2vt_ref[...], hv.astype(BF16))


def _compress(kvc, posk, posv, w1k, w1v, w2k, w2v):
    bsz, s, width = kvc.shape
    nseg = s // CMP_STRIDE
    c2 = lambda bi: (0, 0)
    c3 = lambda bi: (0, 0, 0)
    return pl.pallas_call(
        functools.partial(_compress_kernel, nseg=nseg),
        grid=(bsz,),
        in_specs=[
            pl.BlockSpec((1, s, C_KV_WIDTH), lambda bi: (bi, 0, 0)),
            pl.BlockSpec((1, s, C_KV_WIDTH), lambda bi: (bi, 0, 1)),
            pl.BlockSpec(posk.shape, c2),
            pl.BlockSpec(posv.shape, c2),
            pl.BlockSpec(w1k.shape, c3),
            pl.BlockSpec(w1v.shape, c3),
            pl.BlockSpec(w2k.shape, c2),
            pl.BlockSpec(w2v.shape, c2),
        ],
        out_specs=[
            pl.BlockSpec((1, nseg, C_KV_WIDTH), lambda bi: (bi, 0, 0)),
            pl.BlockSpec((1, C_KV_WIDTH, nseg), lambda bi: (bi, 0, 0)),
        ],
        out_shape=[
            jax.ShapeDtypeStruct((bsz, nseg, C_KV_WIDTH), F32),
            jax.ShapeDtypeStruct((bsz, C_KV_WIDTH, nseg), F32),
        ],
        compiler_params=_params(("arbitrary",), 32),
        name="nsa_compress",
    )(kvc, kvc, posk, posv, w1k, w1v, w2k, w2v)


def _blockdiag2(w):
    z = jnp.zeros_like(w)
    return jnp.concatenate([jnp.concatenate([w, z], axis=-1), jnp.concatenate([z, w], axis=-1)], axis=-2)


def _overlap_t(nseg, ns):
    nc = nseg - 1
    c0 = np.arange(nc)[None, :] * CMP_STRIDE
    s0 = np.arange(ns)[:, None] * SEL_BLOCK
    ov = np.clip(np.minimum(c0 + CMP_BLOCK, s0 + SEL_BLOCK) - np.maximum(c0, s0), 0, None) / CMP_BLOCK
    out = np.zeros((ns, nseg), np.float32)
    out[:, :nc] = ov
    return jnp.asarray(out, dtype=BF16)


def _nsa_kernel(qt_ref, gt_ref, ks_ref, vst_ref, kw_ref, vwt_ref, kc_ref, vct_ref, ovt_ref, e_ref, o_ref,
                qb_ref, m_ref, alpha_ref, acc_ref, res_ref, sbuf_ref, pbuf_ref, wsbuf_ref, wpbuf_ref,
                cbuf_ref, pcmp_ref, *, nseg, ns, n_sel, n_chunks, nb):
    i = pl.program_id(1)
    s0 = i * CHUNK
    width = C_HEADS * CHUNK
    blk_w = 2 * CHUNK
    n_blk = C_HEADS // 2
    blk_per_g = n_blk // C_KV_HEADS
    seqs = range(nb)

    @pl.when((pl.program_id(0) == 0) & (i == 0))
    def _():
        brow = lax.broadcasted_iota(jnp.int32, (C_KV_WIDTH, width), 0)
        for bb in seqs:
            qb_ref[bb, 0:C_KV_WIDTH, :] = jnp.zeros((C_KV_WIDTH, width), BF16)
            qb_ref[bb, 2 * C_KV_WIDTH:, :] = jnp.where(brow == 1, NEG, 0.0).astype(BF16)

    zero = jnp.zeros((HEAD_DIM, CHUNK), BF16)
    for bb in seqs:
        for h in range(C_HEADS):
            qh = qt_ref[bb, h * HEAD_DIM:(h + 1) * HEAD_DIM, :]
            blk = jnp.concatenate([qh, zero] if h < C_REP else [zero, qh], axis=0)
            qb_ref[bb, C_KV_WIDTH:2 * C_KV_WIDTH, h * CHUNK:(h + 1) * CHUNK] = blk
    ones_tile = jnp.where(lax.broadcasted_iota(jnp.int32, (2 * SUBLANES, CHUNK), 0) == 0, 1.0, 0.0).astype(BF16)

    krow = lax.broadcasted_iota(jnp.int32, (CHUNK, blk_w), 0)
    t_loc = lax.broadcasted_iota(jnp.int32, (CHUNK, blk_w), 1) % CHUNK
    tri_diag = jnp.where(krow <= t_loc, 0.0, NEG)
    tri_old = jnp.where(krow > t_loc, 0.0, NEG)

    def cmp_scores():
        for bb in seqs:
            cbuf_ref[bb] = _dot(kc_ref[bb].astype(BF16), qb_ref[bb, C_KV_WIDTH:2 * C_KV_WIDTH, :])

    def cmp_softmax():
        crow = lax.broadcasted_iota(jnp.int32, (nseg, blk_w), 0)
        c_t = s0 + lax.broadcasted_iota(jnp.int32, (nseg, blk_w), 1) % CHUNK
        cbias = jnp.where(crow * CMP_STRIDE + (CMP_BLOCK - 1) <= c_t, 0.0, NEG)
        imps = []
        for bb in seqs:
            psum = [None] * C_KV_HEADS
            for b in range(n_blk):
                g = b // blk_per_g
                cols = slice(b * blk_w, (b + 1) * blk_w)
                sc = cbuf_ref[bb, :, cols] + cbias
                mx = jnp.max(sc, axis=0, keepdims=True)
                e = jnp.exp2(sc - mx)
                den = jnp.sum(e, axis=0, keepdims=True)
                p = e * jnp.where(mx > 0.5 * NEG, 1.0 / den, 0.0)
                pcmp_ref[bb, :, cols] = p.astype(BF16)
                both = p[:, :CHUNK] + p[:, CHUNK:]
                psum[g] = both if psum[g] is None else psum[g] + both
            imps.append([_split_dot_left(ovt_ref[...], ps) for ps in psum])
        return imps

    def cmp_output():
        for bb in seqs:
            vct = vct_ref[bb].astype(BF16)
            for b in range(n_blk):
                g = b // blk_per_g
                cols = slice(b * blk_w, (b + 1) * blk_w)
                res_ref[bb, :, cols] = _dot(vct[g * HEAD_DIM:(g + 1) * HEAD_DIM, :], pcmp_ref[bb, :, cols])

    def select_blocks(all_imps):
        j = lax.broadcasted_iota(jnp.int32, (ns, CHUNK), 0)
        cur = (s0 + lax.broadcasted_iota(jnp.int32, (ns, CHUNK), 1)) // SEL_BLOCK
        forced = (j == 0) | (j == cur) | (j == cur - 1)
        future = j > cur
        slab_rows = lax.broadcasted_iota(jnp.int32, (SUBLANES, CHUNK), 0)
        for bb, g in [(bb, g) for bb in seqs for g in range(C_KV_HEADS)]:
            imps = all_imps[bb]
            imp = jnp.where(forced, BIG, jnp.where(future, NEG, imps[g]))
            slabs = [imp[v * SUBLANES:(v + 1) * SUBLANES, :] for v in range(ns // SUBLANES)]
            ranks = [jnp.zeros((SUBLANES, CHUNK), F32) for _ in slabs]
            for i2 in range(ns):
                r_i = imp[i2:i2 + 1, :]
                for v, slab in enumerate(slabs):
                    if (v + 1) * SUBLANES - 1 <= i2:
                        ranks[v] = ranks[v] + jnp.where(r_i > slab, 1.0, 0.0)
                    elif v * SUBLANES > i2:
                        ranks[v] = ranks[v] + jnp.where(r_i >= slab, 1.0, 0.0)
                    else:
                        tie = jnp.where(slab_rows > i2 - v * SUBLANES, 1.0, 0.0)
                        ranks[v] = ranks[v] + jnp.where(r_i > slab, 1.0, 0.0) + jnp.where(r_i == slab, tie, 0.0)
            rank = jnp.concatenate(ranks, axis=0)
            sel_bias = jnp.where(rank < n_sel, 0.0, NEG).astype(BF16)
            for r in range(C_REP):
                h = g * C_REP + r
                qb_ref[bb, 0:ns, h * CHUNK:(h + 1) * CHUNK] = sel_bias

    bufs = {0: (wsbuf_ref, wpbuf_ref), 1: (sbuf_ref, pbuf_ref)}

    def scores(br, k_ref, chunks, tiles, slot):
        sb, _ = bufs[br]
        for bb in seqs:
            parts = []
            for c, t in zip(chunks, tiles):
                kch = k_ref[bb, pl.ds(pl.multiple_of(c * CHUNK, CHUNK), CHUNK), :]
                parts.append(jnp.concatenate([kch, e_ref[t]] if br == 0 else [e_ref[t], kch], axis=1))
            keys = parts[0] if len(parts) == 1 else jnp.concatenate(parts, axis=0)
            if br == 0:
                rhs = qb_ref[bb, C_KV_WIDTH:3 * C_KV_WIDTH, :]
            else:
                rhs = qb_ref[bb, 0:2 * C_KV_WIDTH, :]
            sb[bb, slot, 0:len(parts) * CHUNK] = _dot(keys, rhs)

    def softmax(br, slot, tris):
        sb, pb = bufs[br]
        for bb in seqs:
            for b in range(n_blk):
                cols = slice(b * blk_w, (b + 1) * blk_w)
                parts = []
                for h, tri in enumerate(tris):
                    s = sb[bb, slot, h * CHUNK:(h + 1) * CHUNK, cols]
                    parts.append(s if tri is None else s + tri)
                m_old = m_ref[bb, br, :, cols]
                m_new = m_old
                for s in parts:
                    m_new = jnp.maximum(m_new, jnp.max(s, axis=0, keepdims=True))
                alpha_ref[bb, br, :, cols] = jnp.exp2(m_old - m_new)
                for h, s in enumerate(parts):
                    pb[bb, h * CHUNK:(h + 1) * CHUNK, cols] = jnp.exp2(s - m_new).astype(BF16)
                m_ref[bb, br, :, cols] = m_new

    def accumulate(br, vt_ref, chunks, first=False):
        _, pb = bufs[br]
        n = len(chunks)
        ones = ones_tile if n == 1 else jnp.concatenate([ones_tile] * n, axis=1)
        for bb in seqs:
            vts = [vt_ref[bb, :, pl.ds(pl.multiple_of(c * CHUNK, CHUNK), CHUNK)] for c in chunks]
            vt = vts[0] if n == 1 else jnp.concatenate(vts, axis=1)
            vone = [jnp.concatenate([vt[g * HEAD_DIM:(g + 1) * HEAD_DIM, :], ones], axis=0)
                    for g in range(C_KV_HEADS)]
            for b in range(n_blk):
                cols = slice(b * blk_w, (b + 1) * blk_w)
                pv = _dot(vone[b // blk_per_g], pb[bb, 0:n * CHUNK, cols])
                acc_ref[bb, br, :, cols] = (pv if first else
                                            alpha_ref[bb, br, :, cols] * acc_ref[bb, br, :, cols] + pv)

    n_back = WINDOW // CHUNK
    chunk = [jnp.maximum(i - n_back + w, 0) for w in range(n_back + 1)]
    tile = [jnp.where(i - n_back + w < 0, n_chunks + 1, n_chunks) for w in range(n_back + 1)]

    cmp_scores()
    m_ref[:, 0] = jnp.full((nb, 1, width), NEG, F32)
    scores(0, kw_ref, chunk[0:1], tile[0:1], 0)
    imps = cmp_softmax()
    softmax(0, 0, [tri_old])
    scores(0, kw_ref, chunk[1:3], tile[1:3], 1)
    accumulate(0, vwt_ref, chunk[0:1], first=True)
    softmax(0, 1, [None, None])
    scores(0, kw_ref, chunk[3:5], tile[3:5], 0)
    select_blocks(imps)

    m_ref[:, 1] = jnp.full((nb, 1, width), NEG, F32)
    alpha_ref[:, 1] = jnp.ones((nb, 1, width), F32)
    for bb in seqs:
        acc_ref[bb, 1] = jnp.zeros(acc_ref.shape[2:], F32)
        pbuf_ref[bb] = jnp.zeros(pbuf_ref.shape[1:], BF16)
    scores(1, ks_ref, [0], [0], 0)

    def sel_stage(k, slot, slot_next):
        accumulate(1, vst_ref, [jnp.maximum(k - 1, 0)])
        softmax(1, slot, [None])
        scores(1, ks_ref, [k + 1], [k + 1], slot_next)

    def sel_pair(kk, carry):
        sel_stage(2 * kk, 0, 1)
        sel_stage(2 * kk + 1, 1, 0)
        return carry

    lax.fori_loop(0, i // 2, sel_pair, 0)

    @pl.when(i % 2 == 1)
    def _():
        sel_stage(i - 1, 0, 1)

    accumulate(1, vst_ref, [jnp.maximum(i - 1, 0)])
    accumulate(0, vwt_ref, chunk[1:3])
    softmax(1, i % 2, [tri_diag])
    softmax(0, 0, [None, tri_diag])
    accumulate(1, vst_ref, [i])
    accumulate(0, vwt_ref, chunk[3:5])
    cmp_output()

    for bb in seqs:
        gates = jax.nn.sigmoid(gt_ref[bb])
        den = [acc_ref[bb, br, HEAD_DIM:HEAD_DIM + 1, :] for br in (0, 1)]
        inv = [jnp.where(x > 0.0, 1.0 / x, 0.0) for x in den]
        for pair in range(C_HEADS // 2):
            pieces = []
            for h in (2 * pair, 2 * pair + 1):
                cs = slice(h * CHUNK, (h + 1) * CHUNK)
                pieces.append(gates[3 * h:3 * h + 1, :] * res_ref[bb, :, cs]
                              + (gates[3 * h + 1:3 * h + 2, :] * inv[1][:, cs]) * acc_ref[bb, 1, 0:HEAD_DIM, cs]
                              + (gates[3 * h + 2:3 * h + 3, :] * inv[0][:, cs]) * acc_ref[bb, 0, 0:HEAD_DIM, cs])
            o_ref[bb, :, pair * 2 * HEAD_DIM:(pair + 1) * 2 * HEAD_DIM] = (
                jnp.concatenate(pieces, axis=0).T.astype(o_ref.dtype))


def _route_tiles(n_chunks, ns):
    e = np.zeros((n_chunks + 2, CHUNK, C_KV_WIDTH), np.float32)
    r = np.arange(CHUNK)
    for c in range(n_chunks):
        e[c, r, 2 * c + r // SEL_BLOCK] = 1.0
    e[n_chunks, :, 0] = 1.0
    e[n_chunks + 1, :, 1] = 1.0
    return jnp.asarray(e, dtype=BF16)


def _nsa(ptq, gt, ksw, kc, vct, ovt, route):
    bsz, s, _ = ksw.shape
    nseg = s // CMP_STRIDE
    ns = s // SEL_BLOCK
    n_chunks = s // CHUNK
    width = C_HEADS * CHUNK
    nb = next(n for n in NSA_SEQS if bsz % n == 0)
    return pl.pallas_call(
        functools.partial(_nsa_kernel, nseg=nseg, ns=ns, n_sel=min(N_SEL, ns), n_chunks=n_chunks, nb=nb),
        grid=(bsz // nb, n_chunks),
        in_specs=[
            pl.BlockSpec((nb, C_WIDTH, CHUNK), lambda b, i: (b, PT_Q // C_WIDTH, i)),
            pl.BlockSpec((nb, gt.shape[1], CHUNK), lambda b, i: (b, 0, i)),
            pl.BlockSpec((nb, s, C_KV_WIDTH), lambda b, i: (b, 0, 0)),
            pl.BlockSpec((nb, C_KV_WIDTH, s), lambda b, i: (b, PT_VS // C_KV_WIDTH, 0)),
            pl.BlockSpec((nb, s, C_KV_WIDTH), lambda b, i: (b, 0, 1)),
            pl.BlockSpec((nb, C_KV_WIDTH, s), lambda b, i: (b, PT_VW // C_KV_WIDTH, 0)),
            pl.BlockSpec((nb, nseg, C_KV_WIDTH), lambda b, i: (b, 0, 0)),
            pl.BlockSpec((nb, C_KV_WIDTH, nseg), lambda b, i: (b, 0, 0)),
            pl.BlockSpec(ovt.shape, lambda b, i: (0, 0)),
            pl.BlockSpec(route.shape, lambda b, i: (0, 0, 0)),
        ],
        out_specs=pl.BlockSpec((nb, CHUNK, C_WIDTH), lambda b, i: (b, i, 0)),
        out_shape=jax.ShapeDtypeStruct((bsz, s, C_WIDTH), BF16),
        scratch_shapes=[
            pltpu.VMEM((nb, 3 * C_KV_WIDTH, width), BF16),
            pltpu.VMEM((nb, 2, 1, width), F32),
            pltpu.VMEM((nb, 2, 1, width), F32),
            pltpu.VMEM((nb, 2, HEAD_DIM + 2 * SUBLANES, width), F32),
            pltpu.VMEM((nb, HEAD_DIM, width), F32),
            pltpu.VMEM((nb, 2, CHUNK, width), F32),
            pltpu.VMEM((nb, CHUNK, width), BF16),
            pltpu.VMEM((nb, 2, 2 * CHUNK, width), F32),
            pltpu.VMEM((nb, 2 * CHUNK, width), BF16),
            pltpu.VMEM((nb, nseg, width), F32),
            pltpu.VMEM((nb, nseg, width), BF16),
        ],
        compiler_params=_params(("arbitrary", "arbitrary"), 48),
        name="nsa_attention",
    )(ptq, gt, ksw, ptq, ksw, ptq, kc, vct, ovt, route)


def _tail_kernel(ya_ref, yb_ref, yc_ref, x_ref, mod_ref, wo_ref, g1_ref, b1_ref,
                 wup_ref, cw_ref, cb_ref, wdn_ref, g2_ref, b2_ref, o_ref,
                 work_ref, carry_ref, x1_ref, h_ref, act_ref, *, d, dff, cwid, tm, alpha):
    @pl.when(pl.program_id(1) == 0)
    def _():
        carry_ref[...] = jnp.zeros_like(carry_ref)

    y1 = _dot(ya_ref[0], wo_ref[0:A_WIDTH, :])
    y1 += _dot(yb_ref[0], wo_ref[A_WIDTH:A_WIDTH + B_WIDTH, :])
    y1 += _dot(yc_ref[0], wo_ref[A_WIDTH + B_WIDTH:, :])
    x1 = _row_ln(alpha * x_ref[0] + mod_ref[0, :, 2 * d:3 * d] * y1, g1_ref[...], b1_ref[...])
    x1_ref[...] = x1

    sh = mod_ref[0, :, 3 * d:4 * d]
    sc = mod_ref[0, :, 4 * d:5 * d]
    gate = mod_ref[0, :, 5 * d:6 * d]
    h_ref[...] = (x1 * (1.0 + sc) + sh).astype(BF16)
    pad = SUBLANES
    n_chunks = dff // cwid

    def up(ci):
        slot = ci % 2
        for part in range(2):
            cs = slice(part * dff + ci * cwid, part * dff + (ci + 1) * cwid)
            work_ref[slot, part, 0:pad, :] = carry_ref[:, cs]
            work_ref[slot, part, pad:pad + tm, :] = _dot(h_ref[...], wup_ref[:, cs])

    def conv_act(ci):
        slot = ci % 2
        conv = []
        for part in range(2):
            cs = slice(part * dff + ci * cwid, part * dff + (ci + 1) * cwid)
            buf = work_ref[slot, part]
            carry_ref[:, cs] = buf[tm:tm + pad, :]
            a = buf[pad:, :]
            a1 = pltpu.roll(buf, 1, 0)[pad:, :]
            a2 = pltpu.roll(buf, 2, 0)[pad:, :]
            conv.append(cw_ref[0:1, cs] * a2 + cw_ref[1:2, cs] * a1 + cw_ref[2:3, cs] * a + cb_ref[:, cs])
        half = 0.5 * conv[0]
        silu = half + half * jnp.tanh(half)
        act_ref[:, ci * cwid:(ci + 1) * cwid] = (silu * conv[1]).astype(BF16)

    split = (n_chunks // 2) * cwid
    up(0)
    y = None
    for ci in range(n_chunks):
        if ci + 1 < n_chunks:
            up(ci + 1)
        conv_act(ci)
        if (ci + 1) * cwid == split:
            y = _dot(act_ref[:, :split], wdn_ref[:split, :])
    y = y + _dot(act_ref[:, split:], wdn_ref[split:, :])
    o_ref[0] = _row_ln(alpha * x1_ref[...] + gate * y, g2_ref[...], b2_ref[...])


def _tail(ya, yb, yc, x, mod, wo, g1, b1, wup, cw, cb, wdn, g2, b2, alpha, tm=TOKEN_TILE, cwid=FFN_CHUNK):
    bsz, s, d = x.shape
    dff = wdn.shape[0]
    const = lambda bi, j: (0, 0)
    tile = lambda w: pl.BlockSpec((1, tm, w), lambda bi, j: (bi, j, 0))
    weight = lambda w: pl.BlockSpec(w.shape, const, pipeline_mode=pl.Buffered(1))
    small = lambda a: pl.BlockSpec(a.shape, const)
    return pl.pallas_call(
        functools.partial(_tail_kernel, d=d, dff=dff, cwid=cwid, tm=tm, alpha=alpha),
        grid=(bsz, s // tm),
        in_specs=[
            tile(A_WIDTH), tile(B_WIDTH), tile(C_WIDTH), tile(d),
            pl.BlockSpec((1, 1, 6 * d), lambda bi, j: (bi, 0, 0)),
            weight(wo), small(g1), small(b1),
            weight(wup), small(cw), small(cb), weight(wdn), small(g2), small(b2),
        ],
        out_specs=tile(d),
        out_shape=jax.ShapeDtypeStruct((bsz, s, d), F32),
        scratch_shapes=[
            pltpu.VMEM((2, 2, tm + SUBLANES, cwid), F32),
            pltpu.VMEM((SUBLANES, 2 * dff), F32),
            pltpu.VMEM((tm, d), F32),
            pltpu.VMEM((tm, d), BF16),
            pltpu.VMEM((tm, dff), BF16),
        ],
        compiler_params=_params(("arbitrary", "arbitrary"), 52),
        name="layer_tail",
    )(ya, yb, yc, x, mod, wo, g1, b1, wup, cw, cb, wdn, g2, b2)


def _inproj_weights(w):
    cols = np.arange(B_WIDTH)
    half = cols // (B_WIDTH // 2)
    perm = ((cols % (B_WIDTH // 2)) // (HEAD_DIM // 2)) * HEAD_DIM + half * (HEAD_DIM // 2) + cols % (HEAD_DIM // 2)
    scale = np.ones((1, w.shape[1]), np.float32)
    scale[:, 1536:2048] = HEAD_DIM ** -0.5 * LOG2E
    w = (w * scale).astype(BF16)
    za = w[:, 0:512]
    qb, kb, vb, gb = (w[:, 512 + i * B_WIDTH:512 + (i + 1) * B_WIDTH] for i in range(4))
    qc = w[:, 1536:2048]
    kcm, vcm, ksl, vsl, kwn, vwn = (w[:, 2048 + i * C_KV_WIDTH:2048 + (i + 1) * C_KV_WIDTH] for i in range(6))
    gc = w[:, 2816:2840]
    wn = jnp.concatenate([qb[:, perm], kb[:, perm], vb, gb, za, ksl, kwn, kcm, vcm], axis=1)
    pad = jnp.zeros((w.shape[0], PT_ROWS - PT_G - gc.shape[1]), w.dtype)
    wt = jnp.concatenate([qc, vsl, vwn, gc, pad], axis=1).T
    return wn, wt


def kernel(x, c, w_ada, b_ada, w_in, a_ln_g, a_ln_b, a_ws, a_bs, b_gn_g, b_gn_b, c_pos_k, c_w1_k, c_w2_k,
           c_pos_v, c_w1_v, c_w2_v, w_out, ln1_g, ln1_b, w_up, conv_w, conv_b, w_down, ln2_g, ln2_b):
    depth = w_in.shape[0]
    bsz, s, d = x.shape
    alpha = (2 * depth) ** 0.25
    nseg = s // CMP_STRIDE
    ns = s // SEL_BLOCK

    lanes = np.arange(A_WIDTH)
    mavg = jnp.asarray((lanes[:, None] // HEAD_DIM == lanes[None, :] // HEAD_DIM) / HEAD_DIM, dtype=BF16)
    gm = jnp.asarray((lanes[None, :] // HEAD_DIM == np.arange(A_GROUPS)[:, None]).astype(np.float32))
    ret_consts = _retention_consts()
    cos, sin = _rotary_tables(s)
    ovt = _overlap_t(nseg, ns)
    route = _route_tiles(s // CHUNK, ns)

    mods = _ada_mod(c, w_ada, b_ada)
    for l in range(depth):
        mod = mods[l][:, None, :]
        wn, wt = _inproj_weights(w_in[l])
        wcat = jnp.transpose(a_ws[l], (1, 0, 2)).reshape(CHUNK, A_GROUPS * CHUNK)
        bias = jnp.repeat(a_bs[l].T, HEAD_DIM, axis=1)
        ya, yb, ksw, kvc, ptq, gt, wo_bf, wup_bf, wdn_bf = _inproj(
            x, mod, wn, wt, cos, sin, ret_consts, mavg, b_gn_g[l].reshape(1, B_WIDTH), b_gn_b[l].reshape(1, B_WIDTH),
            wcat, bias, a_ln_g[l].reshape(1, A_WIDTH), a_ln_b[l].reshape(1, A_WIDTH), gm, l, w_out, w_up, w_down)

        w1k = _blockdiag2(c_w1_k[l].reshape(CMP_BLOCK, HEAD_DIM, HEAD_DIM)).astype(BF16)
        w1v = _blockdiag2(c_w1_v[l].reshape(CMP_BLOCK, HEAD_DIM, HEAD_DIM)).astype(BF16)
        kc, vct = _compress(kvc, jnp.tile(c_pos_k[l], (1, C_KV_HEADS)), jnp.tile(c_pos_v[l], (1, C_KV_HEADS)),
                            w1k, w1v, _blockdiag2(c_w2_k[l]).astype(BF16), _blockdiag2(c_w2_v[l]).T.astype(BF16))
        yc = _nsa(ptq, gt, ksw, kc, vct, ovt, route)

        x = _tail(ya, yb, yc, x, mod, wo_bf, ln1_g[l].reshape(1, d), ln1_b[l].reshape(1, d),
                  wup_bf, conv_w[l], conv_b[l].reshape(1, -1), wdn_bf,
                  ln2_g[l].reshape(1, d), ln2_b[l].reshape(1, d), alpha)
    return x
```

```python
import functools

import numpy as np
import jax
import jax.numpy as jnp
from jax import lax
from jax.experimental import pallas as pl
from jax.experimental.pallas import tpu as pltpu

F32 = jnp.float32
BF16 = jnp.bfloat16

HEAD_DIM = 64
A_GROUPS = 4
A_WIDTH = A_GROUPS * HEAD_DIM
CHUNK = 128
B_HEADS = 4
B_WIDTH = B_HEADS * HEAD_DIM
ROPE_BASE = 10000.0
C_HEADS = 8
C_KV_HEADS = 2
C_REP = C_HEADS // C_KV_HEADS
C_WIDTH = C_HEADS * HEAD_DIM
C_KV_WIDTH = C_KV_HEADS * HEAD_DIM
CMP_BLOCK = 32
CMP_STRIDE = 16
SEL_BLOCK = 64
N_SEL = 8
WINDOW = 512
CONV_WIDTH = 3
LN_EPS = 1e-5
NEG = -1e30
BIG = 1e30
LOG2E = 1.4426950408889634

SUBLANES = 8

TOKEN_TILE = 512
FFN_CHUNK = 256
ADA_TILE = 1536
NSA_SEQS = (4, 2, 1)

PN_RET, PN_ZA, PN_KS, PN_KW, PN_KVC = 0, 1024, 1536, 1664, 1792
PN_COLS = 1536
PT_Q, PT_VS, PT_VW, PT_G = 0, 512, 640, 768
PT_ROWS = 800


def _dot(a, b):
    return jnp.dot(a, b, preferred_element_type=F32)


def _dot_nt(a, b):
    return lax.dot_general(a, b, (((1,), (1,)), ((), ())), preferred_element_type=F32)


def _dot_tn(a, b):
    return lax.dot_general(a, b, (((0,), (0,)), ((), ())), preferred_element_type=F32)


def _split_dot(x, m):
    hi = x.astype(BF16)
    lo = (x - hi.astype(F32)).astype(BF16)
    return _dot(hi, m) + _dot(lo, m)


def _split_dot_left(m, x):
    hi = x.astype(BF16)
    lo = (x - hi.astype(F32)).astype(BF16)
    return _dot(m, hi) + _dot(m, lo)


def _row_ln(x, g, b):
    mu = jnp.mean(x, axis=-1, keepdims=True)
    d = x - mu
    var = jnp.mean(d * d, axis=-1, keepdims=True)
    return d * lax.rsqrt(var + LN_EPS) * g + b


def _params(sem, vmem_mb):
    return pltpu.CompilerParams(dimension_semantics=sem, vmem_limit_bytes=vmem_mb * 1024 * 1024)


def _ada_kernel(c_ref, w_ref, b_ref, o_ref):
    cond = jax.nn.silu(c_ref[...]).astype(BF16)
    o_ref[0] = _dot(cond, w_ref[0].astype(BF16)) + b_ref[0]


def _ada_mod(c, w_ada, b_ada):
    depth, d, n = w_ada.shape
    bsz = c.shape[0]
    tn = ADA_TILE
    return pl.pallas_call(
        _ada_kernel,
        grid=(depth, n // tn),
        in_specs=[
            pl.BlockSpec((bsz, d), lambda l, j: (0, 0)),
            pl.BlockSpec((1, d, tn), lambda l, j: (l, 0, j)),
            pl.BlockSpec((1, 1, tn), lambda l, j: (l, 0, j)),
        ],
        out_specs=pl.BlockSpec((1, bsz, tn), lambda l, j: (l, 0, j)),
        out_shape=jax.ShapeDtypeStruct((depth, bsz, n), F32),
        compiler_params=_params(("arbitrary", "arbitrary"), 40),
        name="ada_mod",
    )(c, w_ada, b_ada.reshape(depth, 1, n))


def _inproj_kernel(x_ref, mod_ref, wn_ref, wt_ref, cos_ref, sin_ref, dec_ref, zeta_ref, xi_ref, cd_ref, qm_ref,
                   vm_ref, bm_ref, mavg_ref, gng_ref, gnb_ref, wcat_ref, abias_ref, alg_ref, alb_ref, gm_ref,
                   wo32_ref, wup32_ref, wdn32_ref,
                   ya_ref, yb_ref, ksw_ref, kvc_ref, ptq_ref, gt_ref, wo16_ref, wup16_ref, wdn16_ref,
                   pn_ref, state_ref, *, d, n_chunks, wdn_steps):
    @pl.when(pl.program_id(1) == 0)
    def _():
        state_ref[...] = jnp.zeros_like(state_ref)

    wo16_ref[...] = wo32_ref[...].astype(BF16)
    wup16_ref[...] = wup32_ref[...].astype(BF16)

    @pl.when(pl.program_id(0) * pl.num_programs(1) + pl.program_id(1) < wdn_steps)
    def _():
        wdn16_ref[...] = wdn32_ref[...].astype(BF16)

    sh = mod_ref[0, :, 0:d]
    sc = mod_ref[0, :, d:2 * d]
    h = (x_ref[0] * (1.0 + sc) + sh).astype(BF16)
    pn_ref[...] = _dot(h, wn_ref[:, :PN_COLS])
    src = lambda rows, lo, hi: pn_ref[rows, lo:hi]
    rest = _dot(h, wn_ref[:, PN_COLS:])
    ksw_ref[0] = rest[:, :PN_KVC - PN_COLS].astype(BF16)
    kvc_ref[0] = rest[:, PN_KVC - PN_COLS:]
    _retention_chunks(src, cos_ref, sin_ref, dec_ref, zeta_ref, xi_ref, cd_ref, qm_ref, vm_ref, bm_ref,
                      mavg_ref, gng_ref, gnb_ref, yb_ref, state_ref, n_chunks)
    pt = _dot_nt(wt_ref[...], h)
    ptq_ref[0] = pt[:PT_G, :].astype(BF16)
    gt_ref[0] = pt[PT_G:, :]
    _gmlp_chunks(src, wcat_ref, abias_ref, alg_ref, alb_ref, mavg_ref, gm_ref, ya_ref, n_chunks)


def _inproj(x, mod, wn, wt, cos, sin, ret_consts, mavg, gn_g, gn_b, wcat, abias, al_g, al_b, gm,
            layer, w_out, w_up, w_down, tm=2 * TOKEN_TILE):
    bsz, s, d = x.shape
    nt = s // tm
    steps = bsz * nt
    dff = w_down.shape[1]
    bf16_rows = 2 * SUBLANES
    slab = d // steps
    assert d % steps == 0 and slab % bf16_rows == 0
    wdn_steps = max(k for k in range(1, steps + 1) if dff % k == 0 and (dff // k) % bf16_rows == 0)
    wdn_slab = dff // wdn_steps
    const = lambda b, j: (0, 0)
    small = lambda a: pl.BlockSpec(a.shape, const)
    rows = lambda w: pl.BlockSpec((1, tm, w), lambda b, j: (b, j, 0))
    step = lambda b, j: b * nt + j
    layered = lambda a: pl.BlockSpec((None,) + a.shape[1:], lambda b, j: (layer,) + (0,) * (a.ndim - 1))
    return pl.pallas_call(
        functools.partial(_inproj_kernel, d=d, n_chunks=tm // CHUNK, wdn_steps=wdn_steps),
        grid=(bsz, nt),
        in_specs=[
            rows(d),
            pl.BlockSpec((None, 1, 1, 6 * d), lambda b, j: (layer, b, 0, 0)),
            small(wn), small(wt),
            pl.BlockSpec((tm, B_WIDTH // 2), lambda b, j: (j, 0)),
            pl.BlockSpec((tm, B_WIDTH // 2), lambda b, j: (j, 0)),
        ] + [small(a) for a in ret_consts] + [
            small(mavg), layered(gn_g), layered(gn_b), small(wcat), small(abias), layered(al_g), layered(al_b),
            small(gm),
            pl.BlockSpec((None, slab, d), lambda b, j: (layer, step(b, j), 0)),
            pl.BlockSpec((None, slab, w_up.shape[2]), lambda b, j: (layer, step(b, j), 0)),
            pl.BlockSpec((None, wdn_slab, d), lambda b, j: (layer, jnp.minimum(step(b, j), wdn_steps - 1), 0)),
        ],
        out_specs=[
            rows(A_WIDTH), rows(B_WIDTH), rows(2 * C_KV_WIDTH), rows(2 * C_KV_WIDTH),
            pl.BlockSpec((1, PT_G, tm), lambda b, j: (b, 0, j)),
            pl.BlockSpec((1, PT_ROWS - PT_G, tm), lambda b, j: (b, 0, j)),
            pl.BlockSpec((slab, d), lambda b, j: (step(b, j), 0)),
            pl.BlockSpec((slab, w_up.shape[2]), lambda b, j: (step(b, j), 0)),
            pl.BlockSpec((wdn_slab, d), lambda b, j: (jnp.minimum(step(b, j), wdn_steps - 1), 0)),
        ],
        out_shape=[
            jax.ShapeDtypeStruct((bsz, s, A_WIDTH), BF16),
            jax.ShapeDtypeStruct((bsz, s, B_WIDTH), BF16),
            jax.ShapeDtypeStruct((bsz, s, 2 * C_KV_WIDTH), BF16),
            jax.ShapeDtypeStruct((bsz, s, 2 * C_KV_WIDTH), F32),
            jax.ShapeDtypeStruct((bsz, PT_G, s), BF16),
            jax.ShapeDtypeStruct((bsz, PT_ROWS - PT_G, s), F32),
            jax.ShapeDtypeStruct(w_out.shape[1:], BF16),
            jax.ShapeDtypeStruct(w_up.shape[1:], BF16),
            jax.ShapeDtypeStruct(w_down.shape[1:], BF16),
        ],
        scratch_shapes=[
            pltpu.VMEM((tm, PN_COLS), F32),
            pltpu.VMEM((B_WIDTH, B_WIDTH), F32),
        ],
        compiler_params=_params(("arbitrary", "arbitrary"), 48),
        name="inproj_mixers",
    )(x, mod, wn, wt, cos, sin, *ret_consts, mavg, gn_g, gn_b, wcat, abias, al_g, al_b, gm, w_out, w_up, w_down)


def _gmlp_chunks(src, w_ref, bias_ref, g_ref, b_ref, mavg_ref, gm_ref, o_ref, n_chunks):
    row = lax.broadcasted_iota(jnp.int32, (CHUNK, A_GROUPS * CHUNK), 0)
    col = lax.broadcasted_iota(jnp.int32, (CHUNK, A_GROUPS * CHUNK), 1)
    wc = jnp.where((col % CHUNK) <= row, w_ref[...], 0.0).astype(BF16)
    chunks = range(n_chunks)
    rows = [slice(c * CHUNK, (c + 1) * CHUNK) for c in chunks]
    z = [jax.nn.gelu(src(r, PN_ZA, PN_ZA + 2 * A_WIDTH)) for r in rows]
    v = [zc[:, A_WIDTH:] for zc in z]
    mavg = mavg_ref[...]
    mu = [_split_dot(vc, mavg) for vc in v]
    dev = [vc - m for vc, m in zip(v, mu)]
    var = [_split_dot(d * d, mavg) for d in dev]
    vn = [d * lax.rsqrt(s2 + LN_EPS) * g_ref[...] + b_ref[...] for d, s2 in zip(dev, var)]
    vstack = [jnp.concatenate([x * gm_ref[g:g + 1, :] for g in range(A_GROUPS)], axis=0).astype(BF16) for x in vn]
    vs = [_dot(wc, x) + bias_ref[...] for x in vstack]
    for c in chunks:
        o_ref[0, rows[c], :] = (z[c][:, :A_WIDTH] * vs[c]).astype(o_ref.dtype)


def _retention_chunks(src, cos_ref, sin_ref, dec_ref, zeta_ref, xi_ref, cd_ref, qm_ref, vm_ref, bm_ref,
                      mavg_ref, g_ref, b_ref, o_ref, state_ref, n_chunks):
    half = B_WIDTH // 2
    chunks = range(n_chunks)
    rows = [slice(c * CHUNK, (c + 1) * CHUNK) for c in chunks]

    def rot(t, r):
        t1 = t[:, :half]
        t2 = t[:, half:]
        cos = cos_ref[r, :]
        sin = sin_ref[r, :]
        return jnp.concatenate([t1 * cos - t2 * sin, t1 * sin + t2 * cos], axis=1)

    qr = [rot(src(r, 0, B_WIDTH), r) for r in rows]
    kr = [rot(src(r, B_WIDTH, 2 * B_WIDTH), r) * (HEAD_DIM ** -0.5) for r in rows]
    v = [src(r, 2 * B_WIDTH, 3 * B_WIDTH) for r in rows]
    qs = [jnp.concatenate([x * qm_ref[h:h + 1, :] for h in range(B_HEADS)], axis=0).astype(BF16) for x in qr]
    s = [_dot_nt(a, b.astype(BF16)) * dec_ref[...] for a, b in zip(qs, kr)]
    kv = [_dot_tn((a * zeta_ref[...]).astype(BF16), b.astype(BF16)) * bm_ref[...]
          for a, b in zip(kr, v)]
    scat = [jnp.concatenate([x[h * CHUNK:(h + 1) * CHUNK, :] for h in range(B_HEADS)], axis=1).astype(BF16)
            for x in s]
    vstack = [jnp.concatenate([x * vm_ref[h:h + 1, :] for h in range(B_HEADS)], axis=0).astype(BF16) for x in v]
    o_inner = [_dot(a, b) for a, b in zip(scat, vstack)]
    state = state_ref[...]
    before = []
    for c in chunks:
        before.append(state)
        state = state * cd_ref[...] + kv[c]
    state_ref[...] = state
    o = [oi + _dot(a.astype(BF16), st.astype(BF16)) * xi_ref[...] for oi, a, st in zip(o_inner, qr, before)]
    mavg = mavg_ref[...]
    mu = [_split_dot(x, mavg) for x in o]
    dev = [x - m for x, m in zip(o, mu)]
    var = [_split_dot(d * d, mavg) for d in dev]
    for c in chunks:
        normed = dev[c] * lax.rsqrt(var[c] + LN_EPS) * g_ref[...] + b_ref[...]
        gate = src(rows[c], 3 * B_WIDTH, 4 * B_WIDTH)
        o_ref[0, rows[c], :] = (jax.nn.silu(gate) * normed).astype(o_ref.dtype)


def _retention_consts():
    h_n, d, l_n = B_HEADS, HEAD_DIM, CHUNK
    f32 = np.float32
    log_gamma = np.log1p(-np.exp2(f32(-5.0) - np.arange(h_n, dtype=f32))).astype(f32)
    idx = np.arange(l_n, dtype=f32)
    diff = idx[:, None] - idx[None, :]
    decay_in = np.where(diff >= 0, np.exp(log_gamma[:, None, None] * np.maximum(diff, f32(0.0))), f32(0.0))
    xi = np.exp(log_gamma[:, None] * (idx + f32(1.0)))
    zeta = np.exp(log_gamma[:, None] * (f32(l_n - 1.0) - idx))
    chunk_decay = np.exp(log_gamma * f32(l_n))
    cols = np.arange(B_WIDTH)
    head_perm = (cols % (B_WIDTH // 2)) // (d // 2)
    head_std = cols // d
    dec = decay_in.reshape(h_n * l_n, l_n)
    zeta_t = zeta.T[:, head_perm]
    xi_t = xi.T[:, head_std]
    cd = chunk_decay[head_std][None, :]
    qm = head_perm[None, :] == np.arange(h_n)[:, None]
    vm = head_std[None, :] == np.arange(h_n)[:, None]
    bm = head_perm[:, None] == head_std[None, :]
    return [jnp.asarray(a.astype(f32)) for a in (dec, zeta_t, xi_t, cd, qm, vm, bm)]


def _rotary_tables(s):
    half = HEAD_DIM // 2
    f32 = np.float32
    inv = np.power(f32(ROPE_BASE), -np.arange(half, dtype=f32) / f32(half)).astype(f32)
    ang = np.arange(s, dtype=f32)[:, None] * inv[None, :]
    return (jnp.asarray(np.tile(np.cos(ang), (1, B_HEADS)).astype(f32)),
            jnp.asarray(np.tile(np.sin(ang), (1, B_HEADS)).astype(f32)))


def _compress_kernel(xk_ref, xv_ref, posk_ref, posv_ref, w1k_ref, w1v_ref, w2k_ref, w2vt_ref, kc_ref, vct_ref,
                     *, nseg):
    half = CMP_BLOCK // 2
    acc = [jnp.zeros((nseg, C_KV_WIDTH), F32) for _ in range(4)]
    for l in range(half):
        xk = xk_ref[0, pl.ds(l, nseg, stride=CMP_STRIDE), :]
        xv = xv_ref[0, pl.ds(l, nseg, stride=CMP_STRIDE), :]
        acc[0] += _dot((xk + posk_ref[l:l + 1, :]).astype(BF16), w1k_ref[l])
        acc[1] += _dot((xk + posk_ref[half + l:half + l + 1, :]).astype(BF16), w1k_ref[half + l])
        acc[2] += _dot((xv + posv_ref[l:l + 1, :]).astype(BF16), w1v_ref[l])
        acc[3] += _dot((xv + posv_ref[half + l:half + l + 1, :]).astype(BF16), w1v_ref[half + l])
    hk = jax.nn.gelu(acc[0] + pltpu.roll(acc[1], nseg - 1, 0))
    hv = jax.nn.gelu(acc[2] + pltpu.roll(acc[3], nseg - 1, 0))
    kc_ref[0] = _dot(hk.astype(BF16), w2k_ref[...])
    vct_ref[0] = _dot_nt(w2vt_ref[...], hv.astype(BF16))


def _compress(kvc, posk, posv, w1k, w1v, w2k, w2v):
    bsz, s, width = kvc.shape
    nseg = s // CMP_STRIDE
    c2 = lambda bi: (0, 0)
    c3 = lambda bi: (0, 0, 0)
    return pl.pallas_call(
        functools.partial(_compress_kernel, nseg=nseg),
        grid=(bsz,),
        in_specs=[
            pl.BlockSpec((1, s, C_KV_WIDTH), lambda bi: (bi, 0, 0)),
            pl.BlockSpec((1, s, C_KV_WIDTH), lambda bi: (bi, 0, 1)),
            pl.BlockSpec(posk.shape, c2),
            pl.BlockSpec(posv.shape, c2),
            pl.BlockSpec(w1k.shape, c3),
            pl.BlockSpec(w1v.shape, c3),
            pl.BlockSpec(w2k.shape, c2),
            pl.BlockSpec(w2v.shape, c2),
        ],
        out_specs=[
            pl.BlockSpec((1, nseg, C_KV_WIDTH), lambda bi: (bi, 0, 0)),
            pl.BlockSpec((1, C_KV_WIDTH, nseg), lambda bi: (bi, 0, 0)),
        ],
        out_shape=[
            jax.ShapeDtypeStruct((bsz, nseg, C_KV_WIDTH), F32),
            jax.ShapeDtypeStruct((bsz, C_KV_WIDTH, nseg), F32),
        ],
        compiler_params=_params(("arbitrary",), 32),
        name="nsa_compress",
    )(kvc, kvc, posk, posv, w1k, w1v, w2k, w2v)


def _blockdiag2(w):
    z = jnp.zeros_like(w)
    return jnp.concatenate([jnp.concatenate([w, z], axis=-1), jnp.concatenate([z, w], axis=-1)], axis=-2)


def _overlap_t(nseg, ns):
    nc = nseg - 1
    c0 = np.arange(nc)[None, :] * CMP_STRIDE
    s0 = np.arange(ns)[:, None] * SEL_BLOCK
    ov = np.clip(np.minimum(c0 + CMP_BLOCK, s0 + SEL_BLOCK) - np.maximum(c0, s0), 0, None) / CMP_BLOCK
    out = np.zeros((ns, nseg), np.float32)
    out[:, :nc] = ov
    return jnp.asarray(out, dtype=BF16)


def _nsa_kernel(qt_ref, gt_ref, ks_ref, vst_ref, kw_ref, vwt_ref, kc_ref, vct_ref, ovt_ref, e_ref, o_ref,
                qb_ref, m_ref, alpha_ref, acc_ref, res_ref, sbuf_ref, pbuf_ref, wsbuf_ref, wpbuf_ref,
                cbuf_ref, pcmp_ref, *, nseg, ns, n_sel, n_chunks, nb):
    i = pl.program_id(1)
    s0 = i * CHUNK
    width = C_HEADS * CHUNK
    blk_w = 2 * CHUNK
    n_blk = C_HEADS // 2
    blk_per_g = n_blk // C_KV_HEADS
    seqs = range(nb)

    @pl.when((pl.program_id(0) == 0) & (i == 0))
    def _():
        brow = lax.broadcasted_iota(jnp.int32, (C_KV_WIDTH, width), 0)
        for bb in seqs:
            qb_ref[bb, 0:C_KV_WIDTH, :] = jnp.zeros((C_KV_WIDTH, width), BF16)
            qb_ref[bb, 2 * C_KV_WIDTH:, :] = jnp.where(brow == 1, NEG, 0.0).astype(BF16)

    zero = jnp.zeros((HEAD_DIM, CHUNK), BF16)
    for bb in seqs:
        for h in range(C_HEADS):
            qh = qt_ref[bb, h * HEAD_DIM:(h + 1) * HEAD_DIM, :]
            blk = jnp.concatenate([qh, zero] if h < C_REP else [zero, qh], axis=0)
            qb_ref[bb, C_KV_WIDTH:2 * C_KV_WIDTH, h * CHUNK:(h + 1) * CHUNK] = blk
    ones_tile = jnp.where(lax.broadcasted_iota(jnp.int32, (2 * SUBLANES, CHUNK), 0) == 0, 1.0, 0.0).astype(BF16)

    krow = lax.broadcasted_iota(jnp.int32, (CHUNK, blk_w), 0)
    t_loc = lax.broadcasted_iota(jnp.int32, (CHUNK, blk_w), 1) % CHUNK
    tri_diag = jnp.where(krow <= t_loc, 0.0, NEG)
    tri_old = jnp.where(krow > t_loc, 0.0, NEG)

    def cmp_scores():
        for bb in seqs:
            cbuf_ref[bb] = _dot(kc_ref[bb].astype(BF16), qb_ref[bb, C_KV_WIDTH:2 * C_KV_WIDTH, :])

    def cmp_softmax():
        crow = lax.broadcasted_iota(jnp.int32, (nseg, blk_w), 0)
        c_t = s0 + lax.broadcasted_iota(jnp.int32, (nseg, blk_w), 1) % CHUNK
        cbias = jnp.where(crow * CMP_STRIDE + (CMP_BLOCK - 1) <= c_t, 0.0, NEG)
        imps = []
        for bb in seqs:
            psum = [None] * C_KV_HEADS
            for b in range(n_blk):
                g = b // blk_per_g
                cols = slice(b * blk_w, (b + 1) * blk_w)
                sc = cbuf_ref[bb, :, cols] + cbias
                mx = jnp.max(sc, axis=0, keepdims=True)
                e = jnp.exp2(sc - mx)
                den = jnp.sum(e, axis=0, keepdims=True)
                p = e * jnp.where(mx > 0.5 * NEG, 1.0 / den, 0.0)
                pcmp_ref[bb, :, cols] = p.astype(BF16)
                both = p[:, :CHUNK] + p[:, CHUNK:]
                psum[g] = both if psum[g] is None else psum[g] + both
            imps.append([_split_dot_left(ovt_ref[...], ps) for ps in psum])
        return imps

    def cmp_output():
        for bb in seqs:
            vct = vct_ref[bb].astype(BF16)
            for b in range(n_blk):
                g = b // blk_per_g
                cols = slice(b * blk_w, (b + 1) * blk_w)
                res_ref[bb, :, cols] = _dot(vct[g * HEAD_DIM:(g + 1) * HEAD_DIM, :], pcmp_ref[bb, :, cols])

    def select_blocks(all_imps):
        j = lax.broadcasted_iota(jnp.int32, (ns, CHUNK), 0)
        cur = (s0 + lax.broadcasted_iota(jnp.int32, (ns, CHUNK), 1)) // SEL_BLOCK
        forced = (j == 0) | (j == cur) | (j == cur - 1)
        future = j > cur
        slab_rows = lax.broadcasted_iota(jnp.int32, (SUBLANES, CHUNK), 0)
        for bb, g in [(bb, g) for bb in seqs for g in range(C_KV_HEADS)]:
            imps = all_imps[bb]
            imp = jnp.where(forced, BIG, jnp.where(future, NEG, imps[g]))
            slabs = [imp[v * SUBLANES:(v + 1) * SUBLANES, :] for v in range(ns // SUBLANES)]
            ranks = [jnp.zeros((SUBLANES, CHUNK), F32) for _ in slabs]
            for i2 in range(ns):
                r_i = imp[i2:i2 + 1, :]
                for v, slab in enumerate(slabs):
                    if (v + 1) * SUBLANES - 1 <= i2:
                        ranks[v] = ranks[v] + jnp.where(r_i > slab, 1.0, 0.0)
                    elif v * SUBLANES > i2:
                        ranks[v] = ranks[v] + jnp.where(r_i >= slab, 1.0, 0.0)
                    else:
                        tie = jnp.where(slab_rows > i2 - v * SUBLANES, 1.0, 0.0)
                        ranks[v] = ranks[v] + jnp.where(r_i > slab, 1.0, 0.0) + jnp.where(r_i == slab, tie, 0.0)
            rank = jnp.concatenate(ranks, axis=0)
            sel_bias = jnp.where(rank < n_sel, 0.0, NEG).astype(BF16)
            for r in range(C_REP):
                h = g * C_REP + r
                qb_ref[bb, 0:ns, h * CHUNK:(h + 1) * CHUNK] = sel_bias

    bufs = {0: (wsbuf_ref, wpbuf_ref), 1: (sbuf_ref, pbuf_ref)}

    def scores(br, k_ref, chunks, tiles, slot):
        sb, _ = bufs[br]
        for bb in seqs:
            parts = []
            for c, t in zip(chunks, tiles):
                kch = k_ref[bb, pl.ds(pl.multiple_of(c * CHUNK, CHUNK), CHUNK), :]
                parts.append(jnp.concatenate([kch, e_ref[t]] if br == 0 else [e_ref[t], kch], axis=1))
            keys = parts[0] if len(parts) == 1 else jnp.concatenate(parts, axis=0)
            if br == 0:
                rhs = qb_ref[bb, C_KV_WIDTH:3 * C_KV_WIDTH, :]
            else:
                rhs = qb_ref[bb, 0:2 * C_KV_WIDTH, :]
            sb[bb, slot, 0:len(parts) * CHUNK] = _dot(keys, rhs)

    def softmax(br, slot, tris):
        sb, pb = bufs[br]
        for bb in seqs:
            for b in range(n_blk):
                cols = slice(b * blk_w, (b + 1) * blk_w)
                parts = []
                for h, tri in enumerate(tris):
                    s = sb[bb, slot, h * CHUNK:(h + 1) * CHUNK, cols]
                    parts.append(s if tri is None else s + tri)
                m_old = m_ref[bb, br, :, cols]
                m_new = m_old
                for s in parts:
                    m_new = jnp.maximum(m_new, jnp.max(s, axis=0, keepdims=True))
                alpha_ref[bb, br, :, cols] = jnp.exp2(m_old - m_new)
                for h, s in enumerate(parts):
                    pb[bb, h * CHUNK:(h + 1) * CHUNK, cols] = jnp.exp2(s - m_new).astype(BF16)
                m_ref[bb, br, :, cols] = m_new

    def accumulate(br, vt_ref, chunks, first=False):
        _, pb = bufs[br]
        n = len(chunks)
        ones = ones_tile if n == 1 else jnp.concatenate([ones_tile] * n, axis=1)
        for bb in seqs:
            vts = [vt_ref[bb, :, pl.ds(pl.multiple_of(c * CHUNK, CHUNK), CHUNK)] for c in chunks]
            vt = vts[0] if n == 1 else jnp.concatenate(vts, axis=1)
            vone = [jnp.concatenate([vt[g * HEAD_DIM:(g + 1) * HEAD_DIM, :], ones], axis=0)
                    for g in range(C_KV_HEADS)]
            for b in range(n_blk):
                cols = slice(b * blk_w, (b + 1) * blk_w)
                pv = _dot(vone[b // blk_per_g], pb[bb, 0:n * CHUNK, cols])
                acc_ref[bb, br, :, cols] = (pv if first else
                                            alpha_ref[bb, br, :, cols] * acc_ref[bb, br, :, cols] + pv)

    n_back = WINDOW // CHUNK
    chunk = [jnp.maximum(i - n_back + w, 0) for w in range(n_back + 1)]
    tile = [jnp.where(i - n_back + w < 0, n_chunks + 1, n_chunks) for w in range(n_back + 1)]

    cmp_scores()
    m_ref[:, 0] = jnp.full((nb, 1, width), NEG, F32)
    scores(0, kw_ref, chunk[0:1], tile[0:1], 0)
    imps = cmp_softmax()
    softmax(0, 0, [tri_old])
    scores(0, kw_ref, chunk[1:3], tile[1:3], 1)
    accumulate(0, vwt_ref, chunk[0:1], first=True)
    softmax(0, 1, [None, None])
    scores(0, kw_ref, chunk[3:5], tile[3:5], 0)
    select_blocks(imps)

    m_ref[:, 1] = jnp.full((nb, 1, width), NEG, F32)
    alpha_ref[:, 1] = jnp.ones((nb, 1, width), F32)
    for bb in seqs:
        acc_ref[bb, 1] = jnp.zeros(acc_ref.shape[2:], F32)
        pbuf_ref[bb] = jnp.zeros(pbuf_ref.shape[1:], BF16)
    scores(1, ks_ref, [0], [0], 0)

    def sel_stage(k, slot, slot_next):
        accumulate(1, vst_ref, [jnp.maximum(k - 1, 0)])
        softmax(1, slot, [None])
        scores(1, ks_ref, [k + 1], [k + 1], slot_next)

    def sel_pair(kk, carry):
        sel_stage(2 * kk, 0, 1)
        sel_stage(2 * kk + 1, 1, 0)
        return carry

    lax.fori_loop(0, i // 2, sel_pair, 0)

    @pl.when(i % 2 == 1)
    def _():
        sel_stage(i - 1, 0, 1)

    accumulate(1, vst_ref, [jnp.maximum(i - 1, 0)])
    accumulate(0, vwt_ref, chunk[1:3])
    softmax(1, i % 2, [tri_diag])
    softmax(0, 0, [None, tri_diag])
    accumulate(1, vst_ref, [i])
    accumulate(0, vwt_ref, chunk[3:5])
    cmp_output()

    for bb in seqs:
        gates = jax.nn.sigmoid(gt_ref[bb])
        den = [acc_ref[bb, br, HEAD_DIM:HEAD_DIM + 1, :] for br in (0, 1)]
        inv = [jnp.where(x > 0.0, 1.0 / x, 0.0) for x in den]
        for pair in range(C_HEADS // 2):
            pieces = []
            for h in (2 * pair, 2 * pair + 1):
                cs = slice(h * CHUNK, (h + 1) * CHUNK)
                pieces.append(gates[3 * h:3 * h + 1, :] * res_ref[bb, :, cs]
                              + (gates[3 * h + 1:3 * h + 2, :] * inv[1][:, cs]) * acc_ref[bb, 1, 0:HEAD_DIM, cs]
                              + (gates[3 * h + 2:3 * h + 3, :] * inv[0][:, cs]) * acc_ref[bb, 0, 0:HEAD_DIM, cs])
            o_ref[bb, :, pair * 2 * HEAD_DIM:(pair + 1) * 2 * HEAD_DIM] = (
                jnp.concatenate(pieces, axis=0).T.astype(o_ref.dtype))


def _route_tiles(n_chunks, ns):
    e = np.zeros((n_chunks + 2, CHUNK, C_KV_WIDTH), np.float32)
    r = np.arange(CHUNK)
    for c in range(n_chunks):
        e[c, r, 2 * c + r // SEL_BLOCK] = 1.0
    e[n_chunks, :, 0] = 1.0
    e[n_chunks + 1, :, 1] = 1.0
    return jnp.asarray(e, dtype=BF16)


def _nsa(ptq, gt, ksw, kc, vct, ovt, route):
    bsz, s, _ = ksw.shape
    nseg = s // CMP_STRIDE
    ns = s // SEL_BLOCK
    n_chunks = s // CHUNK
    width = C_HEADS * CHUNK
    nb = next(n for n in NSA_SEQS if bsz % n == 0)
    return pl.pallas_call(
        functools.partial(_nsa_kernel, nseg=nseg, ns=ns, n_sel=min(N_SEL, ns), n_chunks=n_chunks, nb=nb),
        grid=(bsz // nb, n_chunks),
        in_specs=[
            pl.BlockSpec((nb, C_WIDTH, CHUNK), lambda b, i: (b, PT_Q // C_WIDTH, i)),
            pl.BlockSpec((nb, gt.shape[1], CHUNK), lambda b, i: (b, 0, i)),
            pl.BlockSpec((nb, s, C_KV_WIDTH), lambda b, i: (b, 0, 0)),
            pl.BlockSpec((nb, C_KV_WIDTH, s), lambda b, i: (b, PT_VS // C_KV_WIDTH, 0)),
            pl.BlockSpec((nb, s, C_KV_WIDTH), lambda b, i: (b, 0, 1)),
            pl.BlockSpec((nb, C_KV_WIDTH, s), lambda b, i: (b, PT_VW // C_KV_WIDTH, 0)),
            pl.BlockSpec((nb, nseg, C_KV_WIDTH), lambda b, i: (b, 0, 0)),
            pl.BlockSpec((nb, C_KV_WIDTH, nseg), lambda b, i: (b, 0, 0)),
            pl.BlockSpec(ovt.shape, lambda b, i: (0, 0)),
            pl.BlockSpec(route.shape, lambda b, i: (0, 0, 0)),
        ],
        out_specs=pl.BlockSpec((nb, CHUNK, C_WIDTH), lambda b, i: (b, i, 0)),
        out_shape=jax.ShapeDtypeStruct((bsz, s, C_WIDTH), BF16),
        scratch_shapes=[
            pltpu.VMEM((nb, 3 * C_KV_WIDTH, width), BF16),
            pltpu.VMEM((nb, 2, 1, width), F32),
            pltpu.VMEM((nb, 2, 1, width), F32),
            pltpu.VMEM((nb, 2, HEAD_DIM + 2 * SUBLANES, width), F32),
            pltpu.VMEM((nb, HEAD_DIM, width), F32),
            pltpu.VMEM((nb, 2, CHUNK, width), F32),
            pltpu.VMEM((nb, CHUNK, width), BF16),
            pltpu.VMEM((nb, 2, 2 * CHUNK, width), F32),
            pltpu.VMEM((nb, 2 * CHUNK, width), BF16),
            pltpu.VMEM((nb, nseg, width), F32),
            pltpu.VMEM((nb, nseg, width), BF16),
        ],
        compiler_params=_params(("arbitrary", "arbitrary"), 48),
        name="nsa_attention",
    )(ptq, gt, ksw, ptq, ksw, ptq, kc, vct, ovt, route)


def _tail_kernel(ya_ref, yb_ref, yc_ref, x_ref, mod_ref, wo_ref, g1_ref, b1_ref,
                 wup_ref, cw_ref, cb_ref, wdn_ref, g2_ref, b2_ref, o_ref,
                 work_ref, carry_ref, x1_ref, h_ref, act_ref, *, d, dff, cwid, tm, alpha):
    @pl.when(pl.program_id(1) == 0)
    def _():
        carry_ref[...] = jnp.zeros_like(carry_ref)

    y1 = _dot(ya_ref[0], wo_ref[0:A_WIDTH, :])
    y1 += _dot(yb_ref[0], wo_ref[A_WIDTH:A_WIDTH + B_WIDTH, :])
    y1 += _dot(yc_ref[0], wo_ref[A_WIDTH + B_WIDTH:, :])
    x1 = _row_ln(alpha * x_ref[0] + mod_ref[0, :, 2 * d:3 * d] * y1, g1_ref[...], b1_ref[...])
    x1_ref[...] = x1

    sh = mod_ref[0, :, 3 * d:4 * d]
    sc = mod_ref[0, :, 4 * d:5 * d]
    gate = mod_ref[0, :, 5 * d:6 * d]
    h_ref[...] = (x1 * (1.0 + sc) + sh).astype(BF16)
    pad = SUBLANES
    n_chunks = dff // cwid

    def up(ci):
        slot = ci % 2
        for part in range(2):
            cs = slice(part * dff + ci * cwid, part * dff + (ci + 1) * cwid)
            work_ref[slot, part, 0:pad, :] = carry_ref[:, cs]
            work_ref[slot, part, pad:pad + tm, :] = _dot(h_ref[...], wup_ref[:, cs])

    def conv_act(ci):
        slot = ci % 2
        conv = []
        for part in range(2):
            cs = slice(part * dff + ci * cwid, part * dff + (ci + 1) * cwid)
            buf = work_ref[slot, part]
            carry_ref[:, cs] = buf[tm:tm + pad, :]
            a = buf[pad:, :]
            a1 = pltpu.roll(buf, 1, 0)[pad:, :]
            a2 = pltpu.roll(buf, 2, 0)[pad:, :]
            conv.append(cw_ref[0:1, cs] * a2 + cw_ref[1:2, cs] * a1 + cw_ref[2:3, cs] * a + cb_ref[:, cs])
        half = 0.5 * conv[0]
        silu = half + half * jnp.tanh(half)
        act_ref[:, ci * cwid:(ci + 1) * cwid] = (silu * conv[1]).astype(BF16)

    split = (n_chunks // 2) * cwid
    up(0)
    y = None
    for ci in range(n_chunks):
        if ci + 1 < n_chunks:
            up(ci + 1)
        conv_act(ci)
        if (ci + 1) * cwid == split:
            y = _dot(act_ref[:, :split], wdn_ref[:split, :])
    y = y + _dot(act_ref[:, split:], wdn_ref[split:, :])
    o_ref[0] = _row_ln(alpha * x1_ref[...] + gate * y, g2_ref[...], b2_ref[...])


def _tail(ya, yb, yc, x, mod, layer, wo, g1, b1, wup, cw, cb, wdn, g2, b2, alpha, tm=TOKEN_TILE, cwid=FFN_CHUNK):
    bsz, s, d = x.shape
    dff = wdn.shape[0]
    const = lambda bi, j: (0, 0)
    tile = lambda w: pl.BlockSpec((1, tm, w), lambda bi, j: (bi, j, 0))
    weight = lambda w: pl.BlockSpec(w.shape, const, pipeline_mode=pl.Buffered(1))
    layered = lambda a: pl.BlockSpec((None,) + a.shape[1:], lambda bi, j: (layer,) + (0,) * (a.ndim - 1))
    return pl.pallas_call(
        functools.partial(_tail_kernel, d=d, dff=dff, cwid=cwid, tm=tm, alpha=alpha),
        grid=(bsz, s // tm),
        in_specs=[
            tile(A_WIDTH), tile(B_WIDTH), tile(C_WIDTH), tile(d),
            pl.BlockSpec((None, 1, 1, 6 * d), lambda bi, j: (layer, bi, 0, 0)),
            weight(wo), layered(g1), layered(b1),
            weight(wup), layered(cw), layered(cb), weight(wdn), layered(g2), layered(b2),
        ],
        out_specs=tile(d),
        out_shape=jax.ShapeDtypeStruct((bsz, s, d), F32),
        scratch_shapes=[
            pltpu.VMEM((2, 2, tm + SUBLANES, cwid), F32),
            pltpu.VMEM((SUBLANES, 2 * dff), F32),
            pltpu.VMEM((tm, d), F32),
            pltpu.VMEM((tm, d), BF16),
            pltpu.VMEM((tm, dff), BF16),
        ],
        compiler_params=_params(("arbitrary", "arbitrary"), 52),
        name="layer_tail",
    )(ya, yb, yc, x, mod, wo, g1, b1, wup, cw, cb, wdn, g2, b2)


def _inproj_weights(w):
    cols = np.arange(B_WIDTH)
    half = cols // (B_WIDTH // 2)
    perm = ((cols % (B_WIDTH // 2)) // (HEAD_DIM // 2)) * HEAD_DIM + half * (HEAD_DIM // 2) + cols % (HEAD_DIM // 2)
    scale = np.ones((1, w.shape[1]), np.float32)
    scale[:, 1536:2048] = HEAD_DIM ** -0.5 * LOG2E
    w = (w * scale).astype(BF16)
    za = w[:, 0:512]
    qb, kb, vb, gb = (w[:, 512 + i * B_WIDTH:512 + (i + 1) * B_WIDTH] for i in range(4))
    qc = w[:, 1536:2048]
    kcm, vcm, ksl, vsl, kwn, vwn = (w[:, 2048 + i * C_KV_WIDTH:2048 + (i + 1) * C_KV_WIDTH] for i in range(6))
    gc = w[:, 2816:2840]
    wn = jnp.concatenate([qb[:, perm], kb[:, perm], vb, gb, za, ksl, kwn, kcm, vcm], axis=1)
    pad = jnp.zeros((w.shape[0], PT_ROWS - PT_G - gc.shape[1]), w.dtype)
    wt = jnp.concatenate([qc, vsl, vwn, gc, pad], axis=1).T
    return wn, wt


def kernel(x, c, w_ada, b_ada, w_in, a_ln_g, a_ln_b, a_ws, a_bs, b_gn_g, b_gn_b, c_pos_k, c_w1_k, c_w2_k,
           c_pos_v, c_w1_v, c_w2_v, w_out, ln1_g, ln1_b, w_up, conv_w, conv_b, w_down, ln2_g, ln2_b):
    depth = w_in.shape[0]
    bsz, s, d = x.shape
    alpha = (2 * depth) ** 0.25
    nseg = s // CMP_STRIDE
    ns = s // SEL_BLOCK

    lanes = np.arange(A_WIDTH)
    mavg = jnp.asarray((lanes[:, None] // HEAD_DIM == lanes[None, :] // HEAD_DIM) / HEAD_DIM, dtype=BF16)
    gm = jnp.asarray((lanes[None, :] // HEAD_DIM == np.arange(A_GROUPS)[:, None]).astype(np.float32))
    ret_consts = _retention_consts()
    cos, sin = _rotary_tables(s)
    ovt = _overlap_t(nseg, ns)
    route = _route_tiles(s // CHUNK, ns)

    row = lambda a: a.reshape(depth, 1, -1)
    mod = _ada_mod(c, w_ada, b_ada).reshape(depth, bsz, 1, 6 * d)
    gn_g, gn_b, al_g, al_b = row(b_gn_g), row(b_gn_b), row(a_ln_g), row(a_ln_b)
    g1, b1, g2, b2, cb = row(ln1_g), row(ln1_b), row(ln2_g), row(ln2_b), row(conv_b)
    for l in range(depth):
        wn, wt = _inproj_weights(w_in[l])
        wcat = jnp.transpose(a_ws[l], (1, 0, 2)).reshape(CHUNK, A_GROUPS * CHUNK)
        bias = jnp.repeat(a_bs[l].T, HEAD_DIM, axis=1)
        ya, yb, ksw, kvc, ptq, gt, wo_bf, wup_bf, wdn_bf = _inproj(
            x, mod, wn, wt, cos, sin, ret_consts, mavg, gn_g, gn_b, wcat, bias, al_g, al_b, gm,
            l, w_out, w_up, w_down)

        w1k = _blockdiag2(c_w1_k[l].reshape(CMP_BLOCK, HEAD_DIM, HEAD_DIM)).astype(BF16)
        w1v = _blockdiag2(c_w1_v[l].reshape(CMP_BLOCK, HEAD_DIM, HEAD_DIM)).astype(BF16)
        kc, vct = _compress(kvc, jnp.tile(c_pos_k[l], (1, C_KV_HEADS)), jnp.tile(c_pos_v[l], (1, C_KV_HEADS)),
                            w1k, w1v, _blockdiag2(c_w2_k[l]).astype(BF16), _blockdiag2(c_w2_v[l]).T.astype(BF16))
        yc = _nsa(ptq, gt, ksw, kc, vct, ovt, route)

        x = _tail(ya, yb, yc, x, mod, l, wo_bf, g1, b1, wup_bf, conv_w, cb, wdn_bf, g2, b2, alpha)
    return x
```
